```python
import math
import jax, jax.numpy as jnp
from jax import lax
import numpy as np

D_MODEL = 2048
BATCH = 2
SEQ = 4096
DEPTH = 1

MIX_WIDTH = D_MODEL
POOL_WIDTH = MIX_WIDTH // 2
ATTN_WIDTH = MIX_WIDTH - POOL_WIDTH
POOL_WINDOWS = (2, 4, 8, 16)
N_POOL_GROUPS = len(POOL_WINDOWS)
POOL_GROUP_DIM = POOL_WIDTH // N_POOL_GROUPS
DIFF_HEAD_DIM = 64
DIFF_V_DIM = 2 * DIFF_HEAD_DIM
N_DIFF_HEADS = ATTN_WIDTH // DIFF_V_DIM
QK_WIDTH = N_DIFF_HEADS * 2 * DIFF_HEAD_DIM
IN_WIDTH = POOL_WIDTH + 2 * QK_WIDTH + ATTN_WIDTH
ROPE_THETA = 500000.0
ROT_DIM = DIFF_HEAD_DIM // 4
D_FF = int(math.ceil(8 * D_MODEL / 3 / 256) * 256)
Q_BLOCK = 128
NORM_EPS = 1e-6
NEG_INF = -1e30

kernel_name = "hybrid_pool_diffattn_block"


def rms_norm(x, g):
    xf = x.astype(jnp.float32)
    y = xf * lax.rsqrt(jnp.mean(xf * xf, axis=-1, keepdims=True) + NORM_EPS)
    return (y * g.astype(jnp.float32)).astype(x.dtype)


def lambda_init_fn(layer_idx):
    return 0.8 - 0.6 * math.exp(-0.3 * layer_idx)


def apply_partial_rope(t, positions):
    half = ROT_DIM // 2
    inv_freq = ROPE_THETA ** (-jnp.arange(0, ROT_DIM, 2, dtype=jnp.float32) / ROT_DIM)
    ang = positions.astype(jnp.float32)[..., None] * inv_freq
    cos = jnp.cos(ang)[:, :, None, None, :]
    sin = jnp.sin(ang)[:, :, None, None, :]
    tf = t.astype(jnp.float32)
    x1 = tf[..., :half]
    x2 = tf[..., half:ROT_DIM]
    rot = jnp.concatenate([x1 * cos - x2 * sin, x2 * cos + x1 * sin], axis=-1)
    return jnp.concatenate([rot, tf[..., ROT_DIM:]], axis=-1).astype(t.dtype)


def causal_multiscale_pool(u, pool_w, pool_scale):
    B, S, _ = u.shape
    ug = u.reshape(B, S, N_POOL_GROUPS, POOL_GROUP_DIM)
    ugf = ug.astype(jnp.float32)
    cs = jnp.cumsum(ugf, axis=1)
    t = jnp.arange(S)
    means = []
    for g, w in enumerate(POOL_WINDOWS):
        c = cs[:, :, g]
        prev = jnp.pad(c, ((0, 0), (w, 0), (0, 0)))[:, :S]
        cnt = jnp.minimum(t + 1, w).astype(jnp.float32)[None, :, None]
        means.append((c - prev) / cnt)
    mean = jnp.stack(means, axis=2)
    pooled = (mean - ugf).astype(u.dtype)
    mixed = jnp.einsum('bsgc,gcd->bsgd', pooled, pool_w)
    return mixed.reshape(B, S, POOL_WIDTH) * pool_scale


def differential_attention(q, k, v, lam):
    B, S = q.shape[0], q.shape[1]
    n_blocks = S // Q_BLOCK
    scale = DIFF_HEAD_DIM ** -0.5
    k_idx = jnp.arange(S)

    def block(i):
        start = i * Q_BLOCK
        qb = lax.dynamic_slice_in_dim(q, start, Q_BLOCK, axis=1)
        s = jnp.einsum('bqhcd,bkhcd->bhcqk', qb, k).astype(jnp.float32) * scale
        q_idx = start + jnp.arange(Q_BLOCK)
        mask = k_idx[None, :] <= q_idx[:, None]
        s = jnp.where(mask, s, NEG_INF)
        p = jax.nn.softmax(s, axis=-1)
        diff = p[:, :, 0] - lam * p[:, :, 1]
        return jnp.einsum('bhqk,bkhe->bqhe', diff.astype(v.dtype), v)

    out = lax.map(block, jnp.arange(n_blocks))
    out = jnp.transpose(out, (1, 0, 2, 3, 4)).reshape(B, S, N_DIFF_HEADS, DIFF_V_DIM)
    return out


def setup_inputs(seed: int = 0) -> dict:
    key = jax.random.key(seed)
    ks = jax.random.split(key, 20)
    f32 = jnp.float32

    def normal(k, shape, scale):
        return jax.random.normal(k, shape, f32) * scale

    def gain(k, shape):
        return 1.0 + 0.02 * jax.random.normal(k, shape, f32)

    x = jax.random.normal(ks[0], (BATCH, SEQ, D_MODEL), f32)
    positions = jnp.broadcast_to(jnp.arange(SEQ, dtype=jnp.int32)[None, :], (BATCH, SEQ))
    return {
        "x": x,
        "positions": positions,
        "pre_mix_norm": gain(ks[1], (DEPTH, D_MODEL)),
        "post_mix_norm": gain(ks[2], (DEPTH, D_MODEL)),
        "w_in": normal(ks[3], (DEPTH, D_MODEL, IN_WIDTH), D_MODEL ** -0.5),
        "pool_w": normal(ks[4], (DEPTH, N_POOL_GROUPS, POOL_GROUP_DIM, POOL_GROUP_DIM), POOL_GROUP_DIM ** -0.5),
        "pool_scale": gain(ks[5], (DEPTH, POOL_WIDTH)),
        "lam_q1": normal(ks[6], (DEPTH, DIFF_HEAD_DIM), 0.1),
        "lam_k1": normal(ks[7], (DEPTH, DIFF_HEAD_DIM), 0.1),
        "lam_q2": normal(ks[8], (DEPTH, DIFF_HEAD_DIM), 0.1),
        "lam_k2": normal(ks[9], (DEPTH, DIFF_HEAD_DIM), 0.1),
        "subln_w": gain(ks[10], (DEPTH, DIFF_V_DIM)),
        "w_out": normal(ks[11], (DEPTH, MIX_WIDTH, D_MODEL), MIX_WIDTH ** -0.5),
        "pre_ffn_norm": gain(ks[12], (DEPTH, D_MODEL)),
        "post_ffn_norm": gain(ks[13], (DEPTH, D_MODEL)),
        "w_gate": normal(ks[14], (DEPTH, D_MODEL, D_FF), D_MODEL ** -0.5),
        "w_up": normal(ks[15], (DEPTH, D_MODEL, D_FF), D_MODEL ** -0.5),
        "w_down": normal(ks[16], (DEPTH, D_FF, D_MODEL), D_FF ** -0.5),
    }


def reference(x, positions, pre_mix_norm, post_mix_norm, w_in, pool_w, pool_scale,
              lam_q1, lam_k1, lam_q2, lam_k2, subln_w, w_out,
              pre_ffn_norm, post_ffn_norm, w_gate, w_up, w_down):
    B, S, _ = x.shape
    h = x
    for l in range(DEPTH):
        lambda_init = lambda_init_fn(l)
        hn = rms_norm(h, pre_mix_norm[l])
        proj = hn @ w_in[l]
        u_pool = proj[..., :POOL_WIDTH]
        q = proj[..., POOL_WIDTH:POOL_WIDTH + QK_WIDTH].reshape(B, S, N_DIFF_HEADS, 2, DIFF_HEAD_DIM)
        k = proj[..., POOL_WIDTH + QK_WIDTH:POOL_WIDTH + 2 * QK_WIDTH].reshape(B, S, N_DIFF_HEADS, 2, DIFF_HEAD_DIM)
        v = proj[..., POOL_WIDTH + 2 * QK_WIDTH:].reshape(B, S, N_DIFF_HEADS, DIFF_V_DIM)

        pool_out = causal_multiscale_pool(u_pool, pool_w[l], pool_scale[l])

        q = apply_partial_rope(q, positions)
        k = apply_partial_rope(k, positions)
        lam = (jnp.exp(jnp.sum(lam_q1[l].astype(jnp.float32) * lam_k1[l].astype(jnp.float32)))
               - jnp.exp(jnp.sum(lam_q2[l].astype(jnp.float32) * lam_k2[l].astype(jnp.float32)))
               + lambda_init)
        attn = differential_attention(q, k, v, lam)
        attn = rms_norm(attn, subln_w[l]) * (1.0 - lambda_init)
        attn_out = attn.reshape(B, S, ATTN_WIDTH)

        mixed = jnp.concatenate([pool_out, attn_out], axis=-1) @ w_out[l]
        h = h + rms_norm(mixed, post_mix_norm[l])

        hn = rms_norm(h, pre_ffn_norm[l])
        ff = (jax.nn.silu(hn @ w_gate[l]) * (hn @ w_up[l])) @ w_down[l]
        h = h + rms_norm(ff, post_ffn_norm[l])
    return h
```

```python
import functools
import math

import jax
import jax.numpy as jnp
from jax import lax
from jax.experimental import pallas as pl
from jax.experimental.pallas import tpu as pltpu

F32 = jnp.float32
BF16 = jnp.bfloat16

POOL_WINDOWS = (2, 4, 8, 16)
DIFF_HEAD_DIM = 64
DIFF_V_DIM = 2 * DIFF_HEAD_DIM
ROPE_THETA = 500000.0
ROT_DIM = DIFF_HEAD_DIM // 4
ROT_HALF = ROT_DIM // 2
NORM_EPS = 1e-6
NEG_INF = -1e30

LANES = 128
POOL_HALO = 32
VMEM_LIMIT_BYTES = 56 * 1024 * 1024

TM_IN = 512
TQ = 512
TK = 512
TM_OUT = 512
TM_FFN = 1024
TF_FFN = 512
N_CHUNK = 512


def _lambda_init(layer_idx):
    return 0.8 - 0.6 * math.exp(-0.3 * layer_idx)


def _rms_norm(xf, g):
    ms = jnp.mean(xf * xf, axis=-1, keepdims=True)
    return xf * lax.rsqrt(ms + NORM_EPS) * g


def _in_proj_kernel(x_ref, pos_ref, invf_ref, g_ref, w_ref, pw_ref, ps_ref,
                    pool_ref, q_ref, k_ref, v_ref, hn_ref, carry_ref, *, seq, pool_width, qk_width):
    tm = x_ref.shape[0]
    i = pl.program_id(0)
    tiles_per_seq = seq // tm
    ti = i % tiles_per_seq

    hn_ref[...] = _rms_norm(x_ref[...], g_ref[...]).astype(BF16)

    @pl.when(ti == 0)
    def _():
        carry_ref[...] = jnp.zeros_like(carry_ref)

    gdim = pool_width // len(POOL_WINDOWS)
    t_in_seq = ti * tm + lax.broadcasted_iota(jnp.int32, (tm, 1), 0)
    for g, w in enumerate(POOL_WINDOWS):
        c0 = g * gdim
        u = jnp.dot(hn_ref[...], w_ref[:, c0:c0 + gdim], preferred_element_type=F32)
        ext = jnp.concatenate([carry_ref[g], u], axis=0)
        carry_ref[g] = u[tm - POOL_HALO:, :]
        lvl, off, k = ext, 0, 1
        while k < w:
            new_off = min(off + 8, POOL_HALO)
            cur = lvl[new_off - off:, :]
            shifted = lvl[new_off - off - k: lvl.shape[0] - k, :]
            lvl, off, k = cur + shifted, new_off, 2 * k
        wsum = lvl[POOL_HALO - off:, :]
        cnt = jnp.minimum(t_in_seq + 1, w).astype(F32)
        pooled = (wsum / cnt - u).astype(BF16)
        mixed = jnp.dot(pooled, pw_ref[g], preferred_element_type=F32)
        pool_ref[:, c0:c0 + gdim] = (mixed * ps_ref[:, c0:c0 + gdim]).astype(BF16)

    lane = lax.broadcasted_iota(jnp.int32, (1, LANES), 1) % DIFF_HEAD_DIM
    ang = pos_ref[...].astype(F32) * invf_ref[...]
    cos, sin = jnp.cos(ang), jnp.sin(ang)
    coef_self = jnp.where(lane < ROT_DIM, cos, 1.0)
    coef_lo = jnp.where((lane >= ROT_HALF) & (lane < ROT_DIM), sin, 0.0)
    coef_hi = jnp.where(lane < ROT_HALF, -sin, 0.0)

    def rope_store(out_ref, col0, scale):
        for c in range(0, qk_width, N_CHUNK):
            t = jnp.dot(hn_ref[...], w_ref[:, col0 + c:col0 + c + N_CHUNK], preferred_element_type=F32)
            for h in range(N_CHUNK // LANES):
                th = t[:, h * LANES:(h + 1) * LANES]
                r = (th * coef_self + pltpu.roll(th, ROT_HALF, 1) * coef_lo
                     + pltpu.roll(th, LANES - ROT_HALF, 1) * coef_hi)
                if scale != 1.0:
                    r = r * scale
                out_ref[:, c + h * LANES:c + (h + 1) * LANES] = r.astype(BF16)

    rope_store(q_ref, pool_width, DIFF_HEAD_DIM ** -0.5)
    rope_store(k_ref, pool_width + qk_width, 1.0)

    v0 = pool_width + 2 * qk_width
    for c in range(0, v_ref.shape[1], N_CHUNK):
        v_ref[:, c:c + N_CHUNK] = jnp.dot(hn_ref[...], w_ref[:, v0 + c:v0 + c + N_CHUNK],
                                          preferred_element_type=F32).astype(BF16)


def _in_proj(x2, pos2, inv_freq, g, w_in, pool_w, pool_scale, *, seq, pool_width, qk_width, attn_width):
    n, d = x2.shape
    tm = TM_IN
    n_groups = len(POOL_WINDOWS)
    gdim = pool_width // n_groups
    const = lambda i: (0, 0)
    row = lambda i: (i, 0)
    kern = functools.partial(_in_proj_kernel, seq=seq, pool_width=pool_width, qk_width=qk_width)
    return pl.pallas_call(
        kern,
        grid=(n // tm,),
        in_specs=[
            pl.BlockSpec((tm, d), row),
            pl.BlockSpec((tm, 1), row),
            pl.BlockSpec((1, LANES), const),
            pl.BlockSpec((1, d), const),
            pl.BlockSpec(w_in.shape, const, pipeline_mode=pl.Buffered(1)),
            pl.BlockSpec(pool_w.shape, lambda i: (0, 0, 0), pipeline_mode=pl.Buffered(1)),
            pl.BlockSpec((1, pool_width), const),
        ],
        out_specs=[
            pl.BlockSpec((tm, pool_width), row),
            pl.BlockSpec((tm, qk_width), row),
            pl.BlockSpec((tm, qk_width), row),
            pl.BlockSpec((tm, attn_width), row),
        ],
        out_shape=[
            jax.ShapeDtypeStruct((n, pool_width), BF16),
            jax.ShapeDtypeStruct((n, qk_width), BF16),
            jax.ShapeDtypeStruct((n, qk_width), BF16),
            jax.ShapeDtypeStruct((n, attn_width), BF16),
        ],
        scratch_shapes=[
            pltpu.VMEM((tm, d), BF16),
            pltpu.VMEM((n_groups, POOL_HALO, gdim), F32),
        ],
        compiler_params=pltpu.CompilerParams(
            dimension_semantics=("arbitrary",), vmem_limit_bytes=VMEM_LIMIT_BYTES),
        name="in_proj",
    )(x2, pos2, inv_freq, g, w_in, pool_w, pool_scale)


def _attn_kernel(lam_ref, sw_ref, q_ref, k_ref, v_ref, o_ref, m_ref, l_ref, acc_ref, *, lambda_init):
    tq = q_ref.shape[0]
    qi = pl.program_id(2)

    q = q_ref[...]
    lane = lax.broadcasted_iota(jnp.int32, q.shape, 1)
    zero = jnp.zeros_like(q)
    q2 = jnp.concatenate([jnp.where(lane < DIFF_HEAD_DIM, q, zero),
                          jnp.where(lane >= DIFF_HEAD_DIM, q, zero)], axis=0)

    m_ref[...] = jnp.full_like(m_ref, NEG_INF)
    l_ref[...] = jnp.zeros_like(l_ref)
    acc_ref[...] = jnp.zeros_like(acc_ref)

    def step(j, masked):
        start = pl.multiple_of(j * TK, TK)
        kj = k_ref[pl.ds(start, TK), :]
        vj = v_ref[pl.ds(start, TK), :]
        s = lax.dot_general(q2, kj, (((1,), (1,)), ((), ())), preferred_element_type=F32)
        if masked:
            r = lax.broadcasted_iota(jnp.int32, (tq, TK), 0)
            c = lax.broadcasted_iota(jnp.int32, (tq, TK), 1)
            keep = jnp.concatenate([c <= r, c <= r], axis=0)
            s = jnp.where(keep, s, NEG_INF)
        chunks = [s[:, c0:c0 + LANES] for c0 in range(0, TK, LANES)]
        cmax = functools.reduce(jnp.maximum, chunks)
        m_prev = m_ref[...]
        m_new = jnp.maximum(m_prev, jnp.max(cmax, axis=-1, keepdims=True))
        alpha = jnp.exp(m_prev - m_new)
        p = [jnp.exp(ch - m_new) for ch in chunks]
        l_ref[...] = alpha * l_ref[...] + functools.reduce(jnp.add, p)
        pv = jnp.dot(jnp.concatenate([x.astype(BF16) for x in p], axis=1), vj,
                     preferred_element_type=F32)
        acc_ref[...] = alpha * acc_ref[...] + pv
        m_ref[...] = m_new

    def body(j, carry):
        step(j, masked=False)
        return carry

    lax.fori_loop(0, qi, body, 0)
    step(qi, masked=True)

    lam_v = lam_ref[...]
    lam = (jnp.exp(jnp.sum(lam_v[0:1] * lam_v[1:2], keepdims=True))
           - jnp.exp(jnp.sum(lam_v[2:3] * lam_v[3:4], keepdims=True)) + lambda_init)
    o = acc_ref[...] / jnp.sum(l_ref[...], axis=-1, keepdims=True)
    attn = o[:tq] - lam * o[tq:]
    o_ref[...] = (_rms_norm(attn, sw_ref[...]) * (1.0 - lambda_init)).astype(BF16)


def _attention(q, k, v, lam_vecs, subln_w, *, batch, seq, n_heads, lambda_init):
    n, width = q.shape
    nq = seq // TQ
    assert TQ == TK
    kern = functools.partial(_attn_kernel, lambda_init=lambda_init)
    q_map = lambda b, h, i: (b * nq + i, h)
    kv_map = lambda b, h, i: (b, h)
    return pl.pallas_call(
        kern,
        grid=(batch, n_heads, nq),
        in_specs=[
            pl.BlockSpec(lam_vecs.shape, lambda b, h, i: (0, 0)),
            pl.BlockSpec((1, DIFF_V_DIM), lambda b, h, i: (0, 0)),
            pl.BlockSpec((TQ, DIFF_V_DIM), q_map),
            pl.BlockSpec((seq, DIFF_V_DIM), kv_map),
            pl.BlockSpec((seq, DIFF_V_DIM), kv_map),
        ],
        out_specs=pl.BlockSpec((TQ, DIFF_V_DIM), q_map),
        out_shape=jax.ShapeDtypeStruct((n, width), BF16),
        scratch_shapes=[
            pltpu.VMEM((2 * TQ, LANES), F32),
            pltpu.VMEM((2 * TQ, LANES), F32),
            pltpu.VMEM((2 * TQ, DIFF_V_DIM), F32),
        ],
        compiler_params=pltpu.CompilerParams(
            dimension_semantics=("arbitrary", "arbitrary", "arbitrary"), vmem_limit_bytes=VMEM_LIMIT_BYTES),
        name="diff_attention",
    )(lam_vecs, subln_w, q, k, v)


def _out_proj_kernel(pool_ref, attn_ref, x_ref, w_ref, g_ref, h_ref, mix_ref):
    pw = pool_ref.shape[1]
    d = x_ref.shape[1]
    for c in range(0, d, N_CHUNK):
        mix_ref[:, c:c + N_CHUNK] = (
            jnp.dot(pool_ref[...], w_ref[:pw, c:c + N_CHUNK], preferred_element_type=F32)
            + jnp.dot(attn_ref[...], w_ref[pw:, c:c + N_CHUNK], preferred_element_type=F32))
    h_ref[...] = x_ref[...] + _rms_norm(mix_ref[...], g_ref[...])


def _out_proj(pool_out, attn_out, x2, w_out, g):
    n, d = x2.shape
    tm = TM_OUT
    row = lambda i: (i, 0)
    const = lambda i: (0, 0)
    return pl.pallas_call(
        _out_proj_kernel,
        grid=(n // tm,),
        in_specs=[
            pl.BlockSpec((tm, pool_out.shape[1]), row),
            pl.BlockSpec((tm, attn_out.shape[1]), row),
            pl.BlockSpec((tm, d), row),
            pl.BlockSpec(w_out.shape, const, pipeline_mode=pl.Buffered(1)),
            pl.BlockSpec((1, d), const),
        ],
        out_specs=pl.BlockSpec((tm, d), row),
        out_shape=jax.ShapeDtypeStruct((n, d), F32),
        scratch_shapes=[pltpu.VMEM((tm, d), F32)],
        compiler_params=pltpu.CompilerParams(
            dimension_semantics=("arbitrary",), vmem_limit_bytes=VMEM_LIMIT_BYTES),
        name="out_proj",
    )(pool_out, attn_out, x2, w_out, g)


def _ffn_kernel(h_ref, gpre_ref, gpost_ref, wg_ref, wu_ref, wd_ref, o_ref, hn_ref):
    f = pl.program_id(1)

    @pl.when(f == 0)
    def _():
        hn_ref[...] = _rms_norm(h_ref[...], gpre_ref[...]).astype(BF16)

        o_ref[...] = jnp.zeros_like(o_ref)

    hn = hn_ref[...]
    gate = jnp.dot(hn, wg_ref[...], preferred_element_type=F32)
    up = jnp.dot(hn, wu_ref[...], preferred_element_type=F32)
    act = (gate * jax.nn.sigmoid(gate) * up).astype(BF16)
    for c in range(0, o_ref.shape[1], N_CHUNK):
        o_ref[:, c:c + N_CHUNK] += jnp.dot(act, wd_ref[:, c:c + N_CHUNK], preferred_element_type=F32)

    @pl.when(f == pl.num_programs(1) - 1)
    def _():
        o_ref[...] = h_ref[...] + _rms_norm(o_ref[...], gpost_ref[...])


def _ffn(h, g_pre, g_post, w_gate, w_up, w_down):
    n, d = h.shape
    d_ff = w_gate.shape[1]
    tm, tf = TM_FFN, TF_FFN
    return pl.pallas_call(
        _ffn_kernel,
        grid=(n // tm, d_ff // tf),
        in_specs=[
            pl.BlockSpec((tm, d), lambda i, f: (i, 0), pipeline_mode=pl.Buffered(1)),
            pl.BlockSpec((1, d), lambda i, f: (0, 0)),
            pl.BlockSpec((1, d), lambda i, f: (0, 0)),
            pl.BlockSpec((d, tf), lambda i, f: (0, f)),
            pl.BlockSpec((d, tf), lambda i, f: (0, f)),
            pl.BlockSpec((tf, d), lambda i, f: (f, 0)),
        ],
        out_specs=pl.BlockSpec((tm, d), lambda i, f: (i, 0)),
        out_shape=jax.ShapeDtypeStruct((n, d), F32),
        scratch_shapes=[pltpu.VMEM((tm, d), BF16)],
        compiler_params=pltpu.CompilerParams(
            dimension_semantics=("arbitrary", "arbitrary"), vmem_limit_bytes=VMEM_LIMIT_BYTES),
        name="ffn",
    )(h, g_pre, g_post, w_gate, w_up, w_down)


def kernel(x, positions, pre_mix_norm, post_mix_norm, w_in, pool_w, pool_scale,
           lam_q1, lam_k1, lam_q2, lam_k2, subln_w, w_out,
           pre_ffn_norm, post_ffn_norm, w_gate, w_up, w_down):
    batch, seq, d_model = x.shape
    depth = w_in.shape[0]
    pool_width = pool_scale.shape[1]
    attn_width = w_out.shape[1] - pool_width
    qk_width = (w_in.shape[2] - pool_width - attn_width) // 2
    n_heads = attn_width // DIFF_V_DIM
    assert qk_width == n_heads * 2 * DIFF_HEAD_DIM
    assert seq % TQ == 0 and seq % TM_IN == 0 and (batch * seq) % TM_FFN == 0

    n = batch * seq
    h = x.reshape(n, d_model)
    pos2 = positions.reshape(n, 1)
    inv_freq = ROPE_THETA ** (-jnp.arange(0, ROT_DIM, 2, dtype=F32) / ROT_DIM)
    lane_dim = jnp.arange(LANES) % DIFF_HEAD_DIM
    inv_freq_lane = jnp.where(lane_dim < ROT_DIM, inv_freq[lane_dim % ROT_HALF], 0.0).reshape(1, LANES)

    for l in range(depth):
        lambda_init = _lambda_init(l)
        pool_out, q, k, v = _in_proj(
            h, pos2, inv_freq_lane, pre_mix_norm[l].reshape(1, -1), w_in[l].astype(BF16),
            pool_w[l].astype(BF16), pool_scale[l].reshape(1, -1),
            seq=seq, pool_width=pool_width, qk_width=qk_width, attn_width=attn_width)
        lam_vecs = jnp.stack([lam_q1[l], lam_k1[l], lam_q2[l], lam_k2[l]]).astype(F32)
        attn_out = _attention(q, k, v, lam_vecs, subln_w[l].reshape(1, -1),
                              batch=batch, seq=seq, n_heads=n_heads, lambda_init=lambda_init)
        h = _out_proj(pool_out, attn_out, h, w_out[l].astype(BF16), post_mix_norm[l].reshape(1, -1))
        h = _ffn(h, pre_ffn_norm[l].reshape(1, -1), post_ffn_norm[l].reshape(1, -1),
                 w_gate[l].astype(BF16), w_up[l].astype(BF16), w_down[l].astype(BF16))
    return h.reshape(batch, seq, d_model)
```

```python
import functools
import math

import jax
import jax.numpy as jnp
from jax import lax
from jax.experimental import pallas as pl
from jax.experimental.pallas import tpu as pltpu

F32 = jnp.float32
BF16 = jnp.bfloat16

POOL_WINDOWS = (2, 4, 8, 16)
DIFF_HEAD_DIM = 64
DIFF_V_DIM = 2 * DIFF_HEAD_DIM
ROPE_THETA = 500000.0
ROT_DIM = DIFF_HEAD_DIM // 4
ROT_HALF = ROT_DIM // 2
NORM_EPS = 1e-6
NEG_INF = -1e30
LOG2_E = math.log2(math.e)

LANES = 128
POOL_HALO = 32
VMEM_LIMIT_BYTES = 56 * 1024 * 1024

TM_IN = 512
TQ = 1024
TK = 512
TM_OUT = 512
TM_FFN = 1024
TF_FFN = 512
N_CHUNK = 512


def _lambda_init(layer_idx):
    return 0.8 - 0.6 * math.exp(-0.3 * layer_idx)


def _rms_norm(xf, g):
    ms = jnp.mean(xf * xf, axis=-1, keepdims=True)
    return xf * lax.rsqrt(ms + NORM_EPS) * g


def _in_proj_kernel(x_ref, pos_ref, invf_ref, g_ref, w_ref, wvt_ref, pw_ref, ps_ref,
                    pool_ref, q_ref, k_ref, vt_ref, hn_ref, u_ref, carry_ref, *, seq, pool_width, qk_width):
    tm = x_ref.shape[0]
    i = pl.program_id(0)
    tiles_per_seq = seq // tm
    ti = i % tiles_per_seq

    hn_ref[...] = _rms_norm(x_ref[...], g_ref[...]).astype(BF16)

    @pl.when(ti == 0)
    def _():
        carry_ref[...] = jnp.zeros_like(carry_ref)

    lane = lax.broadcasted_iota(jnp.int32, (1, LANES), 1) % DIFF_HEAD_DIM
    ang = pos_ref[...].astype(F32) * invf_ref[...]
    cos, sin = jnp.cos(ang), jnp.sin(ang)
    coef_self = jnp.where(lane < ROT_DIM, cos, 1.0)
    coef_lo = jnp.where((lane >= ROT_HALF) & (lane < ROT_DIM), sin, 0.0)
    coef_hi = jnp.where(lane < ROT_HALF, -sin, 0.0)

    def rope_chunk(out_ref, col0, c, scale):
        t = jnp.dot(hn_ref[...], w_ref[:, col0 + c:col0 + c + N_CHUNK], preferred_element_type=F32)
        for h in range(N_CHUNK // LANES):
            th = t[:, h * LANES:(h + 1) * LANES]
            r = (th * coef_self + pltpu.roll(th, ROT_HALF, 1) * coef_lo
                 + pltpu.roll(th, LANES - ROT_HALF, 1) * coef_hi)
            if scale != 1.0:
                r = r * scale
            out_ref[:, c + h * LANES:c + (h + 1) * LANES] = r.astype(BF16)

    gdim = pool_width // len(POOL_WINDOWS)
    t_in_seq = ti * tm + lax.broadcasted_iota(jnp.int32, (tm, 1), 0)

    def pool_group(g, w):
        c0 = g * gdim
        u = u_ref[:, c0:c0 + gdim]
        ext = jnp.concatenate([carry_ref[g], u], axis=0)
        carry_ref[g] = u[tm - POOL_HALO:, :]
        lvl, off, k = ext, 0, 1
        while k < w:
            new_off = min(off + 8, POOL_HALO)
            cur = lvl[new_off - off:, :]
            shifted = lvl[new_off - off - k: lvl.shape[0] - k, :]
            lvl, off, k = cur + shifted, new_off, 2 * k
        wsum = lvl[POOL_HALO - off:, :]
        cnt = jnp.minimum(t_in_seq + 1, w).astype(F32)
        pooled = (wsum / cnt - u).astype(BF16)
        mixed = jnp.dot(pooled, pw_ref[g], preferred_element_type=F32)
        pool_ref[:, c0:c0 + gdim] = (mixed * ps_ref[:, c0:c0 + gdim]).astype(BF16)

    for c in range(0, pool_width, N_CHUNK):
        u_ref[:, c:c + N_CHUNK] = jnp.dot(hn_ref[...], w_ref[:, c:c + N_CHUNK], preferred_element_type=F32)

    q_scale = DIFF_HEAD_DIM ** -0.5 * LOG2_E
    chunks = ([(q_ref, pool_width, c, q_scale) for c in range(0, qk_width, N_CHUNK)]
              + [(k_ref, pool_width + qk_width, c, 1.0) for c in range(0, qk_width, N_CHUNK)])
    groups = list(enumerate(POOL_WINDOWS))
    for idx, args in enumerate(chunks):
        rope_chunk(*args)
        for g, w in groups[idx * len(groups) // len(chunks):(idx + 1) * len(groups) // len(chunks)]:
            pool_group(g, w)

    for c in range(0, vt_ref.shape[0], N_CHUNK):
        vt_ref[c:c + N_CHUNK, :] = lax.dot_general(
            wvt_ref[c:c + N_CHUNK, :], hn_ref[...], (((1,), (1,)), ((), ())),
            preferred_element_type=F32).astype(BF16)


def _in_proj(x2, pos2, inv_freq, g, w_pqk, w_vt, pool_w, pool_scale, *, seq, pool_width, qk_width):
    n, d = x2.shape
    tm = TM_IN
    n_groups = len(POOL_WINDOWS)
    gdim = pool_width // n_groups
    attn_width = w_vt.shape[0]
    const = lambda i: (0, 0)
    row = lambda i: (i, 0)
    kern = functools.partial(_in_proj_kernel, seq=seq, pool_width=pool_width, qk_width=qk_width)
    return pl.pallas_call(
        kern,
        grid=(n // tm,),
        in_specs=[
            pl.BlockSpec((tm, d), row),
            pl.BlockSpec((tm, 1), row),
            pl.BlockSpec((1, LANES), const),
            pl.BlockSpec((1, d), const),
            pl.BlockSpec(w_pqk.shape, const, pipeline_mode=pl.Buffered(1)),
            pl.BlockSpec(w_vt.shape, const, pipeline_mode=pl.Buffered(1)),
            pl.BlockSpec(pool_w.shape, lambda i: (0, 0, 0), pipeline_mode=pl.Buffered(1)),
            pl.BlockSpec((1, pool_width), const),
        ],
        out_specs=[
            pl.BlockSpec((tm, pool_width), row),
            pl.BlockSpec((tm, qk_width), row),
            pl.BlockSpec((tm, qk_width), row),
            pl.BlockSpec((None, attn_width, tm), lambda i: (i, 0, 0)),
        ],
        out_shape=[
            jax.ShapeDtypeStruct((n, pool_width), BF16),
            jax.ShapeDtypeStruct((n, qk_width), BF16),
            jax.ShapeDtypeStruct((n, qk_width), BF16),
            jax.ShapeDtypeStruct((n // tm, attn_width, tm), BF16),
        ],
        scratch_shapes=[
            pltpu.VMEM((tm, d), BF16),
            pltpu.VMEM((tm, pool_width), F32),
            pltpu.VMEM((n_groups, POOL_HALO, gdim), F32),
        ],
        compiler_params=pltpu.CompilerParams(
            dimension_semantics=("arbitrary",), vmem_limit_bytes=VMEM_LIMIT_BYTES),
        name="in_proj",
    )(x2, pos2, inv_freq, g, w_pqk, w_vt, pool_w, pool_scale)


def _attn_kernel(lam_ref, sw_ref, q_ref, k_ref, vt_ref, o_ref,
                 qc_ref, s0_ref, s1_ref, mx0_ref, mx1_ref, m_ref, l_ref, acc_ref, *, lambda_init):
    tq = q_ref.shape[0]
    qi = pl.program_id(2)
    s_slots, mx_slots = (s0_ref, s1_ref), (mx0_ref, mx1_ref)

    q = q_ref[...]
    lane = lax.broadcasted_iota(jnp.int32, q.shape, 1)
    zero = jnp.zeros_like(q)
    qc_ref[0] = jnp.where(lane < DIFF_HEAD_DIM, q, zero)
    qc_ref[1] = jnp.where(lane >= DIFF_HEAD_DIM, q, zero)

    m_ref[...] = jnp.full_like(m_ref, NEG_INF)
    l_ref[...] = jnp.zeros_like(l_ref)
    acc_ref[...] = jnp.zeros_like(acc_ref)

    def scores(j, slot, c, mask):
        start = pl.multiple_of(j * TK, TK)
        s = lax.dot_general(k_ref[pl.ds(start, TK), :], qc_ref[c], (((1,), (1,)), ((), ())),
                            preferred_element_type=F32)
        if mask is not None:
            kv = lax.broadcasted_iota(jnp.int32, (TK, tq), 0)
            r = lax.broadcasted_iota(jnp.int32, (TK, tq), 1)
            keep = (kv + TK <= r) if mask == "hi" else (kv <= r)
            if mask == "lo_if_first":
                keep = keep | (qi > 0)
            s = jnp.where(keep, s, NEG_INF)
        s_slots[slot][c] = s
        mx_slots[slot][c] = jnp.max(s, axis=0, keepdims=True)

    def exp_pv(j, slot, c):
        m_prev = m_ref[c]
        m_new = jnp.maximum(m_prev, mx_slots[slot][c])
        alpha = jnp.exp2(m_prev - m_new)
        p = jnp.exp2(s_slots[slot][c] - m_new)
        l_ref[c] = alpha * l_ref[c] + jnp.sum(p, axis=0, keepdims=True)
        pv = jnp.dot(vt_ref[j], p.astype(BF16), preferred_element_type=F32)
        acc_ref[c] = alpha * acc_ref[c] + pv
        m_ref[c] = m_new

    def stage(j, slot, mask):
        for c in range(2):
            scores(j + 1, 1 - slot, c, mask)
            exp_pv(j, slot, c)

    for c in range(2):
        scores(0, 0, c, "lo_if_first")

    def body(t, carry):
        stage(2 * t, 0, None)
        stage(2 * t + 1, 1, None)
        return carry

    lax.fori_loop(0, qi - 1, body, 0)

    @pl.when(qi > 0)
    def _():
        stage(2 * qi - 2, 0, None)
        stage(2 * qi - 1, 1, "lo")

    stage(2 * qi, 0, "hi")
    for c in range(2):
        exp_pv(2 * qi + 1, 1, c)

    lam_v = lam_ref[...]
    lam = (jnp.exp(jnp.sum(lam_v[0:1] * lam_v[1:2], keepdims=True))
           - jnp.exp(jnp.sum(lam_v[2:3] * lam_v[3:4], keepdims=True)) + lambda_init)
    attn_t = acc_ref[0] / l_ref[0] - lam * (acc_ref[1] / l_ref[1])
    ms = jnp.mean(attn_t * attn_t, axis=0, keepdims=True)
    y = (attn_t * lax.rsqrt(ms + NORM_EPS)).T
    o_ref[...] = (y * sw_ref[...] * (1.0 - lambda_init)).astype(BF16)


def _attention(q, k, vt, lam_vecs, subln_w, *, batch, seq, n_heads, lambda_init):
    n, width = q.shape
    nq = seq // TQ
    nk = seq // TK
    assert TQ == 2 * TK and vt.shape == (n // TK, n_heads * DIFF_V_DIM, TK)
    kern = functools.partial(_attn_kernel, lambda_init=lambda_init)
    q_map = lambda b, h, i: (b * nq + i, h)
    return pl.pallas_call(
        kern,
        grid=(batch, n_heads, nq),
        in_specs=[
            pl.BlockSpec(lam_vecs.shape, lambda b, h, i: (0, 0)),
            pl.BlockSpec((1, DIFF_V_DIM), lambda b, h, i: (0, 0)),
            pl.BlockSpec((TQ, DIFF_V_DIM), q_map),
            pl.BlockSpec((seq, DIFF_V_DIM), lambda b, h, i: (b, h)),
            pl.BlockSpec((nk, DIFF_V_DIM, TK), lambda b, h, i: (b, h, 0)),
        ],
        out_specs=pl.BlockSpec((TQ, DIFF_V_DIM), q_map),
        out_shape=jax.ShapeDtypeStruct((n, width), BF16),
        scratch_shapes=[
            pltpu.VMEM((2, TQ, DIFF_V_DIM), BF16),
            pltpu.VMEM((2, TK, TQ), F32),
            pltpu.VMEM((2, TK, TQ), F32),
            pltpu.VMEM((2, 1, TQ), F32),
            pltpu.VMEM((2, 1, TQ), F32),
            pltpu.VMEM((2, 1, TQ), F32),
            pltpu.VMEM((2, 1, TQ), F32),
            pltpu.VMEM((2, DIFF_V_DIM, TQ), F32),
        ],
        compiler_params=pltpu.CompilerParams(
            dimension_semantics=("arbitrary", "arbitrary", "arbitrary"), vmem_limit_bytes=VMEM_LIMIT_BYTES),
        name="diff_attention",
    )(lam_vecs, subln_w, q, k, vt)


def _out_proj_kernel(pool_ref, attn_ref, x_ref, w_ref, g_ref, h_ref, mix_ref):
    pw = pool_ref.shape[1]
    d = x_ref.shape[1]
    for c in range(0, d, N_CHUNK):
        mix_ref[:, c:c + N_CHUNK] = (
            jnp.dot(pool_ref[...], w_ref[:pw, c:c + N_CHUNK], preferred_element_type=F32)
            + jnp.dot(attn_ref[...], w_ref[pw:, c:c + N_CHUNK], preferred_element_type=F32))
    h_ref[...] = x_ref[...] + _rms_norm(mix_ref[...], g_ref[...])


def _out_proj(pool_out, attn_out, x2, w_out, g):
    n, d = x2.shape
    tm = TM_OUT
    row = lambda i: (i, 0)
    const = lambda i: (0, 0)
    return pl.pallas_call(
        _out_proj_kernel,
        grid=(n // tm,),
        in_specs=[
            pl.BlockSpec((tm, pool_out.shape[1]), row),
            pl.BlockSpec((tm, attn_out.shape[1]), row),
            pl.BlockSpec((tm, d), row),
            pl.BlockSpec(w_out.shape, const, pipeline_mode=pl.Buffered(1)),
            pl.BlockSpec((1, d), const),
        ],
        out_specs=pl.BlockSpec((tm, d), row),
        out_shape=jax.ShapeDtypeStruct((n, d), F32),
        scratch_shapes=[pltpu.VMEM((tm, d), F32)],
        compiler_params=pltpu.CompilerParams(
            dimension_semantics=("arbitrary",), vmem_limit_bytes=VMEM_LIMIT_BYTES),
        name="out_proj",
    )(pool_out, attn_out, x2, w_out, g)


def _ffn_kernel(h_ref, gpre_ref, gpost_ref, wg_ref, wu_ref, wd_ref, o_ref, hn_ref):
    f = pl.program_id(1)

    @pl.when(f == 0)
    def _():
        hn_ref[...] = _rms_norm(h_ref[...], gpre_ref[...]).astype(BF16)
        o_ref[...] = jnp.zeros_like(o_ref)

    hn = hn_ref[...]
    gate = jnp.dot(hn, wg_ref[...], preferred_element_type=F32)
    up = jnp.dot(hn, wu_ref[...], preferred_element_type=F32)
    act = (gate * jax.nn.sigmoid(gate) * up).astype(BF16)
    for c in range(0, o_ref.shape[1], N_CHUNK):
        o_ref[:, c:c + N_CHUNK] += jnp.dot(act, wd_ref[:, c:c + N_CHUNK], preferred_element_type=F32)

    @pl.when(f == pl.num_programs(1) - 1)
    def _():
        o_ref[...] = h_ref[...] + _rms_norm(o_ref[...], gpost_ref[...])


def _ffn(h, g_pre, g_post, w_gate, w_up, w_down):
    n, d = h.shape
    d_ff = w_gate.shape[1]
    tm, tf = TM_FFN, TF_FFN
    return pl.pallas_call(
        _ffn_kernel,
        grid=(n // tm, d_ff // tf),
        in_specs=[
            pl.BlockSpec((tm, d), lambda i, f: (i, 0), pipeline_mode=pl.Buffered(1)),
            pl.BlockSpec((1, d), lambda i, f: (0, 0)),
            pl.BlockSpec((1, d), lambda i, f: (0, 0)),
            pl.BlockSpec((d, tf), lambda i, f: (0, f)),
            pl.BlockSpec((d, tf), lambda i, f: (0, f)),
            pl.BlockSpec((tf, d), lambda i, f: (f, 0)),
        ],
        out_specs=pl.BlockSpec((tm, d), lambda i, f: (i, 0)),
        out_shape=jax.ShapeDtypeStruct((n, d), F32),
        scratch_shapes=[pltpu.VMEM((tm, d), BF16)],
        compiler_params=pltpu.CompilerParams(
            dimension_semantics=("arbitrary", "arbitrary"), vmem_limit_bytes=VMEM_LIMIT_BYTES),
        name="ffn",
    )(h, g_pre, g_post, w_gate, w_up, w_down)


def kernel(x, positions, pre_mix_norm, post_mix_norm, w_in, pool_w, pool_scale,
           lam_q1, lam_k1, lam_q2, lam_k2, subln_w, w_out,
           pre_ffn_norm, post_ffn_norm, w_gate, w_up, w_down):
    batch, seq, d_model = x.shape
    depth = w_in.shape[0]
    pool_width = pool_scale.shape[1]
    attn_width = w_out.shape[1] - pool_width
    qk_width = (w_in.shape[2] - pool_width - attn_width) // 2
    n_heads = attn_width // DIFF_V_DIM
    assert qk_width == n_heads * 2 * DIFF_HEAD_DIM
    assert seq % TQ == 0 and seq % TM_IN == 0 and (batch * seq) % TM_FFN == 0
    assert TM_IN == TK

    n = batch * seq
    h = x.reshape(n, d_model)
    pos2 = positions.reshape(n, 1)
    inv_freq = ROPE_THETA ** (-jnp.arange(0, ROT_DIM, 2, dtype=F32) / ROT_DIM)
    lane_dim = jnp.arange(LANES) % DIFF_HEAD_DIM
    inv_freq_lane = jnp.where(lane_dim < ROT_DIM, inv_freq[lane_dim % ROT_HALF], 0.0).reshape(1, LANES)

    for l in range(depth):
        lambda_init = _lambda_init(l)
        v0 = pool_width + 2 * qk_width
        pool_out, q, k, vt = _in_proj(
            h, pos2, inv_freq_lane, pre_mix_norm[l].reshape(1, -1), w_in[l][:, :v0].astype(BF16),
            w_in[l][:, v0:].T.astype(BF16), pool_w[l].astype(BF16), pool_scale[l].reshape(1, -1),
            seq=seq, pool_width=pool_width, qk_width=qk_width)
        lam_vecs = jnp.stack([lam_q1[l], lam_k1[l], lam_q2[l], lam_k2[l]]).astype(F32)
        attn_out = _attention(q, k, vt, lam_vecs, subln_w[l].reshape(1, -1),
                              batch=batch, seq=seq, n_heads=n_heads, lambda_init=lambda_init)
        h = _out_proj(pool_out, attn_out, h, w_out[l].astype(BF16), post_mix_norm[l].reshape(1, -1))
        h = _ffn(h, pre_ffn_norm[l].reshape(1, -1), post_ffn_norm[l].reshape(1, -1),
                 w_gate[l].astype(BF16), w_up[l].astype(BF16), w_down[l].astype(BF16))
    return h.reshape(batch, seq, d_model)
```

```python
import functools
import math

import jax
import jax.numpy as jnp
from jax import lax
from jax.experimental import pallas as pl
from jax.experimental.pallas import tpu as pltpu

F32 = jnp.float32
BF16 = jnp.bfloat16

POOL_WINDOWS = (2, 4, 8, 16)
DIFF_HEAD_DIM = 64
DIFF_V_DIM = 2 * DIFF_HEAD_DIM
ROPE_THETA = 500000.0
ROT_DIM = DIFF_HEAD_DIM // 4
ROT_HALF = ROT_DIM // 2
NORM_EPS = 1e-6
NEG_INF = -1e30
LOG2_E = math.log2(math.e)

LANES = 128
POOL_HALO = 32
VMEM_LIMIT_BYTES = 56 * 1024 * 1024

TM_IN = 512
TQ = 1024
TK = 512
TM_OUT = 512
TM_FFN = 1024
TF_FFN = 512
N_CHUNK = 512
NORM_ROWS = 64


def _lambda_init(layer_idx):
    return 0.8 - 0.6 * math.exp(-0.3 * layer_idx)


def _rms_norm(xf, g):
    ms = jnp.mean(xf * xf, axis=-1, keepdims=True)
    return xf * lax.rsqrt(ms + NORM_EPS) * g


def _rms_norm_rows(src_ref, g_ref, scale_ref, emit, unroll):
    n = src_ref.shape[0]

    def scale_body(c, carry):
        rows = pl.ds(pl.multiple_of(c * NORM_ROWS, NORM_ROWS), NORM_ROWS)
        x = src_ref[rows, :]
        scale_ref[rows, :] = lax.rsqrt(jnp.mean(x * x, axis=-1, keepdims=True) + NORM_EPS)
        return carry

    lax.fori_loop(0, n // NORM_ROWS, scale_body, 0, unroll=unroll)

    def apply_body(c, carry):
        rows = pl.ds(pl.multiple_of(c * NORM_ROWS, NORM_ROWS), NORM_ROWS)
        emit(rows, src_ref[rows, :] * scale_ref[rows, :] * g_ref[...])
        return carry

    lax.fori_loop(0, n // NORM_ROWS, apply_body, 0, unroll=unroll)


def _in_proj_kernel(x_ref, pos_ref, invf_ref, g_ref, w_ref, pw_ref, ps_ref,
                    pool_ref, q_ref, k_ref, vt_ref, w_bf_ref, wvt_ref, hn_ref, u_ref, carry_ref, nscale_ref,
                    *, seq, pool_width, qk_width):
    tm = x_ref.shape[0]
    step = pl.program_id(0)
    n_direct = w_bf_ref.shape[0]
    n_cast = n_direct + wvt_ref.shape[0] // N_CHUNK

    @pl.when(step < n_direct)
    def _():
        w_bf_ref[step] = w_ref[...].astype(BF16)

    @pl.when((step >= n_direct) & (step < n_cast))
    def _():
        r0 = pl.multiple_of((step - n_direct) * N_CHUNK, N_CHUNK)
        wvt_ref[pl.ds(r0, N_CHUNK), :] = w_ref[...].T.astype(BF16)

    @pl.when(step >= n_cast)
    def _():
        tiles_per_seq = seq // tm
        ti = (step - n_cast) % tiles_per_seq

        def store_hn(rows, y):
            hn_ref[rows, :] = y.astype(BF16)

        _rms_norm_rows(x_ref, g_ref, nscale_ref, store_hn, unroll=2)

        @pl.when(ti == 0)
        def _():
            carry_ref[...] = jnp.zeros_like(carry_ref)

        lane = lax.broadcasted_iota(jnp.int32, (1, LANES), 1) % DIFF_HEAD_DIM
        ang = pos_ref[...].astype(F32) * invf_ref[...]
        cos, sin = jnp.cos(ang), jnp.sin(ang)
        coef_self = jnp.where(lane < ROT_DIM, cos, 1.0)
        coef_lo = jnp.where((lane >= ROT_HALF) & (lane < ROT_DIM), sin, 0.0)
        coef_hi = jnp.where(lane < ROT_HALF, -sin, 0.0)

        def rope_chunk(out_ref, chunk, c, scale):
            t = jnp.dot(hn_ref[...], w_bf_ref[chunk], preferred_element_type=F32)
            for h in range(N_CHUNK // LANES):
                th = t[:, h * LANES:(h + 1) * LANES]
                r = (th * coef_self + pltpu.roll(th, ROT_HALF, 1) * coef_lo
                     + pltpu.roll(th, LANES - ROT_HALF, 1) * coef_hi)
                if scale != 1.0:
                    r = r * scale
                out_ref[:, c + h * LANES:c + (h + 1) * LANES] = r.astype(BF16)

        gdim = pool_width // len(POOL_WINDOWS)
        t_in_seq = ti * tm + lax.broadcasted_iota(jnp.int32, (tm, 1), 0)

        def pool_group(g, w):
            c0 = g * gdim
            u = u_ref[:, c0:c0 + gdim]
            ext = jnp.concatenate([carry_ref[g], u], axis=0)
            carry_ref[g] = u[tm - POOL_HALO:, :]
            lvl, off, k = ext, 0, 1
            while k < w:
                new_off = min(off + 8, POOL_HALO)
                cur = lvl[new_off - off:, :]
                shifted = lvl[new_off - off - k: lvl.shape[0] - k, :]
                lvl, off, k = cur + shifted, new_off, 2 * k
            wsum = lvl[POOL_HALO - off:, :]
            cnt = jnp.minimum(t_in_seq + 1, w).astype(F32)
            pooled = (wsum / cnt - u).astype(BF16)
            mixed = jnp.dot(pooled, pw_ref[g].astype(BF16), preferred_element_type=F32)
            pool_ref[:, c0:c0 + gdim] = (mixed * ps_ref[:, c0:c0 + gdim]).astype(BF16)

        n_pool, n_qk = pool_width // N_CHUNK, qk_width // N_CHUNK
        for ci in range(n_pool):
            u_ref[:, ci * N_CHUNK:(ci + 1) * N_CHUNK] = jnp.dot(
                hn_ref[...], w_bf_ref[ci], preferred_element_type=F32)

        q_scale = DIFF_HEAD_DIM ** -0.5 * LOG2_E
        chunks = ([(q_ref, n_pool + ci, ci * N_CHUNK, q_scale) for ci in range(n_qk)]
                  + [(k_ref, n_pool + n_qk + ci, ci * N_CHUNK, 1.0) for ci in range(n_qk)])
        groups = list(enumerate(POOL_WINDOWS))
        for idx, args in enumerate(chunks):
            rope_chunk(*args)
            for g, w in groups[idx * len(groups) // len(chunks):(idx + 1) * len(groups) // len(chunks)]:
                pool_group(g, w)

        for c in range(0, vt_ref.shape[0], N_CHUNK):
            vt_ref[c:c + N_CHUNK, :] = lax.dot_general(
                wvt_ref[c:c + N_CHUNK, :], hn_ref[...], (((1,), (1,)), ((), ())),
                preferred_element_type=F32).astype(BF16)


def _in_proj(x2, pos2, inv_freq, g, w_in, pool_w, pool_scale, *, seq, pool_width, qk_width, attn_width):
    n, d = x2.shape
    tm = TM_IN
    n_groups = len(POOL_WINDOWS)
    gdim = pool_width // n_groups
    n_direct = (pool_width + 2 * qk_width) // N_CHUNK
    n_cast = w_in.shape[1] // N_CHUNK
    const = lambda s: (0, 0)
    row = lambda s: (jnp.maximum(s - n_cast, 0), 0)
    kern = functools.partial(_in_proj_kernel, seq=seq, pool_width=pool_width, qk_width=qk_width)
    return pl.pallas_call(
        kern,
        grid=(n_cast + n // tm,),
        in_specs=[
            pl.BlockSpec((tm, d), row),
            pl.BlockSpec((tm, 1), row),
            pl.BlockSpec((1, LANES), const),
            pl.BlockSpec((1, d), const),
            pl.BlockSpec((d, N_CHUNK), lambda s: (0, jnp.minimum(s, n_cast - 1))),
            pl.BlockSpec(pool_w.shape, lambda s: (0, 0, 0)),
            pl.BlockSpec((1, pool_width), const),
        ],
        out_specs=[
            pl.BlockSpec((tm, pool_width), row),
            pl.BlockSpec((tm, qk_width), row),
            pl.BlockSpec((tm, qk_width), row),
            pl.BlockSpec((None, attn_width, tm), lambda s: (jnp.maximum(s - n_cast, 0), 0, 0)),
        ],
        out_shape=[
            jax.ShapeDtypeStruct((n, pool_width), BF16),
            jax.ShapeDtypeStruct((n, qk_width), BF16),
            jax.ShapeDtypeStruct((n, qk_width), BF16),
            jax.ShapeDtypeStruct((n // tm, attn_width, tm), BF16),
        ],
        scratch_shapes=[
            pltpu.VMEM((n_direct, d, N_CHUNK), BF16),
            pltpu.VMEM((attn_width, d), BF16),
            pltpu.VMEM((tm, d), BF16),
            pltpu.VMEM((tm, pool_width), F32),
            pltpu.VMEM((n_groups, POOL_HALO, gdim), F32),
            pltpu.VMEM((tm, 1), F32),
        ],
        compiler_params=pltpu.CompilerParams(
            dimension_semantics=("arbitrary",), vmem_limit_bytes=VMEM_LIMIT_BYTES),
        name="in_proj",
    )(x2, pos2, inv_freq, g, w_in, pool_w, pool_scale)


def _attn_kernel(lam_ref, sw_ref, q_ref, k_ref, vt_ref, o_ref,
                 qc_ref, s0_ref, s1_ref, mx0_ref, mx1_ref, m_ref, l_ref, acc_ref, *, lambda_init):
    tq = q_ref.shape[0]
    qi = pl.program_id(2)
    s_slots, mx_slots = (s0_ref, s1_ref), (mx0_ref, mx1_ref)

    q = q_ref[...]
    lane = lax.broadcasted_iota(jnp.int32, q.shape, 1)
    zero = jnp.zeros_like(q)
    qc_ref[0] = jnp.where(lane < DIFF_HEAD_DIM, q, zero)
    qc_ref[1] = jnp.where(lane >= DIFF_HEAD_DIM, q, zero)

    m_ref[...] = jnp.full_like(m_ref, NEG_INF)
    l_ref[...] = jnp.zeros_like(l_ref)
    acc_ref[...] = jnp.zeros_like(acc_ref)

    def scores(j, slot, c, mask):
        start = pl.multiple_of(j * TK, TK)
        s = lax.dot_general(k_ref[pl.ds(start, TK), :], qc_ref[c], (((1,), (1,)), ((), ())),
                            preferred_element_type=F32)
        if mask is not None:
            kv = lax.broadcasted_iota(jnp.int32, (TK, tq), 0)
            r = lax.broadcasted_iota(jnp.int32, (TK, tq), 1)
            keep = (kv + TK <= r) if mask == "hi" else (kv <= r)
            if mask == "lo_if_first":
                keep = keep | (qi > 0)
            s = jnp.where(keep, s, NEG_INF)
        s_slots[slot][c] = s
        mx_slots[slot][c] = jnp.max(s, axis=0, keepdims=True)

    def exp_pv(j, slot, c):
        m_prev = m_ref[c]
        m_new = jnp.maximum(m_prev, mx_slots[slot][c])
        alpha = jnp.exp2(m_prev - m_new)
        p = jnp.exp2(s_slots[slot][c] - m_new)
        l_ref[c] = alpha * l_ref[c] + jnp.sum(p, axis=0, keepdims=True)
        pv = jnp.dot(vt_ref[j], p.astype(BF16), preferred_element_type=F32)
        acc_ref[c] = alpha * acc_ref[c] + pv
        m_ref[c] = m_new

    def stage(j, slot, mask):
        for c in range(2):
            scores(j + 1, 1 - slot, c, mask)
            exp_pv(j, slot, c)

    for c in range(2):
        scores(0, 0, c, "lo_if_first")

    def body(t, carry):
        stage(2 * t, 0, None)
        stage(2 * t + 1, 1, None)
        return carry

    lax.fori_loop(0, qi - 1, body, 0)

    @pl.when(qi > 0)
    def _():
        stage(2 * qi - 2, 0, None)
        stage(2 * qi - 1, 1, "lo")

    stage(2 * qi, 0, "hi")
    for c in range(2):
        exp_pv(2 * qi + 1, 1, c)

    lam_v = lam_ref[...]
    lam = (jnp.exp(jnp.sum(lam_v[0:1] * lam_v[1:2], keepdims=True))
           - jnp.exp(jnp.sum(lam_v[2:3] * lam_v[3:4], keepdims=True)) + lambda_init)
    attn_t = acc_ref[0] / l_ref[0] - lam * (acc_ref[1] / l_ref[1])
    ms = jnp.mean(attn_t * attn_t, axis=0, keepdims=True)
    y = (attn_t * lax.rsqrt(ms + NORM_EPS)).T
    o_ref[...] = (y * sw_ref[...] * (1.0 - lambda_init)).astype(BF16)


def _attention(q, k, vt, lam_vecs, subln_w, *, batch, seq, n_heads, lambda_init):
    n, width = q.shape
    nq = seq // TQ
    nk = seq // TK
    assert TQ == 2 * TK and vt.shape == (n // TK, n_heads * DIFF_V_DIM, TK)
    kern = functools.partial(_attn_kernel, lambda_init=lambda_init)
    q_map = lambda b, h, i: (b * nq + i, h)
    return pl.pallas_call(
        kern,
        grid=(batch, n_heads, nq),
        in_specs=[
            pl.BlockSpec(lam_vecs.shape, lambda b, h, i: (0, 0)),
            pl.BlockSpec((1, DIFF_V_DIM), lambda b, h, i: (0, 0)),
            pl.BlockSpec((TQ, DIFF_V_DIM), q_map),
            pl.BlockSpec((seq, DIFF_V_DIM), lambda b, h, i: (b, h)),
            pl.BlockSpec((nk, DIFF_V_DIM, TK), lambda b, h, i: (b, h, 0)),
        ],
        out_specs=pl.BlockSpec((TQ, DIFF_V_DIM), q_map),
        out_shape=jax.ShapeDtypeStruct((n, width), BF16),
        scratch_shapes=[
            pltpu.VMEM((2, TQ, DIFF_V_DIM), BF16),
            pltpu.VMEM((2, TK, TQ), F32),
            pltpu.VMEM((2, TK, TQ), F32),
            pltpu.VMEM((2, 1, TQ), F32),
            pltpu.VMEM((2, 1, TQ), F32),
            pltpu.VMEM((2, 1, TQ), F32),
            pltpu.VMEM((2, 1, TQ), F32),
            pltpu.VMEM((2, DIFF_V_DIM, TQ), F32),
        ],
        compiler_params=pltpu.CompilerParams(
            dimension_semantics=("arbitrary", "arbitrary", "arbitrary"), vmem_limit_bytes=VMEM_LIMIT_BYTES),
        name="diff_attention",
    )(lam_vecs, subln_w, q, k, vt)


def _out_proj_kernel(pool_ref, attn_ref, x_ref, w_ref, g_ref, h_ref, w_bf_ref, mix_ref, nscale_ref):
    step = pl.program_id(0)
    n_cast = w_bf_ref.shape[0]
    pw = pool_ref.shape[1]

    @pl.when(step < n_cast)
    def _():
        w_bf_ref[step] = w_ref[...].astype(BF16)

    @pl.when(step >= n_cast)
    def _():
        for ci in range(n_cast):
            mix_ref[:, ci * N_CHUNK:(ci + 1) * N_CHUNK] = (
                jnp.dot(pool_ref[...], w_bf_ref[ci, :pw, :], preferred_element_type=F32)
                + jnp.dot(attn_ref[...], w_bf_ref[ci, pw:, :], preferred_element_type=F32))

        def store_h(rows, y):
            h_ref[rows, :] = x_ref[rows, :] + y

        _rms_norm_rows(mix_ref, g_ref, nscale_ref, store_h, unroll=2)


def _out_proj(pool_out, attn_out, x2, w_out, g):
    n, d = x2.shape
    tm = TM_OUT
    n_cast = w_out.shape[1] // N_CHUNK
    row = lambda s: (jnp.maximum(s - n_cast, 0), 0)
    const = lambda s: (0, 0)
    return pl.pallas_call(
        _out_proj_kernel,
        grid=(n_cast + n // tm,),
        in_specs=[
            pl.BlockSpec((tm, pool_out.shape[1]), row),
            pl.BlockSpec((tm, attn_out.shape[1]), row),
            pl.BlockSpec((tm, d), row),
            pl.BlockSpec((w_out.shape[0], N_CHUNK), lambda s: (0, jnp.minimum(s, n_cast - 1))),
            pl.BlockSpec((1, d), const),
        ],
        out_specs=pl.BlockSpec((tm, d), row),
        out_shape=jax.ShapeDtypeStruct((n, d), F32),
        scratch_shapes=[
            pltpu.VMEM((n_cast, w_out.shape[0], N_CHUNK), BF16),
            pltpu.VMEM((tm, d), F32),
            pltpu.VMEM((tm, 1), F32),
        ],
        compiler_params=pltpu.CompilerParams(
            dimension_semantics=("arbitrary",), vmem_limit_bytes=VMEM_LIMIT_BYTES),
        name="out_proj",
    )(pool_out, attn_out, x2, w_out, g)


def _ffn_kernel(h_ref, gpre_ref, gpost_ref, wg_ref, wu_ref, wd_ref, o_ref, hn_ref, nscale_ref):
    f = pl.program_id(1)

    @pl.when(f == 0)
    def _():
        def store_hn(rows, y):
            hn_ref[rows, :] = y.astype(BF16)
            o_ref[rows, :] = jnp.zeros_like(y)

        _rms_norm_rows(h_ref, gpre_ref, nscale_ref, store_hn, unroll=True)

    hn = hn_ref[...]
    gate = jnp.dot(hn, wg_ref[...], preferred_element_type=F32)
    up = jnp.dot(hn, wu_ref[...], preferred_element_type=F32)
    act = (gate * jax.nn.sigmoid(gate) * up).astype(BF16)
    for c in range(0, o_ref.shape[1], N_CHUNK):
        o_ref[:, c:c + N_CHUNK] += jnp.dot(act, wd_ref[:, c:c + N_CHUNK], preferred_element_type=F32)

    @pl.when(f == pl.num_programs(1) - 1)
    def _():
        def store_out(rows, y):
            o_ref[rows, :] = h_ref[rows, :] + y

        _rms_norm_rows(o_ref, gpost_ref, nscale_ref, store_out, unroll=True)


def _ffn(h, g_pre, g_post, w_gate, w_up, w_down):
    n, d = h.shape
    d_ff = w_gate.shape[1]
    tm, tf = TM_FFN, TF_FFN
    return pl.pallas_call(
        _ffn_kernel,
        grid=(n // tm, d_ff // tf),
        in_specs=[
            pl.BlockSpec((tm, d), lambda i, f: (i, 0), pipeline_mode=pl.Buffered(1)),
            pl.BlockSpec((1, d), lambda i, f: (0, 0)),
            pl.BlockSpec((1, d), lambda i, f: (0, 0)),
            pl.BlockSpec((d, tf), lambda i, f: (0, f)),
            pl.BlockSpec((d, tf), lambda i, f: (0, f)),
            pl.BlockSpec((tf, d), lambda i, f: (f, 0)),
        ],
        out_specs=pl.BlockSpec((tm, d), lambda i, f: (i, 0)),
        out_shape=jax.ShapeDtypeStruct((n, d), F32),
        scratch_shapes=[pltpu.VMEM((tm, d), BF16), pltpu.VMEM((tm, 1), F32)],
        compiler_params=pltpu.CompilerParams(
            dimension_semantics=("arbitrary", "arbitrary"), vmem_limit_bytes=VMEM_LIMIT_BYTES),
        name="ffn",
    )(h, g_pre, g_post, w_gate, w_up, w_down)


def kernel(x, positions, pre_mix_norm, post_mix_norm, w_in, pool_w, pool_scale,
           lam_q1, lam_k1, lam_q2, lam_k2, subln_w, w_out,
           pre_ffn_norm, post_ffn_norm, w_gate, w_up, w_down):
    batch, seq, d_model = x.shape
    depth = w_in.shape[0]
    pool_width = pool_scale.shape[1]
    attn_width = w_out.shape[1] - pool_width
    qk_width = (w_in.shape[2] - pool_width - attn_width) // 2
    n_heads = attn_width // DIFF_V_DIM
    assert qk_width == n_heads * 2 * DIFF_HEAD_DIM
    assert seq % TQ == 0 and seq % TM_IN == 0 and (batch * seq) % TM_FFN == 0
    assert TM_IN == TK

    n = batch * seq
    h = x.reshape(n, d_model)
    pos2 = positions.reshape(n, 1)
    inv_freq = ROPE_THETA ** (-jnp.arange(0, ROT_DIM, 2, dtype=F32) / ROT_DIM)
    lane_dim = jnp.arange(LANES) % DIFF_HEAD_DIM
    inv_freq_lane = jnp.where(lane_dim < ROT_DIM, inv_freq[lane_dim % ROT_HALF], 0.0).reshape(1, LANES)

    for l in range(depth):
        lambda_init = _lambda_init(l)
        pool_out, q, k, vt = _in_proj(
            h, pos2, inv_freq_lane, pre_mix_norm[l].reshape(1, -1), w_in[l], pool_w[l],
            pool_scale[l].reshape(1, -1),
            seq=seq, pool_width=pool_width, qk_width=qk_width, attn_width=attn_width)
        lam_vecs = jnp.stack([lam_q1[l], lam_k1[l], lam_q2[l], lam_k2[l]]).astype(F32)
        attn_out = _attention(q, k, vt, lam_vecs, subln_w[l].reshape(1, -1),
                              batch=batch, seq=seq, n_heads=n_heads, lambda_init=lambda_init)
        h = _out_proj(pool_out, attn_out, h, w_out[l], post_mix_norm[l].reshape(1, -1))
        h = _ffn(h, pre_ffn_norm[l].reshape(1, -1), post_ffn_norm[l].reshape(1, -1),
                 w_gate[l].astype(BF16), w_up[l].astype(BF16), w_down[l].astype(BF16))
    return h.reshape(batch, seq, d_model)
```

```python
import functools
import math

import jax
import jax.numpy as jnp
from jax import lax
from jax.experimental import pallas as pl
from jax.experimental.pallas import tpu as pltpu

F32 = jnp.float32
BF16 = jnp.bfloat16

POOL_WINDOWS = (2, 4, 8, 16)
DIFF_HEAD_DIM = 64
DIFF_V_DIM = 2 * DIFF_HEAD_DIM
ROPE_THETA = 500000.0
ROT_DIM = DIFF_HEAD_DIM // 4
ROT_HALF = ROT_DIM // 2
NORM_EPS = 1e-6
NEG_INF = -1e30
LOG2_E = math.log2(math.e)

LANES = 128
POOL_HALO = 32
VMEM_LIMIT_BYTES = 56 * 1024 * 1024

TM_IN = 512
TQ = 1024
TK = 512
TM_OUT = 512
TM_FFN = 1024
TF_FFN = 512
N_CHUNK = 512
NORM_ROWS = 64
IN_CAST_STEPS = 8
OUT_CAST_STEPS = 4


def _lambda_init(layer_idx):
    return 0.8 - 0.6 * math.exp(-0.3 * layer_idx)


def _rms_norm(xf, g):
    ms = jnp.mean(xf * xf, axis=-1, keepdims=True)
    return xf * lax.rsqrt(ms + NORM_EPS) * g


def _rms_norm_rows(src_ref, g_ref, scale_ref, emit, unroll):
    n = src_ref.shape[0]

    def scale_body(c, carry):
        rows = pl.ds(pl.multiple_of(c * NORM_ROWS, NORM_ROWS), NORM_ROWS)
        x = src_ref[rows, :]
        scale_ref[rows, :] = lax.rsqrt(jnp.mean(x * x, axis=-1, keepdims=True) + NORM_EPS)
        return carry

    lax.fori_loop(0, n // NORM_ROWS, scale_body, 0, unroll=unroll)

    def apply_body(c, carry):
        rows = pl.ds(pl.multiple_of(c * NORM_ROWS, NORM_ROWS), NORM_ROWS)
        emit(rows, src_ref[rows, :] * scale_ref[rows, :] * g_ref[...])
        return carry

    lax.fori_loop(0, n // NORM_ROWS, apply_body, 0, unroll=unroll)


def _in_proj_kernel(x_ref, pos_ref, invf_ref, g_ref, w_ref, pw_ref, ps_ref,
                    pool_ref, q_ref, k_ref, vt_ref, w_bf_ref, wvt_ref, hn_ref, u_ref, carry_ref, nscale_ref,
                    *, seq, pool_width, qk_width):
    tm = x_ref.shape[0]
    step = pl.program_id(0)
    rows_per_cast = w_ref.shape[0]
    n_cast = w_bf_ref.shape[0] // rows_per_cast
    n_direct = w_bf_ref.shape[1]

    @pl.when(step < n_cast)
    def _():
        r0 = pl.multiple_of(step * rows_per_cast, rows_per_cast)
        w_bf_ref[pl.ds(r0, rows_per_cast), :] = w_ref[:, :n_direct].astype(BF16)

    for t in range(n_cast):
        @pl.when(step == t)
        def _():
            wvt_ref[:, t * rows_per_cast:(t + 1) * rows_per_cast] = w_ref[:, n_direct:].T.astype(BF16)

    @pl.when(step >= n_cast)
    def _():
        tiles_per_seq = seq // tm
        ti = (step - n_cast) % tiles_per_seq

        def store_hn(rows, y):
            hn_ref[rows, :] = y.astype(BF16)

        _rms_norm_rows(x_ref, g_ref, nscale_ref, store_hn, unroll=2)

        @pl.when(ti == 0)
        def _():
            carry_ref[...] = jnp.zeros_like(carry_ref)

        lane = lax.broadcasted_iota(jnp.int32, (1, LANES), 1) % DIFF_HEAD_DIM
        ang = pos_ref[...].astype(F32) * invf_ref[...]
        cos, sin = jnp.cos(ang), jnp.sin(ang)
        coef_self = jnp.where(lane < ROT_DIM, cos, 1.0)
        coef_lo = jnp.where((lane >= ROT_HALF) & (lane < ROT_DIM), sin, 0.0)
        coef_hi = jnp.where(lane < ROT_HALF, -sin, 0.0)

        def rope_chunk(out_ref, chunk, c, scale):
            t = jnp.dot(hn_ref[...], w_bf_ref[:, chunk * N_CHUNK:(chunk + 1) * N_CHUNK],
                        preferred_element_type=F32)
            for h in range(N_CHUNK // LANES):
                th = t[:, h * LANES:(h + 1) * LANES]
                r = (th * coef_self + pltpu.roll(th, ROT_HALF, 1) * coef_lo
                     + pltpu.roll(th, LANES - ROT_HALF, 1) * coef_hi)
                if scale != 1.0:
                    r = r * scale
                out_ref[:, c + h * LANES:c + (h + 1) * LANES] = r.astype(BF16)

        gdim = pool_width // len(POOL_WINDOWS)
        t_in_seq = ti * tm + lax.broadcasted_iota(jnp.int32, (tm, 1), 0)

        def pool_group(g, w):
            c0 = g * gdim
            u = u_ref[:, c0:c0 + gdim]
            ext = jnp.concatenate([carry_ref[g], u], axis=0)
            carry_ref[g] = u[tm - POOL_HALO:, :]
            lvl, off, k = ext, 0, 1
            while k < w:
                new_off = min(off + 8, POOL_HALO)
                cur = lvl[new_off - off:, :]
                shifted = lvl[new_off - off - k: lvl.shape[0] - k, :]
                lvl, off, k = cur + shifted, new_off, 2 * k
            wsum = lvl[POOL_HALO - off:, :]
            cnt = jnp.minimum(t_in_seq + 1, w).astype(F32)
            pooled = (wsum / cnt - u).astype(BF16)
            mixed = jnp.dot(pooled, pw_ref[g].astype(BF16), preferred_element_type=F32)
            pool_ref[:, c0:c0 + gdim] = (mixed * ps_ref[:, c0:c0 + gdim]).astype(BF16)

        n_pool, n_qk = pool_width // N_CHUNK, qk_width // N_CHUNK
        for ci in range(n_pool):
            u_ref[:, ci * N_CHUNK:(ci + 1) * N_CHUNK] = jnp.dot(
                hn_ref[...], w_bf_ref[:, ci * N_CHUNK:(ci + 1) * N_CHUNK], preferred_element_type=F32)

        q_scale = DIFF_HEAD_DIM ** -0.5 * LOG2_E
        chunks = ([(q_ref, n_pool + ci, ci * N_CHUNK, q_scale) for ci in range(n_qk)]
                  + [(k_ref, n_pool + n_qk + ci, ci * N_CHUNK, 1.0) for ci in range(n_qk)])
        groups = list(enumerate(POOL_WINDOWS))
        for idx, args in enumerate(chunks):
            rope_chunk(*args)
            for g, w in groups[idx * len(groups) // len(chunks):(idx + 1) * len(groups) // len(chunks)]:
                pool_group(g, w)

        for c in range(0, vt_ref.shape[0], N_CHUNK):
            vt_ref[c:c + N_CHUNK, :] = lax.dot_general(
                wvt_ref[c:c + N_CHUNK, :], hn_ref[...], (((1,), (1,)), ((), ())),
                preferred_element_type=F32).astype(BF16)


def _in_proj(x2, pos2, inv_freq, g, w_in, pool_w, pool_scale, *, seq, pool_width, qk_width, attn_width):
    n, d = x2.shape
    tm = TM_IN
    n_groups = len(POOL_WINDOWS)
    gdim = pool_width // n_groups
    n_direct = pool_width + 2 * qk_width
    n_cast = IN_CAST_STEPS
    const = lambda s: (0, 0)
    row = lambda s: (jnp.maximum(s - n_cast, 0), 0)
    kern = functools.partial(_in_proj_kernel, seq=seq, pool_width=pool_width, qk_width=qk_width)
    return pl.pallas_call(
        kern,
        grid=(n_cast + n // tm,),
        in_specs=[
            pl.BlockSpec((tm, d), row),
            pl.BlockSpec((tm, 1), row),
            pl.BlockSpec((1, LANES), const),
            pl.BlockSpec((1, d), const),
            pl.BlockSpec((d // n_cast, w_in.shape[1]), lambda s: (jnp.minimum(s, n_cast - 1), 0)),
            pl.BlockSpec(pool_w.shape, lambda s: (0, 0, 0)),
            pl.BlockSpec((1, pool_width), const),
        ],
        out_specs=[
            pl.BlockSpec((tm, pool_width), row),
            pl.BlockSpec((tm, qk_width), row),
            pl.BlockSpec((tm, qk_width), row),
            pl.BlockSpec((None, attn_width, tm), lambda s: (jnp.maximum(s - n_cast, 0), 0, 0)),
        ],
        out_shape=[
            jax.ShapeDtypeStruct((n, pool_width), BF16),
            jax.ShapeDtypeStruct((n, qk_width), BF16),
            jax.ShapeDtypeStruct((n, qk_width), BF16),
            jax.ShapeDtypeStruct((n // tm, attn_width, tm), BF16),
        ],
        scratch_shapes=[
            pltpu.VMEM((d, n_direct), BF16),
            pltpu.VMEM((attn_width, d), BF16),
            pltpu.VMEM((tm, d), BF16),
            pltpu.VMEM((tm, pool_width), F32),
            pltpu.VMEM((n_groups, POOL_HALO, gdim), F32),
            pltpu.VMEM((tm, 1), F32),
        ],
        compiler_params=pltpu.CompilerParams(
            dimension_semantics=("arbitrary",), vmem_limit_bytes=VMEM_LIMIT_BYTES),
        name="in_proj",
    )(x2, pos2, inv_freq, g, w_in, pool_w, pool_scale)


def _attn_kernel(lam_ref, sw_ref, q_ref, k_ref, vt_ref, o_ref,
                 qc_ref, s0_ref, s1_ref, mx0_ref, mx1_ref, m_ref, l_ref, acc_ref, *, lambda_init):
    tq = q_ref.shape[0]
    qi = pl.program_id(2)
    s_slots, mx_slots = (s0_ref, s1_ref), (mx0_ref, mx1_ref)

    q = q_ref[...]
    lane = lax.broadcasted_iota(jnp.int32, q.shape, 1)
    zero = jnp.zeros_like(q)
    qc_ref[0] = jnp.where(lane < DIFF_HEAD_DIM, q, zero)
    qc_ref[1] = jnp.where(lane >= DIFF_HEAD_DIM, q, zero)

    m_ref[...] = jnp.full_like(m_ref, NEG_INF)
    l_ref[...] = jnp.zeros_like(l_ref)
    acc_ref[...] = jnp.zeros_like(acc_ref)

    def scores(j, slot, c, mask):
        start = pl.multiple_of(j * TK, TK)
        s = lax.dot_general(k_ref[pl.ds(start, TK), :], qc_ref[c], (((1,), (1,)), ((), ())),
                            preferred_element_type=F32)
        if mask is not None:
            kv = lax.broadcasted_iota(jnp.int32, (TK, tq), 0)
            r = lax.broadcasted_iota(jnp.int32, (TK, tq), 1)
            keep = (kv + TK <= r) if mask == "hi" else (kv <= r)
            if mask == "lo_if_first":
                keep = keep | (qi > 0)
            s = jnp.where(keep, s, NEG_INF)
        s_slots[slot][c] = s
        mx_slots[slot][c] = jnp.max(s, axis=0, keepdims=True)

    def exp_pv(j, slot, c):
        m_prev = m_ref[c]
        m_new = jnp.maximum(m_prev, mx_slots[slot][c])
        alpha = jnp.exp2(m_prev - m_new)
        p = jnp.exp2(s_slots[slot][c] - m_new)
        l_ref[c] = alpha * l_ref[c] + jnp.sum(p, axis=0, keepdims=True)
        pv = jnp.dot(vt_ref[j], p.astype(BF16), preferred_element_type=F32)
        acc_ref[c] = alpha * acc_ref[c] + pv
        m_ref[c] = m_new

    def stage(j, slot, mask):
        for c in range(2):
            scores(j + 1, 1 - slot, c, mask)
            exp_pv(j, slot, c)

    for c in range(2):
        scores(0, 0, c, "lo_if_first")

    def body(t, carry):
        stage(2 * t, 0, None)
        stage(2 * t + 1, 1, None)
        return carry

    lax.fori_loop(0, qi - 1, body, 0)

    @pl.when(qi > 0)
    def _():
        stage(2 * qi - 2, 0, None)
        stage(2 * qi - 1, 1, "lo")

    stage(2 * qi, 0, "hi")
    for c in range(2):
        exp_pv(2 * qi + 1, 1, c)

    lam_v = lam_ref[...]
    lam = (jnp.exp(jnp.sum(lam_v[0:1] * lam_v[1:2], keepdims=True))
           - jnp.exp(jnp.sum(lam_v[2:3] * lam_v[3:4], keepdims=True)) + lambda_init)
    attn_t = acc_ref[0] / l_ref[0] - lam * (acc_ref[1] / l_ref[1])
    ms = jnp.mean(attn_t * attn_t, axis=0, keepdims=True)
    y = (attn_t * lax.rsqrt(ms + NORM_EPS)).T
    o_ref[...] = (y * sw_ref[...] * (1.0 - lambda_init)).astype(BF16)


def _attention(q, k, vt, lam_vecs, subln_w, *, batch, seq, n_heads, lambda_init):
    n, width = q.shape
    nq = seq // TQ
    nk = seq // TK
    assert TQ == 2 * TK and vt.shape == (n // TK, n_heads * DIFF_V_DIM, TK)
    kern = functools.partial(_attn_kernel, lambda_init=lambda_init)
    q_map = lambda b, h, i: (b * nq + i, h)
    return pl.pallas_call(
        kern,
        grid=(batch, n_heads, nq),
        in_specs=[
            pl.BlockSpec(lam_vecs.shape, lambda b, h, i: (0, 0)),
            pl.BlockSpec((1, DIFF_V_DIM), lambda b, h, i: (0, 0)),
            pl.BlockSpec((TQ, DIFF_V_DIM), q_map),
            pl.BlockSpec((seq, DIFF_V_DIM), lambda b, h, i: (b, h)),
            pl.BlockSpec((nk, DIFF_V_DIM, TK), lambda b, h, i: (b, h, 0)),
        ],
        out_specs=pl.BlockSpec((TQ, DIFF_V_DIM), q_map),
        out_shape=jax.ShapeDtypeStruct((n, width), BF16),
        scratch_shapes=[
            pltpu.VMEM((2, TQ, DIFF_V_DIM), BF16),
            pltpu.VMEM((2, TK, TQ), F32),
            pltpu.VMEM((2, TK, TQ), F32),
            pltpu.VMEM((2, 1, TQ), F32),
            pltpu.VMEM((2, 1, TQ), F32),
            pltpu.VMEM((2, 1, TQ), F32),
            pltpu.VMEM((2, 1, TQ), F32),
            pltpu.VMEM((2, DIFF_V_DIM, TQ), F32),
        ],
        compiler_params=pltpu.CompilerParams(
            dimension_semantics=("arbitrary", "arbitrary", "arbitrary"), vmem_limit_bytes=VMEM_LIMIT_BYTES),
        name="diff_attention",
    )(lam_vecs, subln_w, q, k, vt)


def _out_proj_kernel(pool_ref, attn_ref, x_ref, w_ref, g_ref, h_ref, w_bf_ref, mix_ref, nscale_ref):
    step = pl.program_id(0)
    rows_per_cast = w_ref.shape[0]
    n_cast = w_bf_ref.shape[0] // rows_per_cast
    pw = pool_ref.shape[1]

    @pl.when(step < n_cast)
    def _():
        r0 = pl.multiple_of(step * rows_per_cast, rows_per_cast)
        w_bf_ref[pl.ds(r0, rows_per_cast), :] = w_ref[...].astype(BF16)

    @pl.when(step >= n_cast)
    def _():
        for c in range(0, mix_ref.shape[1], N_CHUNK):
            mix_ref[:, c:c + N_CHUNK] = (
                jnp.dot(pool_ref[...], w_bf_ref[:pw, c:c + N_CHUNK], preferred_element_type=F32)
                + jnp.dot(attn_ref[...], w_bf_ref[pw:, c:c + N_CHUNK], preferred_element_type=F32))

        def store_h(rows, y):
            h_ref[rows, :] = x_ref[rows, :] + y

        _rms_norm_rows(mix_ref, g_ref, nscale_ref, store_h, unroll=2)


def _out_proj(pool_out, attn_out, x2, w_out, g):
    n, d = x2.shape
    tm = TM_OUT
    n_cast = OUT_CAST_STEPS
    row = lambda s: (jnp.maximum(s - n_cast, 0), 0)
    const = lambda s: (0, 0)
    return pl.pallas_call(
        _out_proj_kernel,
        grid=(n_cast + n // tm,),
        in_specs=[
            pl.BlockSpec((tm, pool_out.shape[1]), row),
            pl.BlockSpec((tm, attn_out.shape[1]), row),
            pl.BlockSpec((tm, d), row),
            pl.BlockSpec((w_out.shape[0] // n_cast, w_out.shape[1]), lambda s: (jnp.minimum(s, n_cast - 1), 0)),
            pl.BlockSpec((1, d), const),
        ],
        out_specs=pl.BlockSpec((tm, d), row),
        out_shape=jax.ShapeDtypeStruct((n, d), F32),
        scratch_shapes=[
            pltpu.VMEM(w_out.shape, BF16),
            pltpu.VMEM((tm, d), F32),
            pltpu.VMEM((tm, 1), F32),
        ],
        compiler_params=pltpu.CompilerParams(
            dimension_semantics=("arbitrary",), vmem_limit_bytes=VMEM_LIMIT_BYTES),
        name="out_proj",
    )(pool_out, attn_out, x2, w_out, g)


def _ffn_kernel(h_ref, gpre_ref, gpost_ref, wg_ref, wu_ref, wd_ref, o_ref, hn_ref, nscale_ref):
    f = pl.program_id(1)

    @pl.when(f == 0)
    def _():
        def store_hn(rows, y):
            hn_ref[rows, :] = y.astype(BF16)
            o_ref[rows, :] = jnp.zeros_like(y)

        _rms_norm_rows(h_ref, gpre_ref, nscale_ref, store_hn, unroll=True)

    hn = hn_ref[...]
    gate = jnp.dot(hn, wg_ref[...], preferred_element_type=F32)
    up = jnp.dot(hn, wu_ref[...], preferred_element_type=F32)
    act = (gate * jax.nn.sigmoid(gate) * up).astype(BF16)
    for c in range(0, o_ref.shape[1], N_CHUNK):
        o_ref[:, c:c + N_CHUNK] += jnp.dot(act, wd_ref[:, c:c + N_CHUNK], preferred_element_type=F32)

    @pl.when(f == pl.num_programs(1) - 1)
    def _():
        def store_out(rows, y):
            o_ref[rows, :] = h_ref[rows, :] + y

        _rms_norm_rows(o_ref, gpost_ref, nscale_ref, store_out, unroll=True)


def _column_blocks(w):
    d, d_ff = w.shape
    return w.astype(BF16).reshape(d, d_ff // TF_FFN, TF_FFN).transpose(1, 0, 2)


def _ffn(h, g_pre, g_post, w_gate, w_up, w_down):
    n, d = h.shape
    n_f, _, tf = w_gate.shape
    tm = TM_FFN
    return pl.pallas_call(
        _ffn_kernel,
        grid=(n // tm, n_f),
        in_specs=[
            pl.BlockSpec((tm, d), lambda i, f: (i, 0), pipeline_mode=pl.Buffered(1)),
            pl.BlockSpec((1, d), lambda i, f: (0, 0)),
            pl.BlockSpec((1, d), lambda i, f: (0, 0)),
            pl.BlockSpec((None, d, tf), lambda i, f: (f, 0, 0)),
            pl.BlockSpec((None, d, tf), lambda i, f: (f, 0, 0)),
            pl.BlockSpec((tf, d), lambda i, f: (f, 0)),
        ],
        out_specs=pl.BlockSpec((tm, d), lambda i, f: (i, 0)),
        out_shape=jax.ShapeDtypeStruct((n, d), F32),
        scratch_shapes=[pltpu.VMEM((tm, d), BF16), pltpu.VMEM((tm, 1), F32)],
        compiler_params=pltpu.CompilerParams(
            dimension_semantics=("arbitrary", "arbitrary"), vmem_limit_bytes=VMEM_LIMIT_BYTES),
        name="ffn",
    )(h, g_pre, g_post, w_gate, w_up, w_down)


def kernel(x, positions, pre_mix_norm, post_mix_norm, w_in, pool_w, pool_scale,
           lam_q1, lam_k1, lam_q2, lam_k2, subln_w, w_out,
           pre_ffn_norm, post_ffn_norm, w_gate, w_up, w_down):
    batch, seq, d_model = x.shape
    depth = w_in.shape[0]
    pool_width = pool_scale.shape[1]
    attn_width = w_out.shape[1] - pool_width
    qk_width = (w_in.shape[2] - pool_width - attn_width) // 2
    n_heads = attn_width // DIFF_V_DIM
    assert qk_width == n_heads * 2 * DIFF_HEAD_DIM
    assert seq % TQ == 0 and seq % TM_IN == 0 and (batch * seq) % TM_FFN == 0
    assert TM_IN == TK

    n = batch * seq
    h = x.reshape(n, d_model)
    pos2 = positions.reshape(n, 1)
    inv_freq = ROPE_THETA ** (-jnp.arange(0, ROT_DIM, 2, dtype=F32) / ROT_DIM)
    lane_dim = jnp.arange(LANES) % DIFF_HEAD_DIM
    inv_freq_lane = jnp.where(lane_dim < ROT_DIM, inv_freq[lane_dim % ROT_HALF], 0.0).reshape(1, LANES)

    for l in range(depth):
        lambda_init = _lambda_init(l)
        pool_out, q, k, vt = _in_proj(
            h, pos2, inv_freq_lane, pre_mix_norm[l].reshape(1, -1), w_in[l], pool_w[l],
            pool_scale[l].reshape(1, -1),
            seq=seq, pool_width=pool_width, qk_width=qk_width, attn_width=attn_width)
        lam_vecs = jnp.stack([lam_q1[l], lam_k1[l], lam_q2[l], lam_k2[l]]).astype(F32)
        attn_out = _attention(q, k, vt, lam_vecs, subln_w[l].reshape(1, -1),
                              batch=batch, seq=seq, n_heads=n_heads, lambda_init=lambda_init)
        h = _out_proj(pool_out, attn_out, h, w_out[l], post_mix_norm[l].reshape(1, -1))
        h = _ffn(h, pre_ffn_norm[l].reshape(1, -1), post_ffn_norm[l].reshape(1, -1),
                 _column_blocks(w_gate[l]), _column_blocks(w_up[l]), w_down[l].astype(BF16))
    return h.reshape(batch, seq, d_model)
```

```python
import functools
import math

import jax
import jax.numpy as jnp
from jax import lax
from jax.experimental import pallas as pl
from jax.experimental.pallas import tpu as pltpu

F32 = jnp.float32
BF16 = jnp.bfloat16

POOL_WINDOWS = (2, 4, 8, 16)
DIFF_HEAD_DIM = 64
DIFF_V_DIM = 2 * DIFF_HEAD_DIM
ROPE_THETA = 500000.0
ROT_DIM = DIFF_HEAD_DIM // 4
ROT_HALF = ROT_DIM // 2
NORM_EPS = 1e-6
NEG_INF = -1e30
LOG2_E = math.log2(math.e)

LANES = 128
POOL_HALO = 32
VMEM_LIMIT_BYTES = 56 * 1024 * 1024

TM_IN = 512
TQ = 1024
TK = 512
TM_OUT = 512
TM_FFN = 1024
TF_FFN = 512
N_CHUNK = 512
NORM_ROWS = 64
IN_CAST_STEPS = 8
OUT_CAST_STEPS = 4


def _lambda_init(layer_idx):
    return 0.8 - 0.6 * math.exp(-0.3 * layer_idx)


def _rms_norm(xf, g):
    ms = jnp.mean(xf * xf, axis=-1, keepdims=True)
    return xf * lax.rsqrt(ms + NORM_EPS) * g


def _rms_norm_rows(src_ref, g_ref, scale_ref, emit, unroll):
    n = src_ref.shape[0]

    def scale_body(c, carry):
        rows = pl.ds(pl.multiple_of(c * NORM_ROWS, NORM_ROWS), NORM_ROWS)
        x = src_ref[rows, :]
        scale_ref[rows, :] = lax.rsqrt(jnp.mean(x * x, axis=-1, keepdims=True) + NORM_EPS)
        return carry

    lax.fori_loop(0, n // NORM_ROWS, scale_body, 0, unroll=unroll)

    def apply_body(c, carry):
        rows = pl.ds(pl.multiple_of(c * NORM_ROWS, NORM_ROWS), NORM_ROWS)
        emit(rows, src_ref[rows, :] * scale_ref[rows, :] * g_ref[...])
        return carry

    lax.fori_loop(0, n // NORM_ROWS, apply_body, 0, unroll=unroll)


def _in_proj_kernel(x_ref, pos_ref, invf_ref, g_ref, w_ref, pw_ref, ps_ref,
                    pool_ref, q_ref, k_ref, vt_ref, w_bf_ref, wvt_ref, hn_ref, u_ref, carry_ref, nscale_ref,
                    *, seq, pool_width, qk_width):
    tm = x_ref.shape[0]
    step = pl.program_id(0)
    rows_per_cast = w_ref.shape[0]
    n_cast = w_bf_ref.shape[0] // rows_per_cast
    n_direct = w_bf_ref.shape[1]

    @pl.when(step < n_cast)
    def _():
        r0 = pl.multiple_of(step * rows_per_cast, rows_per_cast)
        w_bf_ref[pl.ds(r0, rows_per_cast), :] = w_ref[:, :n_direct].astype(BF16)

    for t in range(n_cast):
        @pl.when(step == t)
        def _():
            wvt_ref[:, t * rows_per_cast:(t + 1) * rows_per_cast] = w_ref[:, n_direct:].T.astype(BF16)

    @pl.when(step >= n_cast)
    def _():
        tiles_per_seq = seq // tm
        ti = (step - n_cast) % tiles_per_seq

        hn_ref[...] = _rms_norm(x_ref[...], g_ref[...]).astype(BF16)

        @pl.when(ti == 0)
        def _():
            carry_ref[...] = jnp.zeros_like(carry_ref)

        lane = lax.broadcasted_iota(jnp.int32, (1, LANES), 1) % DIFF_HEAD_DIM
        ang = pos_ref[...].astype(F32) * invf_ref[...]
        cos, sin = jnp.cos(ang), jnp.sin(ang)
        coef_self = jnp.where(lane < ROT_DIM, cos, 1.0)
        coef_lo = jnp.where((lane >= ROT_HALF) & (lane < ROT_DIM), sin, 0.0)
        coef_hi = jnp.where(lane < ROT_HALF, -sin, 0.0)

        def rope_chunk(out_ref, chunk, c, scale):
            t = jnp.dot(hn_ref[...], w_bf_ref[:, chunk * N_CHUNK:(chunk + 1) * N_CHUNK],
                        preferred_element_type=F32)
            for h in range(N_CHUNK // LANES):
                th = t[:, h * LANES:(h + 1) * LANES]
                r = (th * coef_self + pltpu.roll(th, ROT_HALF, 1) * coef_lo
                     + pltpu.roll(th, LANES - ROT_HALF, 1) * coef_hi)
                if scale != 1.0:
                    r = r * scale
                out_ref[:, c + h * LANES:c + (h + 1) * LANES] = r.astype(BF16)

        gdim = pool_width // len(POOL_WINDOWS)
        t_in_seq = ti * tm + lax.broadcasted_iota(jnp.int32, (tm, 1), 0)

        def pool_group(g, w):
            c0 = g * gdim
            u = u_ref[:, c0:c0 + gdim]
            ext = jnp.concatenate([carry_ref[g], u], axis=0)
            carry_ref[g] = u[tm - POOL_HALO:, :]
            lvl, off, k = ext, 0, 1
            while k < w:
                new_off = min(off + 8, POOL_HALO)
                cur = lvl[new_off - off:, :]
                shifted = lvl[new_off - off - k: lvl.shape[0] - k, :]
                lvl, off, k = cur + shifted, new_off, 2 * k
            wsum = lvl[POOL_HALO - off:, :]
            cnt = jnp.minimum(t_in_seq + 1, w).astype(F32)
            pooled = (wsum / cnt - u).astype(BF16)
            mixed = jnp.dot(pooled, pw_ref[g].astype(BF16), preferred_element_type=F32)
            pool_ref[:, c0:c0 + gdim] = (mixed * ps_ref[:, c0:c0 + gdim]).astype(BF16)

        n_pool, n_qk = pool_width // N_CHUNK, qk_width // N_CHUNK
        for ci in range(n_pool):
            u_ref[:, ci * N_CHUNK:(ci + 1) * N_CHUNK] = jnp.dot(
                hn_ref[...], w_bf_ref[:, ci * N_CHUNK:(ci + 1) * N_CHUNK], preferred_element_type=F32)

        q_scale = DIFF_HEAD_DIM ** -0.5 * LOG2_E
        chunks = ([(q_ref, n_pool + ci, ci * N_CHUNK, q_scale) for ci in range(n_qk)]
                  + [(k_ref, n_pool + n_qk + ci, ci * N_CHUNK, 1.0) for ci in range(n_qk)])
        groups = list(enumerate(POOL_WINDOWS))
        for idx, args in enumerate(chunks):
            rope_chunk(*args)
            for g, w in groups[idx * len(groups) // len(chunks):(idx + 1) * len(groups) // len(chunks)]:
                pool_group(g, w)

        for c in range(0, vt_ref.shape[0], N_CHUNK):
            vt_ref[c:c + N_CHUNK, :] = lax.dot_general(
                wvt_ref[c:c + N_CHUNK, :], hn_ref[...], (((1,), (1,)), ((), ())),
                preferred_element_type=F32).astype(BF16)


def _in_proj(x2, pos2, inv_freq, g, w_in, pool_w, pool_scale, *, seq, pool_width, qk_width, attn_width):
    n, d = x2.shape
    tm = TM_IN
    n_groups = len(POOL_WINDOWS)
    gdim = pool_width // n_groups
    n_direct = pool_width + 2 * qk_width
    n_cast = IN_CAST_STEPS
    const = lambda s: (0, 0)
    row = lambda s: (jnp.maximum(s - n_cast, 0), 0)
    kern = functools.partial(_in_proj_kernel, seq=seq, pool_width=pool_width, qk_width=qk_width)
    return pl.pallas_call(
        kern,
        grid=(n_cast + n // tm,),
        in_specs=[
            pl.BlockSpec((tm, d), row),
            pl.BlockSpec((tm, 1), row),
            pl.BlockSpec((1, LANES), const),
            pl.BlockSpec((1, d), const),
            pl.BlockSpec((d // n_cast, w_in.shape[1]), lambda s: (jnp.minimum(s, n_cast - 1), 0)),
            pl.BlockSpec(pool_w.shape, lambda s: (0, 0, 0)),
            pl.BlockSpec((1, pool_width), const),
        ],
        out_specs=[
            pl.BlockSpec((tm, pool_width), row),
            pl.BlockSpec((tm, qk_width), row),
            pl.BlockSpec((tm, qk_width), row),
            pl.BlockSpec((None, attn_width, tm), lambda s: (jnp.maximum(s - n_cast, 0), 0, 0)),
        ],
        out_shape=[
            jax.ShapeDtypeStruct((n, pool_width), BF16),
            jax.ShapeDtypeStruct((n, qk_width), BF16),
            jax.ShapeDtypeStruct((n, qk_width), BF16),
            jax.ShapeDtypeStruct((n // tm, attn_width, tm), BF16),
        ],
        scratch_shapes=[
            pltpu.VMEM((d, n_direct), BF16),
            pltpu.VMEM((attn_width, d), BF16),
            pltpu.VMEM((tm, d), BF16),
            pltpu.VMEM((tm, pool_width), F32),
            pltpu.VMEM((n_groups, POOL_HALO, gdim), F32),
            pltpu.VMEM((tm, 1), F32),
        ],
        compiler_params=pltpu.CompilerParams(
            dimension_semantics=("arbitrary",), vmem_limit_bytes=VMEM_LIMIT_BYTES),
        name="in_proj",
    )(x2, pos2, inv_freq, g, w_in, pool_w, pool_scale)


def _attn_kernel(lam_ref, sw_ref, q_ref, k_ref, vt_ref, o_ref,
                 qc_ref, s0_ref, s1_ref, mx0_ref, mx1_ref, m_ref, l_ref, acc_ref, *, lambda_init):
    seq = q_ref.shape[0]
    n_tiles = seq // TQ
    s_slots, mx_slots = (s0_ref, s1_ref), (mx0_ref, mx1_ref)
    lane = lax.broadcasted_iota(jnp.int32, (TQ, DIFF_V_DIM), 1)
    hi_cols = slice(TK, TQ)
    all_cols = slice(0, TQ)

    def rows(i, size):
        return pl.ds(i * size if isinstance(i, int) else pl.multiple_of(i * size, size), size)

    def load_q(qi):
        q = q_ref[rows(qi, TQ), :]
        zero = jnp.zeros_like(q)
        qc_ref[0] = jnp.where(lane < DIFF_HEAD_DIM, q, zero)
        qc_ref[1] = jnp.where(lane >= DIFF_HEAD_DIM, q, zero)

    def scores(j, slot, c, cs, masked):
        width = cs.stop - cs.start
        s = lax.dot_general(k_ref[rows(j, TK), :], qc_ref[c, cs, :],
                            (((1,), (1,)), ((), ())), preferred_element_type=F32)
        if masked:
            kv = lax.broadcasted_iota(jnp.int32, (TK, width), 0)
            r = lax.broadcasted_iota(jnp.int32, (TK, width), 1)
            s = jnp.where(kv <= r, s, NEG_INF)
        s_slots[slot][c, :, 0:width] = s
        mx_slots[slot][c, :, 0:width] = jnp.max(s, axis=0, keepdims=True)

    def exp_pv(j, slot, c, cs):
        width = cs.stop - cs.start
        m_prev = m_ref[c, :, cs]
        m_new = jnp.maximum(m_prev, mx_slots[slot][c, :, 0:width])
        alpha = jnp.exp2(m_prev - m_new)
        p = jnp.exp2(s_slots[slot][c, :, 0:width] - m_new)
        l_ref[c, :, cs] = alpha * l_ref[c, :, cs] + jnp.sum(p, axis=0, keepdims=True)
        pv = jnp.dot(vt_ref[j], p.astype(BF16), preferred_element_type=F32)
        acc_ref[c, :, cs] = alpha * acc_ref[c, :, cs] + pv
        m_ref[c, :, cs] = m_new

    def stage(j, slot, next_cols=all_cols, next_masked=False, cur_cols=all_cols):
        for c in range(2):
            scores(j + 1, 1 - slot, c, next_cols, next_masked)
            exp_pv(j, slot, c, cur_cols)

    def tile(qi, carry):
        m_ref[...] = jnp.full_like(m_ref, NEG_INF)
        l_ref[...] = jnp.zeros_like(l_ref)
        acc_ref[...] = jnp.zeros_like(acc_ref)

        def pair(t, c2):
            stage(2 * t, 0)
            stage(2 * t + 1, 1)
            return c2

        lax.fori_loop(0, qi - 1, pair, 0)

        @pl.when(qi > 0)
        def _():
            stage(2 * qi - 2, 0)
            stage(2 * qi - 1, 1, next_masked=True)

        stage(2 * qi, 0, next_cols=hi_cols, next_masked=True)
        nxt = jnp.minimum(qi + 1, n_tiles - 1)
        load_q(nxt)
        for c in range(2):
            scores(0, 0, c, all_cols, False)
            exp_pv(2 * qi + 1, 1, c, hi_cols)

        lam_v = lam_ref[...]
        lam = (jnp.exp(jnp.sum(lam_v[0:1] * lam_v[1:2], keepdims=True))
               - jnp.exp(jnp.sum(lam_v[2:3] * lam_v[3:4], keepdims=True)) + lambda_init)
        attn_t = acc_ref[0] / l_ref[0] - lam * (acc_ref[1] / l_ref[1])
        ms = jnp.mean(attn_t * attn_t, axis=0, keepdims=True)
        y = (attn_t * lax.rsqrt(ms + NORM_EPS)).T
        o_ref[rows(qi, TQ), :] = (y * sw_ref[...] * (1.0 - lambda_init)).astype(BF16)
        return carry

    load_q(0)
    for c in range(2):
        scores(0, 0, c, all_cols, True)
    lax.fori_loop(0, n_tiles, tile, 0)


def _attention(q, k, vt, lam_vecs, subln_w, *, batch, seq, n_heads, lambda_init):
    n, width = q.shape
    nk = seq // TK
    assert TQ == 2 * TK and seq % TQ == 0 and vt.shape == (n // TK, n_heads * DIFF_V_DIM, TK)
    kern = functools.partial(_attn_kernel, lambda_init=lambda_init)
    head = lambda b, h: (b, h)
    return pl.pallas_call(
        kern,
        grid=(batch, n_heads),
        in_specs=[
            pl.BlockSpec(lam_vecs.shape, lambda b, h: (0, 0)),
            pl.BlockSpec((1, DIFF_V_DIM), lambda b, h: (0, 0)),
            pl.BlockSpec((seq, DIFF_V_DIM), head),
            pl.BlockSpec((seq, DIFF_V_DIM), head),
            pl.BlockSpec((nk, DIFF_V_DIM, TK), lambda b, h: (b, h, 0)),
        ],
        out_specs=pl.BlockSpec((seq, DIFF_V_DIM), head),
        out_shape=jax.ShapeDtypeStruct((n, width), BF16),
        scratch_shapes=[
            pltpu.VMEM((2, TQ, DIFF_V_DIM), BF16),
            pltpu.VMEM((2, TK, TQ), F32),
            pltpu.VMEM((2, TK, TQ), F32),
            pltpu.VMEM((2, 1, TQ), F32),
            pltpu.VMEM((2, 1, TQ), F32),
            pltpu.VMEM((2, 1, TQ), F32),
            pltpu.VMEM((2, 1, TQ), F32),
            pltpu.VMEM((2, DIFF_V_DIM, TQ), F32),
        ],
        compiler_params=pltpu.CompilerParams(
            dimension_semantics=("arbitrary", "arbitrary"), vmem_limit_bytes=VMEM_LIMIT_BYTES),
        name="diff_attention",
    )(lam_vecs, subln_w, q, k, vt)


def _out_proj_kernel(pool_ref, attn_ref, x_ref, w_ref, g_ref, h_ref, w_bf_ref, mix_ref, nscale_ref):
    step = pl.program_id(0)
    rows_per_cast = w_ref.shape[0]
    n_cast = w_bf_ref.shape[0] // rows_per_cast
    pw = pool_ref.shape[1]

    @pl.when(step < n_cast)
    def _():
        r0 = pl.multiple_of(step * rows_per_cast, rows_per_cast)
        w_bf_ref[pl.ds(r0, rows_per_cast), :] = w_ref[...].astype(BF16)

    @pl.when(step >= n_cast)
    def _():
        for c in range(0, mix_ref.shape[1], N_CHUNK):
            mix_ref[:, c:c + N_CHUNK] = (
                jnp.dot(pool_ref[...], w_bf_ref[:pw, c:c + N_CHUNK], preferred_element_type=F32)
                + jnp.dot(attn_ref[...], w_bf_ref[pw:, c:c + N_CHUNK], preferred_element_type=F32))

        def store_h(rows, y):
            h_ref[rows, :] = x_ref[rows, :] + y

        _rms_norm_rows(mix_ref, g_ref, nscale_ref, store_h, unroll=2)


def _out_proj(pool_out, attn_out, x2, w_out, g):
    n, d = x2.shape
    tm = TM_OUT
    n_cast = OUT_CAST_STEPS
    row = lambda s: (jnp.maximum(s - n_cast, 0), 0)
    const = lambda s: (0, 0)
    return pl.pallas_call(
        _out_proj_kernel,
        grid=(n_cast + n // tm,),
        in_specs=[
            pl.BlockSpec((tm, pool_out.shape[1]), row),
            pl.BlockSpec((tm, attn_out.shape[1]), row),
            pl.BlockSpec((tm, d), row),
            pl.BlockSpec((w_out.shape[0] // n_cast, w_out.shape[1]), lambda s: (jnp.minimum(s, n_cast - 1), 0)),
            pl.BlockSpec((1, d), const),
        ],
        out_specs=pl.BlockSpec((tm, d), row),
        out_shape=jax.ShapeDtypeStruct((n, d), F32),
        scratch_shapes=[
            pltpu.VMEM(w_out.shape, BF16),
            pltpu.VMEM((tm, d), F32),
            pltpu.VMEM((tm, 1), F32),
        ],
        compiler_params=pltpu.CompilerParams(
            dimension_semantics=("arbitrary",), vmem_limit_bytes=VMEM_LIMIT_BYTES),
        name="out_proj",
    )(pool_out, attn_out, x2, w_out, g)


def _ffn_kernel(h_ref, gpre_ref, gpost_ref, wg_ref, wu_ref, wd_ref, o_ref, hn_ref, nscale_ref):
    f = pl.program_id(1)

    @pl.when(f == 0)
    def _():
        def store_hn(rows, y):
            hn_ref[rows, :] = y.astype(BF16)
            o_ref[rows, :] = jnp.zeros_like(y)

        _rms_norm_rows(h_ref, gpre_ref, nscale_ref, store_hn, unroll=True)

    hn = hn_ref[...]
    gate = jnp.dot(hn, wg_ref[...], preferred_element_type=F32)
    up = jnp.dot(hn, wu_ref[...], preferred_element_type=F32)
    act = (gate * jax.nn.sigmoid(gate) * up).astype(BF16)
    for c in range(0, o_ref.shape[1], N_CHUNK):
        o_ref[:, c:c + N_CHUNK] += jnp.dot(act, wd_ref[:, c:c + N_CHUNK], preferred_element_type=F32)

    @pl.when(f == pl.num_programs(1) - 1)
    def _():
        def store_out(rows, y):
            o_ref[rows, :] = h_ref[rows, :] + y

        _rms_norm_rows(o_ref, gpost_ref, nscale_ref, store_out, unroll=True)


def _ffn(h, g_pre, g_post, w_gate, w_up, w_down):
    n, d = h.shape
    d_ff = w_gate.shape[1]
    tm, tf = TM_FFN, TF_FFN
    return pl.pallas_call(
        _ffn_kernel,
        grid=(n // tm, d_ff // tf),
        in_specs=[
            pl.BlockSpec((tm, d), lambda i, f: (i, 0), pipeline_mode=pl.Buffered(1)),
            pl.BlockSpec((1, d), lambda i, f: (0, 0)),
            pl.BlockSpec((1, d), lambda i, f: (0, 0)),
            pl.BlockSpec((d, tf), lambda i, f: (0, f)),
            pl.BlockSpec((d, tf), lambda i, f: (0, f)),
            pl.BlockSpec((tf, d), lambda i, f: (f, 0)),
        ],
        out_specs=pl.BlockSpec((tm, d), lambda i, f: (i, 0)),
        out_shape=jax.ShapeDtypeStruct((n, d), F32),
        scratch_shapes=[pltpu.VMEM((tm, d), BF16), pltpu.VMEM((tm, 1), F32)],
        compiler_params=pltpu.CompilerParams(
            dimension_semantics=("arbitrary", "arbitrary"), vmem_limit_bytes=VMEM_LIMIT_BYTES),
        name="ffn",
    )(h, g_pre, g_post, w_gate, w_up, w_down)


def kernel(x, positions, pre_mix_norm, post_mix_norm, w_in, pool_w, pool_scale,
           lam_q1, lam_k1, lam_q2, lam_k2, subln_w, w_out,
           pre_ffn_norm, post_ffn_norm, w_gate, w_up, w_down):
    batch, seq, d_model = x.shape
    depth = w_in.shape[0]
    pool_width = pool_scale.shape[1]
    attn_width = w_out.shape[1] - pool_width
    qk_width = (w_in.shape[2] - pool_width - attn_width) // 2
    n_heads = attn_width // DIFF_V_DIM
    assert qk_width == n_heads * 2 * DIFF_HEAD_DIM
    assert seq % TQ == 0 and seq % TM_IN == 0 and (batch * seq) % TM_FFN == 0
    assert TM_IN == TK

    n = batch * seq
    h = x.reshape(n, d_model)
    pos2 = positions.reshape(n, 1)
    inv_freq = ROPE_THETA ** (-jnp.arange(0, ROT_DIM, 2, dtype=F32) / ROT_DIM)
    lane_dim = jnp.arange(LANES) % DIFF_HEAD_DIM
    inv_freq_lane = jnp.where(lane_dim < ROT_DIM, inv_freq[lane_dim % ROT_HALF], 0.0).reshape(1, LANES)

    for l in range(depth):
        lambda_init = _lambda_init(l)
        pool_out, q, k, vt = _in_proj(
            h, pos2, inv_freq_lane, pre_mix_norm[l].reshape(1, -1), w_in[l], pool_w[l],
            pool_scale[l].reshape(1, -1),
            seq=seq, pool_width=pool_width, qk_width=qk_width, attn_width=attn_width)
        lam_vecs = jnp.stack([lam_q1[l], lam_k1[l], lam_q2[l], lam_k2[l]]).astype(F32)
        attn_out = _attention(q, k, vt, lam_vecs, subln_w[l].reshape(1, -1),
                              batch=batch, seq=seq, n_heads=n_heads, lambda_init=lambda_init)
        h = _out_proj(pool_out, attn_out, h, w_out[l], post_mix_norm[l].reshape(1, -1))
        h = _ffn(h, pre_ffn_norm[l].reshape(1, -1), post_ffn_norm[l].reshape(1, -1),
                 w_gate[l].astype(BF16), w_up[l].astype(BF16), w_down[l].astype(BF16))
    return h.reshape(batch, seq, d_model)
```

```python
import functools
import math

import jax
import jax.numpy as jnp
from jax import lax
from jax.experimental import pallas as pl
from jax.experimental.pallas import tpu as pltpu

F32 = jnp.float32
BF16 = jnp.bfloat16

POOL_WINDOWS = (2, 4, 8, 16)
DIFF_HEAD_DIM = 64
DIFF_V_DIM = 2 * DIFF_HEAD_DIM
ROPE_THETA = 500000.0
ROT_DIM = DIFF_HEAD_DIM // 4
ROT_HALF = ROT_DIM // 2
NORM_EPS = 1e-6
NEG_INF = -1e30
LOG2_E = math.log2(math.e)

LANES = 128
POOL_HALO = 32
VMEM_LIMIT_BYTES = 56 * 1024 * 1024

TM_IN = 512
TQ = 1024
TK = 512
TM_OUT = 512
TM_FFN = 1024
TF_FFN = 512
N_CHUNK = 512
NORM_ROWS = 64
IN_CAST_STEPS = 8
OUT_CAST_STEPS = 4


def _lambda_init(layer_idx):
    return 0.8 - 0.6 * math.exp(-0.3 * layer_idx)


def _rms_norm(xf, g):
    ms = jnp.mean(xf * xf, axis=-1, keepdims=True)
    return xf * lax.rsqrt(ms + NORM_EPS) * g


def _rms_norm_rows(src_ref, g_ref, scale_ref, emit, unroll):
    n = src_ref.shape[0]

    def scale_body(c, carry):
        rows = pl.ds(pl.multiple_of(c * NORM_ROWS, NORM_ROWS), NORM_ROWS)
        x = src_ref[rows, :]
        scale_ref[rows, :] = lax.rsqrt(jnp.mean(x * x, axis=-1, keepdims=True) + NORM_EPS)
        return carry

    lax.fori_loop(0, n // NORM_ROWS, scale_body, 0, unroll=unroll)

    def apply_body(c, carry):
        rows = pl.ds(pl.multiple_of(c * NORM_ROWS, NORM_ROWS), NORM_ROWS)
        emit(rows, src_ref[rows, :] * scale_ref[rows, :] * g_ref[...])
        return carry

    lax.fori_loop(0, n // NORM_ROWS, apply_body, 0, unroll=unroll)


def _in_proj_kernel(x_ref, pos_ref, invf_ref, g_ref, w_ref, pw_ref, ps_ref,
                    pool_ref, q_ref, k_ref, vt_ref, w_bf_ref, wvt_ref, hn_ref, u_ref, carry_ref, nscale_ref,
                    *, seq, pool_width, qk_width):
    tm = x_ref.shape[0]
    step = pl.program_id(0)
    rows_per_cast = w_ref.shape[0]
    n_cast = w_bf_ref.shape[0] // rows_per_cast
    n_direct = w_bf_ref.shape[1]

    @pl.when(step < n_cast)
    def _():
        r0 = pl.multiple_of(step * rows_per_cast, rows_per_cast)
        w_bf_ref[pl.ds(r0, rows_per_cast), :] = w_ref[:, :n_direct].astype(BF16)

    for t in range(n_cast):
        @pl.when(step == t)
        def _():
            wvt_ref[:, t * rows_per_cast:(t + 1) * rows_per_cast] = w_ref[:, n_direct:].T.astype(BF16)

    @pl.when(step >= n_cast)
    def _():
        tiles_per_seq = seq // tm
        ti = (step - n_cast) % tiles_per_seq

        hn_ref[...] = _rms_norm(x_ref[...], g_ref[...]).astype(BF16)

        @pl.when(ti == 0)
        def _():
            carry_ref[...] = jnp.zeros_like(carry_ref)

        lane = lax.broadcasted_iota(jnp.int32, (1, LANES), 1) % DIFF_HEAD_DIM
        ang = pos_ref[...].astype(F32) * invf_ref[...]
        cos, sin = jnp.cos(ang), jnp.sin(ang)
        coef_self = jnp.where(lane < ROT_DIM, cos, 1.0)
        coef_lo = jnp.where((lane >= ROT_HALF) & (lane < ROT_DIM), sin, 0.0)
        coef_hi = jnp.where(lane < ROT_HALF, -sin, 0.0)

        def rope_chunk(out_ref, chunk, c, scale):
            t = jnp.dot(hn_ref[...], w_bf_ref[:, chunk * N_CHUNK:(chunk + 1) * N_CHUNK],
                        preferred_element_type=F32)
            for h in range(N_CHUNK // LANES):
                th = t[:, h * LANES:(h + 1) * LANES]
                r = (th * coef_self + pltpu.roll(th, ROT_HALF, 1) * coef_lo
                     + pltpu.roll(th, LANES - ROT_HALF, 1) * coef_hi)
                if scale != 1.0:
                    r = r * scale
                out_ref[:, c + h * LANES:c + (h + 1) * LANES] = r.astype(BF16)

        gdim = pool_width // len(POOL_WINDOWS)
        t_in_seq = ti * tm + lax.broadcasted_iota(jnp.int32, (tm, 1), 0)

        def pool_group(g, w):
            c0 = g * gdim
            u = u_ref[:, c0:c0 + gdim]
            ext = jnp.concatenate([carry_ref[g], u], axis=0)
            carry_ref[g] = u[tm - POOL_HALO:, :]
            lvl, off, k = ext, 0, 1
            while k < w:
                new_off = min(off + 8, POOL_HALO)
                cur = lvl[new_off - off:, :]
                shifted = lvl[new_off - off - k: lvl.shape[0] - k, :]
                lvl, off, k = cur + shifted, new_off, 2 * k
            wsum = lvl[POOL_HALO - off:, :]
            cnt = jnp.minimum(t_in_seq + 1, w).astype(F32)
            pooled = (wsum / cnt - u).astype(BF16)
            mixed = jnp.dot(pooled, pw_ref[g].astype(BF16), preferred_element_type=F32)
            pool_ref[:, c0:c0 + gdim] = (mixed * ps_ref[:, c0:c0 + gdim]).astype(BF16)

        n_pool, n_qk = pool_width // N_CHUNK, qk_width // N_CHUNK
        for ci in range(n_pool):
            u_ref[:, ci * N_CHUNK:(ci + 1) * N_CHUNK] = jnp.dot(
                hn_ref[...], w_bf_ref[:, ci * N_CHUNK:(ci + 1) * N_CHUNK], preferred_element_type=F32)

        q_scale = DIFF_HEAD_DIM ** -0.5 * LOG2_E
        chunks = ([(q_ref, n_pool + ci, ci * N_CHUNK, q_scale) for ci in range(n_qk)]
                  + [(k_ref, n_pool + n_qk + ci, ci * N_CHUNK, 1.0) for ci in range(n_qk)])
        groups = list(enumerate(POOL_WINDOWS))
        for idx, args in enumerate(chunks):
            rope_chunk(*args)
            for g, w in groups[idx * len(groups) // len(chunks):(idx + 1) * len(groups) // len(chunks)]:
                pool_group(g, w)

        for c in range(0, vt_ref.shape[0], N_CHUNK):
            vt_ref[c:c + N_CHUNK, :] = lax.dot_general(
                wvt_ref[c:c + N_CHUNK, :], hn_ref[...], (((1,), (1,)), ((), ())),
                preferred_element_type=F32).astype(BF16)


def _in_proj(x2, pos2, inv_freq, g, w_in, pool_w, pool_scale, *, seq, pool_width, qk_width, attn_width):
    n, d = x2.shape
    tm = TM_IN
    n_groups = len(POOL_WINDOWS)
    gdim = pool_width // n_groups
    n_direct = pool_width + 2 * qk_width
    n_cast = IN_CAST_STEPS
    const = lambda s: (0, 0)
    row = lambda s: (jnp.maximum(s - n_cast, 0), 0)
    kern = functools.partial(_in_proj_kernel, seq=seq, pool_width=pool_width, qk_width=qk_width)
    return pl.pallas_call(
        kern,
        grid=(n_cast + n // tm,),
        in_specs=[
            pl.BlockSpec((tm, d), row),
            pl.BlockSpec((tm, 1), row),
            pl.BlockSpec((1, LANES), const),
            pl.BlockSpec((1, d), const),
            pl.BlockSpec((d // n_cast, w_in.shape[1]), lambda s: (jnp.minimum(s, n_cast - 1), 0)),
            pl.BlockSpec(pool_w.shape, lambda s: (0, 0, 0)),
            pl.BlockSpec((1, pool_width), const),
        ],
        out_specs=[
            pl.BlockSpec((tm, pool_width), row),
            pl.BlockSpec((tm, qk_width), row),
            pl.BlockSpec((tm, qk_width), row),
            pl.BlockSpec((None, attn_width, tm), lambda s: (jnp.maximum(s - n_cast, 0), 0, 0)),
        ],
        out_shape=[
            jax.ShapeDtypeStruct((n, pool_width), BF16),
            jax.ShapeDtypeStruct((n, qk_width), BF16),
            jax.ShapeDtypeStruct((n, qk_width), BF16),
            jax.ShapeDtypeStruct((n // tm, attn_width, tm), BF16),
        ],
        scratch_shapes=[
            pltpu.VMEM((d, n_direct), BF16),
            pltpu.VMEM((attn_width, d), BF16),
            pltpu.VMEM((tm, d), BF16),
            pltpu.VMEM((tm, pool_width), F32),
            pltpu.VMEM((n_groups, POOL_HALO, gdim), F32),
            pltpu.VMEM((tm, 1), F32),
        ],
        compiler_params=pltpu.CompilerParams(
            dimension_semantics=("arbitrary",), vmem_limit_bytes=VMEM_LIMIT_BYTES),
        name="in_proj",
    )(x2, pos2, inv_freq, g, w_in, pool_w, pool_scale)


def _attn_kernel(lam_ref, sw_ref, q_ref, k_ref, vt_ref, wg_ref, wu_ref, wd_ref,
                 o_ref, wg_bf_ref, wu_bf_ref, wd_bf_ref,
                 qc_ref, s0_ref, s1_ref, mx0_ref, mx1_ref, m_ref, l_ref, acc_ref, *, lambda_init):
    wg_bf_ref[...] = wg_ref[...].astype(BF16)
    wu_bf_ref[...] = wu_ref[...].astype(BF16)
    wd_bf_ref[...] = wd_ref[...].astype(BF16)

    seq = q_ref.shape[0]
    n_tiles = seq // TQ
    s_slots, mx_slots = (s0_ref, s1_ref), (mx0_ref, mx1_ref)
    lane = lax.broadcasted_iota(jnp.int32, (TQ, DIFF_V_DIM), 1)
    hi_cols = slice(TK, TQ)
    all_cols = slice(0, TQ)

    def rows(i, size):
        return pl.ds(i * size if isinstance(i, int) else pl.multiple_of(i * size, size), size)

    def load_q(qi):
        q = q_ref[rows(qi, TQ), :]
        zero = jnp.zeros_like(q)
        qc_ref[0] = jnp.where(lane < DIFF_HEAD_DIM, q, zero)
        qc_ref[1] = jnp.where(lane >= DIFF_HEAD_DIM, q, zero)

    def scores(j, slot, c, cs, masked):
        width = cs.stop - cs.start
        s = lax.dot_general(k_ref[rows(j, TK), :], qc_ref[c, cs, :],
                            (((1,), (1,)), ((), ())), preferred_element_type=F32)
        if masked:
            kv = lax.broadcasted_iota(jnp.int32, (TK, width), 0)
            r = lax.broadcasted_iota(jnp.int32, (TK, width), 1)
            s = jnp.where(kv <= r, s, NEG_INF)
        s_slots[slot][c, :, 0:width] = s
        mx_slots[slot][c, :, 0:width] = jnp.max(s, axis=0, keepdims=True)

    def exp_pv(j, slot, c, cs):
        width = cs.stop - cs.start
        m_prev = m_ref[c, :, cs]
        m_new = jnp.maximum(m_prev, mx_slots[slot][c, :, 0:width])
        alpha = jnp.exp2(m_prev - m_new)
        p = jnp.exp2(s_slots[slot][c, :, 0:width] - m_new)
        l_ref[c, :, cs] = alpha * l_ref[c, :, cs] + jnp.sum(p, axis=0, keepdims=True)
        pv = jnp.dot(vt_ref[j], p.astype(BF16), preferred_element_type=F32)
        acc_ref[c, :, cs] = alpha * acc_ref[c, :, cs] + pv
        m_ref[c, :, cs] = m_new

    def stage(j, slot, next_cols=all_cols, next_masked=False, cur_cols=all_cols):
        for c in range(2):
            scores(j + 1, 1 - slot, c, next_cols, next_masked)
            exp_pv(j, slot, c, cur_cols)

    def tile(qi, carry):
        m_ref[...] = jnp.full_like(m_ref, NEG_INF)
        l_ref[...] = jnp.zeros_like(l_ref)
        acc_ref[...] = jnp.zeros_like(acc_ref)

        def pair(t, c2):
            stage(2 * t, 0)
            stage(2 * t + 1, 1)
            return c2

        lax.fori_loop(0, qi - 1, pair, 0)

        @pl.when(qi > 0)
        def _():
            stage(2 * qi - 2, 0)
            stage(2 * qi - 1, 1, next_masked=True)

        stage(2 * qi, 0, next_cols=hi_cols, next_masked=True)
        nxt = jnp.minimum(qi + 1, n_tiles - 1)
        load_q(nxt)
        for c in range(2):
            scores(0, 0, c, all_cols, False)
            exp_pv(2 * qi + 1, 1, c, hi_cols)

        lam_v = lam_ref[...]
        lam = (jnp.exp(jnp.sum(lam_v[0:1] * lam_v[1:2], keepdims=True))
               - jnp.exp(jnp.sum(lam_v[2:3] * lam_v[3:4], keepdims=True)) + lambda_init)
        attn_t = acc_ref[0] / l_ref[0] - lam * (acc_ref[1] / l_ref[1])
        ms = jnp.mean(attn_t * attn_t, axis=0, keepdims=True)
        y = (attn_t * lax.rsqrt(ms + NORM_EPS)).T
        o_ref[rows(qi, TQ), :] = (y * sw_ref[...] * (1.0 - lambda_init)).astype(BF16)
        return carry

    load_q(0)
    for c in range(2):
        scores(0, 0, c, all_cols, True)
    lax.fori_loop(0, n_tiles, tile, 0)


def _attention(q, k, vt, lam_vecs, subln_w, ffn_weights, *, batch, seq, n_heads, lambda_init):
    n, width = q.shape
    nk = seq // TK
    n_steps = batch * n_heads
    assert TQ == 2 * TK and seq % TQ == 0 and vt.shape == (n // TK, n_heads * DIFF_V_DIM, TK)
    kern = functools.partial(_attn_kernel, lambda_init=lambda_init)
    head = lambda b, h: (b, h)
    step_rows = lambda b, h: (b * n_heads + h, 0)
    w_specs = [pl.BlockSpec((w.shape[0] // n_steps, w.shape[1]), step_rows) for w in ffn_weights]
    outs = pl.pallas_call(
        kern,
        grid=(batch, n_heads),
        in_specs=[
            pl.BlockSpec(lam_vecs.shape, lambda b, h: (0, 0)),
            pl.BlockSpec((1, DIFF_V_DIM), lambda b, h: (0, 0)),
            pl.BlockSpec((seq, DIFF_V_DIM), head),
            pl.BlockSpec((seq, DIFF_V_DIM), head),
            pl.BlockSpec((nk, DIFF_V_DIM, TK), lambda b, h: (b, h, 0)),
        ] + w_specs,
        out_specs=[pl.BlockSpec((seq, DIFF_V_DIM), head)] + w_specs,
        out_shape=[jax.ShapeDtypeStruct((n, width), BF16)]
        + [jax.ShapeDtypeStruct(w.shape, BF16) for w in ffn_weights],
        scratch_shapes=[
            pltpu.VMEM((2, TQ, DIFF_V_DIM), BF16),
            pltpu.VMEM((2, TK, TQ), F32),
            pltpu.VMEM((2, TK, TQ), F32),
            pltpu.VMEM((2, 1, TQ), F32),
            pltpu.VMEM((2, 1, TQ), F32),
            pltpu.VMEM((2, 1, TQ), F32),
            pltpu.VMEM((2, 1, TQ), F32),
            pltpu.VMEM((2, DIFF_V_DIM, TQ), F32),
        ],
        compiler_params=pltpu.CompilerParams(
            dimension_semantics=("arbitrary", "arbitrary"), vmem_limit_bytes=VMEM_LIMIT_BYTES),
        name="diff_attention",
    )(lam_vecs, subln_w, q, k, vt, *ffn_weights)
    return outs[0], outs[1:]


def _out_proj_kernel(pool_ref, attn_ref, x_ref, w_ref, g_ref, h_ref, w_bf_ref, mix_ref, nscale_ref):
    step = pl.program_id(0)
    rows_per_cast = w_ref.shape[0]
    n_cast = w_bf_ref.shape[0] // rows_per_cast
    pw = pool_ref.shape[1]

    @pl.when(step < n_cast)
    def _():
        r0 = pl.multiple_of(step * rows_per_cast, rows_per_cast)
        w_bf_ref[pl.ds(r0, rows_per_cast), :] = w_ref[...].astype(BF16)

    @pl.when(step >= n_cast)
    def _():
        for c in range(0, mix_ref.shape[1], N_CHUNK):
            mix_ref[:, c:c + N_CHUNK] = (
                jnp.dot(pool_ref[...], w_bf_ref[:pw, c:c + N_CHUNK], preferred_element_type=F32)
                + jnp.dot(attn_ref[...], w_bf_ref[pw:, c:c + N_CHUNK], preferred_element_type=F32))

        def store_h(rows, y):
            h_ref[rows, :] = x_ref[rows, :] + y

        _rms_norm_rows(mix_ref, g_ref, nscale_ref, store_h, unroll=True)


def _out_proj(pool_out, attn_out, x2, w_out, g):
    n, d = x2.shape
    tm = TM_OUT
    n_cast = OUT_CAST_STEPS
    row = lambda s: (jnp.maximum(s - n_cast, 0), 0)
    const = lambda s: (0, 0)
    return pl.pallas_call(
        _out_proj_kernel,
        grid=(n_cast + n // tm,),
        in_specs=[
            pl.BlockSpec((tm, pool_out.shape[1]), row),
            pl.BlockSpec((tm, attn_out.shape[1]), row),
            pl.BlockSpec((tm, d), row),
            pl.BlockSpec((w_out.shape[0] // n_cast, w_out.shape[1]), lambda s: (jnp.minimum(s, n_cast - 1), 0)),
            pl.BlockSpec((1, d), const),
        ],
        out_specs=pl.BlockSpec((tm, d), row),
        out_shape=jax.ShapeDtypeStruct((n, d), F32),
        scratch_shapes=[
            pltpu.VMEM(w_out.shape, BF16),
            pltpu.VMEM((tm, d), F32),
            pltpu.VMEM((tm, 1), F32),
        ],
        compiler_params=pltpu.CompilerParams(
            dimension_semantics=("arbitrary",), vmem_limit_bytes=VMEM_LIMIT_BYTES),
        name="out_proj",
    )(pool_out, attn_out, x2, w_out, g)


def _ffn_kernel(h_ref, gpre_ref, gpost_ref, wg_ref, wu_ref, wd_ref, o_ref, hn_ref, nscale_ref):
    f = pl.program_id(1)

    @pl.when(f == 0)
    def _():
        def store_hn(rows, y):
            hn_ref[rows, :] = y.astype(BF16)
            o_ref[rows, :] = jnp.zeros_like(y)

        _rms_norm_rows(h_ref, gpre_ref, nscale_ref, store_hn, unroll=True)

    hn = hn_ref[...]
    gate = jnp.dot(hn, wg_ref[...], preferred_element_type=F32)
    up = jnp.dot(hn, wu_ref[...], preferred_element_type=F32)
    act = (gate * jax.nn.sigmoid(gate) * up).astype(BF16)
    for c in range(0, o_ref.shape[1], N_CHUNK):
        o_ref[:, c:c + N_CHUNK] += jnp.dot(act, wd_ref[:, c:c + N_CHUNK], preferred_element_type=F32)

    @pl.when(f == pl.num_programs(1) - 1)
    def _():
        def store_out(rows, y):
            o_ref[rows, :] = h_ref[rows, :] + y

        _rms_norm_rows(o_ref, gpost_ref, nscale_ref, store_out, unroll=True)


def _ffn(h, g_pre, g_post, w_gate, w_up, w_down):
    n, d = h.shape
    d_ff = w_gate.shape[1]
    tm, tf = TM_FFN, TF_FFN
    return pl.pallas_call(
        _ffn_kernel,
        grid=(n // tm, d_ff // tf),
        in_specs=[
            pl.BlockSpec((tm, d), lambda i, f: (i, 0), pipeline_mode=pl.Buffered(1)),
            pl.BlockSpec((1, d), lambda i, f: (0, 0)),
            pl.BlockSpec((1, d), lambda i, f: (0, 0)),
            pl.BlockSpec((d, tf), lambda i, f: (0, f)),
            pl.BlockSpec((d, tf), lambda i, f: (0, f)),
            pl.BlockSpec((tf, d), lambda i, f: (f, 0)),
        ],
        out_specs=pl.BlockSpec((tm, d), lambda i, f: (i, 0)),
        out_shape=jax.ShapeDtypeStruct((n, d), F32),
        scratch_shapes=[pltpu.VMEM((tm, d), BF16), pltpu.VMEM((tm, 1), F32)],
        compiler_params=pltpu.CompilerParams(
            dimension_semantics=("arbitrary", "arbitrary"), vmem_limit_bytes=VMEM_LIMIT_BYTES),
        name="ffn",
    )(h, g_pre, g_post, w_gate, w_up, w_down)


def kernel(x, positions, pre_mix_norm, post_mix_norm, w_in, pool_w, pool_scale,
           lam_q1, lam_k1, lam_q2, lam_k2, subln_w, w_out,
           pre_ffn_norm, post_ffn_norm, w_gate, w_up, w_down):
    batch, seq, d_model = x.shape
    depth = w_in.shape[0]
    pool_width = pool_scale.shape[1]
    attn_width = w_out.shape[1] - pool_width
    qk_width = (w_in.shape[2] - pool_width - attn_width) // 2
    n_heads = attn_width // DIFF_V_DIM
    assert qk_width == n_heads * 2 * DIFF_HEAD_DIM
    assert seq % TQ == 0 and seq % TM_IN == 0 and (batch * seq) % TM_FFN == 0
    assert TM_IN == TK

    n = batch * seq
    h = x.reshape(n, d_model)
    pos2 = positions.reshape(n, 1)
    inv_freq = ROPE_THETA ** (-jnp.arange(0, ROT_DIM, 2, dtype=F32) / ROT_DIM)
    lane_dim = jnp.arange(LANES) % DIFF_HEAD_DIM
    inv_freq_lane = jnp.where(lane_dim < ROT_DIM, inv_freq[lane_dim % ROT_HALF], 0.0).reshape(1, LANES)

    for l in range(depth):
        lambda_init = _lambda_init(l)
        pool_out, q, k, vt = _in_proj(
            h, pos2, inv_freq_lane, pre_mix_norm[l].reshape(1, -1), w_in[l], pool_w[l],
            pool_scale[l].reshape(1, -1),
            seq=seq, pool_width=pool_width, qk_width=qk_width, attn_width=attn_width)
        lam_vecs = jnp.stack([lam_q1[l], lam_k1[l], lam_q2[l], lam_k2[l]]).astype(F32)
        attn_out, (wg_bf, wu_bf, wd_bf) = _attention(
            q, k, vt, lam_vecs, subln_w[l].reshape(1, -1), (w_gate[l], w_up[l], w_down[l]),
            batch=batch, seq=seq, n_heads=n_heads, lambda_init=lambda_init)
        h = _out_proj(pool_out, attn_out, h, w_out[l], post_mix_norm[l].reshape(1, -1))
        h = _ffn(h, pre_ffn_norm[l].reshape(1, -1), post_ffn_norm[l].reshape(1, -1), wg_bf, wu_bf, wd_bf)
    return h.reshape(batch, seq, d_model)
```

```python
import functools
import math

import jax
import jax.numpy as jnp
from jax import lax
from jax.experimental import pallas as pl
from jax.experimental.pallas import tpu as pltpu

F32 = jnp.float32
BF16 = jnp.bfloat16

POOL_WINDOWS = (2, 4, 8, 16)
DIFF_HEAD_DIM = 64
DIFF_V_DIM = 2 * DIFF_HEAD_DIM
ROPE_THETA = 500000.0
ROT_DIM = DIFF_HEAD_DIM // 4
ROT_HALF = ROT_DIM // 2
NORM_EPS = 1e-6
NEG_INF = -1e30
LOG2_E = math.log2(math.e)

LANES = 128
POOL_HALO = 32
VMEM_LIMIT_BYTES = 56 * 1024 * 1024

TM_IN = 512
TQ = 1024
TK = 512
ATTN_CHUNK = 256
ATTN_UNROLL = 4
TM_OUT = 512
TM_FFN = 1024
TF_FFN = 512
N_CHUNK = 512
NORM_ROWS = 64
IN_CAST_STEPS = 8
OUT_CAST_STEPS = 4


def _lambda_init(layer_idx):
    return 0.8 - 0.6 * math.exp(-0.3 * layer_idx)


def _rms_norm(xf, g):
    ms = jnp.mean(xf * xf, axis=-1, keepdims=True)
    return xf * lax.rsqrt(ms + NORM_EPS) * g


def _rms_norm_rows(src_ref, g_ref, scale_ref, emit, unroll):
    n = src_ref.shape[0]

    def scale_body(c, carry):
        rows = pl.ds(pl.multiple_of(c * NORM_ROWS, NORM_ROWS), NORM_ROWS)
        x = src_ref[rows, :]
        scale_ref[rows, :] = lax.rsqrt(jnp.mean(x * x, axis=-1, keepdims=True) + NORM_EPS)
        return carry

    lax.fori_loop(0, n // NORM_ROWS, scale_body, 0, unroll=unroll)

    def apply_body(c, carry):
        rows = pl.ds(pl.multiple_of(c * NORM_ROWS, NORM_ROWS), NORM_ROWS)
        emit(rows, src_ref[rows, :] * scale_ref[rows, :] * g_ref[...])
        return carry

    lax.fori_loop(0, n // NORM_ROWS, apply_body, 0, unroll=unroll)


def _in_proj_kernel(x_ref, pos_ref, invf_ref, g_ref, w_ref, pw_ref, ps_ref,
                    pool_ref, q_ref, k_ref, vt_ref, w_bf_ref, wvt_ref, hn_ref, u_ref, carry_ref, nscale_ref,
                    *, seq, pool_width, qk_width):
    tm = x_ref.shape[0]
    step = pl.program_id(0)
    rows_per_cast = w_ref.shape[0]
    n_cast = w_bf_ref.shape[0] // rows_per_cast
    n_direct = w_bf_ref.shape[1]

    @pl.when(step < n_cast)
    def _():
        r0 = pl.multiple_of(step * rows_per_cast, rows_per_cast)
        w_bf_ref[pl.ds(r0, rows_per_cast), :] = w_ref[:, :n_direct].astype(BF16)

    for t in range(n_cast):
        @pl.when(step == t)
        def _():
            wvt_ref[:, t * rows_per_cast:(t + 1) * rows_per_cast] = w_ref[:, n_direct:].T.astype(BF16)

    @pl.when(step >= n_cast)
    def _():
        tiles_per_seq = seq // tm
        ti = (step - n_cast) % tiles_per_seq

        hn_ref[...] = _rms_norm(x_ref[...], g_ref[...]).astype(BF16)

        @pl.when(ti == 0)
        def _():
            carry_ref[...] = jnp.zeros_like(carry_ref)

        lane = lax.broadcasted_iota(jnp.int32, (1, LANES), 1) % DIFF_HEAD_DIM
        ang = pos_ref[...].astype(F32) * invf_ref[...]
        cos, sin = jnp.cos(ang), jnp.sin(ang)
        coef_self = jnp.where(lane < ROT_DIM, cos, 1.0)
        coef_lo = jnp.where((lane >= ROT_HALF) & (lane < ROT_DIM), sin, 0.0)
        coef_hi = jnp.where(lane < ROT_HALF, -sin, 0.0)

        def rope_chunk(out_ref, chunk, c, scale):
            t = jnp.dot(hn_ref[...], w_bf_ref[:, chunk * N_CHUNK:(chunk + 1) * N_CHUNK],
                        preferred_element_type=F32)
            for h in range(N_CHUNK // LANES):
                th = t[:, h * LANES:(h + 1) * LANES]
                r = (th * coef_self + pltpu.roll(th, ROT_HALF, 1) * coef_lo
                     + pltpu.roll(th, LANES - ROT_HALF, 1) * coef_hi)
                if scale != 1.0:
                    r = r * scale
                out_ref[:, c + h * LANES:c + (h + 1) * LANES] = r.astype(BF16)

        gdim = pool_width // len(POOL_WINDOWS)
        t_in_seq = ti * tm + lax.broadcasted_iota(jnp.int32, (tm, 1), 0)

        def pool_group(g, w):
            c0 = g * gdim
            u = u_ref[:, c0:c0 + gdim]
            ext = jnp.concatenate([carry_ref[g], u], axis=0)
            carry_ref[g] = u[tm - POOL_HALO:, :]
            lvl, off, k = ext, 0, 1
            while k < w:
                new_off = min(off + 8, POOL_HALO)
                cur = lvl[new_off - off:, :]
                shifted = lvl[new_off - off - k: lvl.shape[0] - k, :]
                lvl, off, k = cur + shifted, new_off, 2 * k
            wsum = lvl[POOL_HALO - off:, :]
            cnt = jnp.minimum(t_in_seq + 1, w).astype(F32)
            pooled = (wsum / cnt - u).astype(BF16)
            mixed = jnp.dot(pooled, pw_ref[g].astype(BF16), preferred_element_type=F32)
            pool_ref[:, c0:c0 + gdim] = (mixed * ps_ref[:, c0:c0 + gdim]).astype(BF16)

        n_pool, n_qk = pool_width // N_CHUNK, qk_width // N_CHUNK
        for ci in range(n_pool):
            u_ref[:, ci * N_CHUNK:(ci + 1) * N_CHUNK] = jnp.dot(
                hn_ref[...], w_bf_ref[:, ci * N_CHUNK:(ci + 1) * N_CHUNK], preferred_element_type=F32)

        q_scale = DIFF_HEAD_DIM ** -0.5 * LOG2_E
        chunks = ([(q_ref, n_pool + ci, ci * N_CHUNK, q_scale) for ci in range(n_qk)]
                  + [(k_ref, n_pool + n_qk + ci, ci * N_CHUNK, 1.0) for ci in range(n_qk)])
        groups = list(enumerate(POOL_WINDOWS))
        for idx, args in enumerate(chunks):
            rope_chunk(*args)
            for g, w in groups[idx * len(groups) // len(chunks):(idx + 1) * len(groups) // len(chunks)]:
                pool_group(g, w)

        for c in range(0, vt_ref.shape[0], N_CHUNK):
            vt_ref[c:c + N_CHUNK, :] = lax.dot_general(
                wvt_ref[c:c + N_CHUNK, :], hn_ref[...], (((1,), (1,)), ((), ())),
                preferred_element_type=F32).astype(BF16)


def _in_proj(x2, pos2, inv_freq, g, w_in, pool_w, pool_scale, *, seq, pool_width, qk_width, attn_width):
    n, d = x2.shape
    tm = TM_IN
    n_groups = len(POOL_WINDOWS)
    gdim = pool_width // n_groups
    n_direct = pool_width + 2 * qk_width
    n_cast = IN_CAST_STEPS
    const = lambda s: (0, 0)
    row = lambda s: (jnp.maximum(s - n_cast, 0), 0)
    kern = functools.partial(_in_proj_kernel, seq=seq, pool_width=pool_width, qk_width=qk_width)
    return pl.pallas_call(
        kern,
        grid=(n_cast + n // tm,),
        in_specs=[
            pl.BlockSpec((tm, d), row),
            pl.BlockSpec((tm, 1), row),
            pl.BlockSpec((1, LANES), const),
            pl.BlockSpec((1, d), const),
            pl.BlockSpec((d // n_cast, w_in.shape[1]), lambda s: (jnp.minimum(s, n_cast - 1), 0)),
            pl.BlockSpec(pool_w.shape, lambda s: (0, 0, 0)),
            pl.BlockSpec((1, pool_width), const),
        ],
        out_specs=[
            pl.BlockSpec((tm, pool_width), row),
            pl.BlockSpec((tm, qk_width), row),
            pl.BlockSpec((tm, qk_width), row),
            pl.BlockSpec((None, attn_width, tm), lambda s: (jnp.maximum(s - n_cast, 0), 0, 0)),
        ],
        out_shape=[
            jax.ShapeDtypeStruct((n, pool_width), BF16),
            jax.ShapeDtypeStruct((n, qk_width), BF16),
            jax.ShapeDtypeStruct((n, qk_width), BF16),
            jax.ShapeDtypeStruct((n // tm, attn_width, tm), BF16),
        ],
        scratch_shapes=[
            pltpu.VMEM((d, n_direct), BF16),
            pltpu.VMEM((attn_width, d), BF16),
            pltpu.VMEM((tm, d), BF16),
            pltpu.VMEM((tm, pool_width), F32),
            pltpu.VMEM((n_groups, POOL_HALO, gdim), F32),
            pltpu.VMEM((tm, 1), F32),
        ],
        compiler_params=pltpu.CompilerParams(
            dimension_semantics=("arbitrary",), vmem_limit_bytes=VMEM_LIMIT_BYTES),
        name="in_proj",
    )(x2, pos2, inv_freq, g, w_in, pool_w, pool_scale)


def _attn_kernel(lam_ref, sw_ref, q_ref, k_ref, vt_ref, wg_ref, wu_ref, wd_ref,
                 o_ref, wg_bf_ref, wu_bf_ref, wd_bf_ref,
                 qc_ref, s0_ref, s1_ref, mx0_ref, mx1_ref, m_ref, l_ref, acc_ref, *, lambda_init):
    wg_bf_ref[...] = wg_ref[...].astype(BF16)
    wu_bf_ref[...] = wu_ref[...].astype(BF16)
    wd_bf_ref[...] = wd_ref[...].astype(BF16)

    seq = q_ref.shape[0]
    n_tiles = seq // TQ
    n_unmasked = n_tiles * (n_tiles - 1)
    assert n_tiles >= 2 and n_unmasked % ATTN_UNROLL == 0 and ATTN_UNROLL % 2 == 0
    s_slots, mx_slots = (s0_ref, s1_ref), (mx0_ref, mx1_ref)
    chunks = [slice(a, a + ATTN_CHUNK) for a in range(0, TQ, ATTN_CHUNK)]
    chunk_plan = {
        "full": [(ch, None) for ch in chunks],
        "lo": [(ch, ch.start if ch.start < TK else None) for ch in chunks],
        "hi": [(ch, ch.start - TK) for ch in chunks if ch.start >= TK],
    }

    def rows(i, size):
        return pl.ds(i * size if isinstance(i, int) else pl.multiple_of(i * size, size), size)

    def scores_chunk(qi, j, slot, c, ch, offset):
        s = lax.dot_general(k_ref[rows(j, TK), :], qc_ref[qi, c, ch, :], (((1,), (1,)), ((), ())),
                            preferred_element_type=F32)
        if offset is not None:
            kv = lax.broadcasted_iota(jnp.int32, s.shape, 0)
            r = lax.broadcasted_iota(jnp.int32, s.shape, 1)
            s = jnp.where(kv <= r + offset, s, NEG_INF)
        s_slots[slot][c, :, ch] = s
        mx_slots[slot][c, :, ch] = jnp.max(s, axis=0, keepdims=True)

    def exp_pv_chunk(qi, j, slot, c, ch):
        m_prev = m_ref[qi, c, :, ch]
        m_new = jnp.maximum(m_prev, mx_slots[slot][c, :, ch])
        alpha = jnp.exp2(m_prev - m_new)
        p = jnp.exp2(s_slots[slot][c, :, ch] - m_new)
        l_ref[qi, c, :, ch] = alpha * l_ref[qi, c, :, ch] + jnp.sum(p, axis=0, keepdims=True)
        pv = jnp.dot(vt_ref[j], p.astype(BF16), preferred_element_type=F32)
        acc_ref[qi, c, :, ch] = alpha * acc_ref[qi, c, :, ch] + pv
        m_ref[qi, c, :, ch] = m_new

    def stage(cur, cur_kind, slot, nxt, nxt_kind):
        todo_next = chunk_plan[nxt_kind] if nxt is not None else []
        todo_cur = [ch for ch, _ in chunk_plan[cur_kind]]
        for c in range(2):
            for i in range(max(len(todo_next), len(todo_cur))):
                if i < len(todo_next):
                    scores_chunk(nxt[0], nxt[1], 1 - slot, c, *todo_next[i])
                if i < len(todo_cur):
                    exp_pv_chunk(cur[0], cur[1], slot, c, todo_cur[i])

    def finalize(qi):
        lam_v = lam_ref[...]
        lam = (jnp.exp(jnp.sum(lam_v[0:1] * lam_v[1:2], keepdims=True))
               - jnp.exp(jnp.sum(lam_v[2:3] * lam_v[3:4], keepdims=True)) + lambda_init)
        attn_t = acc_ref[qi, 0] / l_ref[qi, 0] - lam * (acc_ref[qi, 1] / l_ref[qi, 1])
        ms = jnp.mean(attn_t * attn_t, axis=0, keepdims=True)
        y = (attn_t * lax.rsqrt(ms + NORM_EPS)).T
        o_ref[rows(qi, TQ), :] = (y * sw_ref[...] * (1.0 - lambda_init)).astype(BF16)

    lane = lax.broadcasted_iota(jnp.int32, (TQ, DIFF_V_DIM), 1)
    for qi in range(n_tiles):
        q = q_ref[qi * TQ:(qi + 1) * TQ, :]
        zero = jnp.zeros_like(q)
        qc_ref[qi, 0] = jnp.where(lane < DIFF_HEAD_DIM, q, zero)
        qc_ref[qi, 1] = jnp.where(lane >= DIFF_HEAD_DIM, q, zero)
    m_ref[...] = jnp.full_like(m_ref, NEG_INF)
    l_ref[...] = jnp.zeros_like(l_ref)
    acc_ref[...] = jnp.zeros_like(acc_ref)

    def unmasked_pair(n):
        qi = 1
        for t in range(2, n_tiles):
            qi = qi + (n >= t * (t - 1)).astype(jnp.int32)
        return qi, n - qi * (qi - 1)

    for c in range(2):
        for ch, off in chunk_plan["full"]:
            scores_chunk(1, 0, 0, c, ch, off)

    def unmasked_body(it, carry):
        for u in range(ATTN_UNROLL):
            n = it * ATTN_UNROLL + u
            stage(unmasked_pair(n), "full", u % 2, unmasked_pair(jnp.minimum(n + 1, n_unmasked - 1)), "full")
        return carry

    lax.fori_loop(0, n_unmasked // ATTN_UNROLL, unmasked_body, 0)

    def masked_stages(qi, nxt_qi):
        stage((qi, 2 * qi), "lo", 0, (qi, 2 * qi + 1), "hi")
        stage((qi, 2 * qi + 1), "hi", 1, (nxt_qi, 2 * nxt_qi), "lo")

    for c in range(2):
        for ch, off in chunk_plan["lo"]:
            scores_chunk(0, 0, 0, c, ch, off)
    masked_stages(0, 1)

    def masked_body(qi, carry):
        finalize(qi - 1)
        masked_stages(qi, jnp.minimum(qi + 1, n_tiles - 1))
        return carry

    lax.fori_loop(1, n_tiles, masked_body, 0)
    finalize(n_tiles - 1)


def _attention(q, k, vt, lam_vecs, subln_w, ffn_weights, *, batch, seq, n_heads, lambda_init):
    n, width = q.shape
    nk = seq // TK
    n_tiles = seq // TQ
    n_steps = batch * n_heads
    assert TQ == 2 * TK and seq % TQ == 0 and vt.shape == (n // TK, n_heads * DIFF_V_DIM, TK)
    kern = functools.partial(_attn_kernel, lambda_init=lambda_init)
    head = lambda b, h: (b, h)
    step_rows = lambda b, h: (b * n_heads + h, 0)
    w_specs = [pl.BlockSpec((w.shape[0] // n_steps, w.shape[1]), step_rows) for w in ffn_weights]
    outs = pl.pallas_call(
        kern,
        grid=(batch, n_heads),
        in_specs=[
            pl.BlockSpec(lam_vecs.shape, lambda b, h: (0, 0)),
            pl.BlockSpec((1, DIFF_V_DIM), lambda b, h: (0, 0)),
            pl.BlockSpec((seq, DIFF_V_DIM), head),
            pl.BlockSpec((seq, DIFF_V_DIM), head),
            pl.BlockSpec((nk, DIFF_V_DIM, TK), lambda b, h: (b, h, 0)),
        ] + w_specs,
        out_specs=[pl.BlockSpec((seq, DIFF_V_DIM), head)] + w_specs,
        out_shape=[jax.ShapeDtypeStruct((n, width), BF16)]
        + [jax.ShapeDtypeStruct(w.shape, BF16) for w in ffn_weights],
        scratch_shapes=[
            pltpu.VMEM((n_tiles, 2, TQ, DIFF_V_DIM), BF16),
            pltpu.VMEM((2, TK, TQ), F32),
            pltpu.VMEM((2, TK, TQ), F32),
            pltpu.VMEM((2, 1, TQ), F32),
            pltpu.VMEM((2, 1, TQ), F32),
            pltpu.VMEM((n_tiles, 2, 1, TQ), F32),
            pltpu.VMEM((n_tiles, 2, 1, TQ), F32),
            pltpu.VMEM((n_tiles, 2, DIFF_V_DIM, TQ), F32),
        ],
        compiler_params=pltpu.CompilerParams(
            dimension_semantics=("arbitrary", "arbitrary"), vmem_limit_bytes=VMEM_LIMIT_BYTES),
        name="diff_attention",
    )(lam_vecs, subln_w, q, k, vt, *ffn_weights)
    return outs[0], outs[1:]


def _out_proj_kernel(pool_ref, attn_ref, x_ref, w_ref, g_ref, h_ref, w_bf_ref, mix_ref, nscale_ref):
    step = pl.program_id(0)
    rows_per_cast = w_ref.shape[0]
    n_cast = w_bf_ref.shape[0] // rows_per_cast
    pw = pool_ref.shape[1]

    @pl.when(step < n_cast)
    def _():
        r0 = pl.multiple_of(step * rows_per_cast, rows_per_cast)
        w_bf_ref[pl.ds(r0, rows_per_cast), :] = w_ref[...].astype(BF16)

    @pl.when(step >= n_cast)
    def _():
        for c in range(0, mix_ref.shape[1], N_CHUNK):
            mix_ref[:, c:c + N_CHUNK] = (
                jnp.dot(pool_ref[...], w_bf_ref[:pw, c:c + N_CHUNK], preferred_element_type=F32)
                + jnp.dot(attn_ref[...], w_bf_ref[pw:, c:c + N_CHUNK], preferred_element_type=F32))

        def store_h(rows, y):
            h_ref[rows, :] = x_ref[rows, :] + y

        _rms_norm_rows(mix_ref, g_ref, nscale_ref, store_h, unroll=True)


def _out_proj(pool_out, attn_out, x2, w_out, g):
    n, d = x2.shape
    tm = TM_OUT
    n_cast = OUT_CAST_STEPS
    row = lambda s: (jnp.maximum(s - n_cast, 0), 0)
    const = lambda s: (0, 0)
    return pl.pallas_call(
        _out_proj_kernel,
        grid=(n_cast + n // tm,),
        in_specs=[
            pl.BlockSpec((tm, pool_out.shape[1]), row),
            pl.BlockSpec((tm, attn_out.shape[1]), row),
            pl.BlockSpec((tm, d), row),
            pl.BlockSpec((w_out.shape[0] // n_cast, w_out.shape[1]), lambda s: (jnp.minimum(s, n_cast - 1), 0)),
            pl.BlockSpec((1, d), const),
        ],
        out_specs=pl.BlockSpec((tm, d), row),
        out_shape=jax.ShapeDtypeStruct((n, d), F32),
        scratch_shapes=[
            pltpu.VMEM(w_out.shape, BF16),
            pltpu.VMEM((tm, d), F32),
            pltpu.VMEM((tm, 1), F32),
        ],
        compiler_params=pltpu.CompilerParams(
            dimension_semantics=("arbitrary",), vmem_limit_bytes=VMEM_LIMIT_BYTES),
        name="out_proj",
    )(pool_out, attn_out, x2, w_out, g)


def _ffn_kernel(h_ref, gpre_ref, gpost_ref, wg_ref, wu_ref, wd_ref, o_ref, hn_ref, nscale_ref):
    f = pl.program_id(1)

    @pl.when(f == 0)
    def _():
        def store_hn(rows, y):
            hn_ref[rows, :] = y.astype(BF16)
            o_ref[rows, :] = jnp.zeros_like(y)

        _rms_norm_rows(h_ref, gpre_ref, nscale_ref, store_hn, unroll=True)

    hn = hn_ref[...]
    gate = jnp.dot(hn, wg_ref[...], preferred_element_type=F32)
    up = jnp.dot(hn, wu_ref[...], preferred_element_type=F32)
    act = (gate * jax.nn.sigmoid(gate) * up).astype(BF16)
    for c in range(0, o_ref.shape[1], N_CHUNK):
        o_ref[:, c:c + N_CHUNK] += jnp.dot(act, wd_ref[:, c:c + N_CHUNK], preferred_element_type=F32)

    @pl.when(f == pl.num_programs(1) - 1)
    def _():
        def store_out(rows, y):
            o_ref[rows, :] = h_ref[rows, :] + y

        _rms_norm_rows(o_ref, gpost_ref, nscale_ref, store_out, unroll=True)


def _ffn(h, g_pre, g_post, w_gate, w_up, w_down):
    n, d = h.shape
    d_ff = w_gate.shape[1]
    tm, tf = TM_FFN, TF_FFN
    return pl.pallas_call(
        _ffn_kernel,
        grid=(n // tm, d_ff // tf),
        in_specs=[
            pl.BlockSpec((tm, d), lambda i, f: (i, 0), pipeline_mode=pl.Buffered(1)),
            pl.BlockSpec((1, d), lambda i, f: (0, 0)),
            pl.BlockSpec((1, d), lambda i, f: (0, 0)),
            pl.BlockSpec((d, tf), lambda i, f: (0, f)),
            pl.BlockSpec((d, tf), lambda i, f: (0, f)),
            pl.BlockSpec((tf, d), lambda i, f: (f, 0)),
        ],
        out_specs=pl.BlockSpec((tm, d), lambda i, f: (i, 0)),
        out_shape=jax.ShapeDtypeStruct((n, d), F32),
        scratch_shapes=[pltpu.VMEM((tm, d), BF16), pltpu.VMEM((tm, 1), F32)],
        compiler_params=pltpu.CompilerParams(
            dimension_semantics=("arbitrary", "arbitrary"), vmem_limit_bytes=VMEM_LIMIT_BYTES),
        name="ffn",
    )(h, g_pre, g_post, w_gate, w_up, w_down)


def kernel(x, positions, pre_mix_norm, post_mix_norm, w_in, pool_w, pool_scale,
           lam_q1, lam_k1, lam_q2, lam_k2, subln_w, w_out,
           pre_ffn_norm, post_ffn_norm, w_gate, w_up, w_down):
    batch, seq, d_model = x.shape
    depth = w_in.shape[0]
    pool_width = pool_scale.shape[1]
    attn_width = w_out.shape[1] - pool_width
    qk_width = (w_in.shape[2] - pool_width - attn_width) // 2
    n_heads = attn_width // DIFF_V_DIM
    assert qk_width == n_heads * 2 * DIFF_HEAD_DIM
    assert seq % TQ == 0 and seq % TM_IN == 0 and (batch * seq) % TM_FFN == 0
    assert TM_IN == TK

    n = batch * seq
    h = x.reshape(n, d_model)
    pos2 = positions.reshape(n, 1)
    inv_freq = ROPE_THETA ** (-jnp.arange(0, ROT_DIM, 2, dtype=F32) / ROT_DIM)
    lane_dim = jnp.arange(LANES) % DIFF_HEAD_DIM
    inv_freq_lane = jnp.where(lane_dim < ROT_DIM, inv_freq[lane_dim % ROT_HALF], 0.0).reshape(1, LANES)

    for l in range(depth):
        lambda_init = _lambda_init(l)
        pool_out, q, k, vt = _in_proj(
            h, pos2, inv_freq_lane, pre_mix_norm[l].reshape(1, -1), w_in[l], pool_w[l],
            pool_scale[l].reshape(1, -1),
            seq=seq, pool_width=pool_width, qk_width=qk_width, attn_width=attn_width)
        lam_vecs = jnp.stack([lam_q1[l], lam_k1[l], lam_q2[l], lam_k2[l]]).astype(F32)
        attn_out, (wg_bf, wu_bf, wd_bf) = _attention(
            q, k, vt, lam_vecs, subln_w[l].reshape(1, -1), (w_gate[l], w_up[l], w_down[l]),
            batch=batch, seq=seq, n_heads=n_heads, lambda_init=lambda_init)
        h = _out_proj(pool_out, attn_out, h, w_out[l], post_mix_norm[l].reshape(1, -1))
        h = _ffn(h, pre_ffn_norm[l].reshape(1, -1), post_ffn_norm[l].reshape(1, -1), wg_bf, wu_bf, wd_bf)
    return h.reshape(batch, seq, d_model)
```

```python
import functools
import math

import jax
import jax.numpy as jnp
from jax import lax
from jax.experimental import pallas as pl
from jax.experimental.pallas import tpu as pltpu

F32 = jnp.float32
BF16 = jnp.bfloat16

POOL_WINDOWS = (2, 4, 8, 16)
DIFF_HEAD_DIM = 64
DIFF_V_DIM = 2 * DIFF_HEAD_DIM
ROPE_THETA = 500000.0
ROT_DIM = DIFF_HEAD_DIM // 4
ROT_HALF = ROT_DIM // 2
NORM_EPS = 1e-6
NEG_INF = -1e30
LOG2_E = math.log2(math.e)

LANES = 128
POOL_HALO = 32
VMEM_LIMIT_BYTES = 56 * 1024 * 1024

TM_IN = 512
TQ = 1024
TK = 512
ATTN_CHUNK = 256
ATTN_UNROLL = 4
TM_OUT = 512
TM_FFN = 1024
TF_FFN = 512
N_CHUNK = 512
NORM_ROWS = 64
IN_CAST_STEPS = 8
OUT_CAST_STEPS = 4


def _lambda_init(layer_idx):
    return 0.8 - 0.6 * math.exp(-0.3 * layer_idx)


def _rms_norm(xf, g):
    ms = jnp.mean(xf * xf, axis=-1, keepdims=True)
    return xf * lax.rsqrt(ms + NORM_EPS) * g


def _rms_norm_rows(src_ref, g_ref, scale_ref, emit, unroll):
    n = src_ref.shape[0]

    def scale_body(c, carry):
        rows = pl.ds(pl.multiple_of(c * NORM_ROWS, NORM_ROWS), NORM_ROWS)
        x = src_ref[rows, :]
        scale_ref[rows, :] = lax.rsqrt(jnp.mean(x * x, axis=-1, keepdims=True) + NORM_EPS)
        return carry

    lax.fori_loop(0, n // NORM_ROWS, scale_body, 0, unroll=unroll)

    def apply_body(c, carry):
        rows = pl.ds(pl.multiple_of(c * NORM_ROWS, NORM_ROWS), NORM_ROWS)
        emit(rows, src_ref[rows, :] * scale_ref[rows, :] * g_ref[...])
        return carry

    lax.fori_loop(0, n // NORM_ROWS, apply_body, 0, unroll=unroll)


def _in_proj_kernel(x_ref, pos_ref, invf_ref, g_ref, w_ref, pw_ref, ps_ref,
                    pool_ref, q_ref, k_ref, vt_ref, w_bf_ref, wvt_ref, trig_ref, hn_ref, u_ref, carry_ref, nscale_ref,
                    *, seq, pool_width, qk_width):
    tm = x_ref.shape[0]
    step = pl.program_id(0)
    rows_per_cast = w_ref.shape[0]
    n_cast = w_bf_ref.shape[0] // rows_per_cast
    n_direct = w_bf_ref.shape[1]
    tiles_per_cast = pos_ref.shape[0] // tm
    lane = lax.broadcasted_iota(jnp.int32, (1, LANES), 1) % DIFF_HEAD_DIM

    @pl.when(step < n_cast)
    def _():
        r0 = pl.multiple_of(step * rows_per_cast, rows_per_cast)
        w_bf_ref[pl.ds(r0, rows_per_cast), :] = w_ref[:, :n_direct].astype(BF16)
        for t in range(tiles_per_cast):
            ang = pos_ref[t * tm:(t + 1) * tm, :].astype(F32) * invf_ref[...]
            trig_ref[step * tiles_per_cast + t] = jnp.where(
                lane < ROT_HALF, jnp.cos(ang), jnp.where(lane < ROT_DIM, jnp.sin(ang), 0.0))

    for t in range(n_cast):
        @pl.when(step == t)
        def _():
            wvt_ref[:, t * rows_per_cast:(t + 1) * rows_per_cast] = w_ref[:, n_direct:].T.astype(BF16)

    @pl.when(step >= n_cast)
    def _():
        tiles_per_seq = seq // tm
        ti = (step - n_cast) % tiles_per_seq

        hn_ref[...] = _rms_norm(x_ref[...], g_ref[...]).astype(BF16)

        @pl.when(ti == 0)
        def _():
            carry_ref[...] = jnp.zeros_like(carry_ref)

        trig = trig_ref[step - n_cast]
        coef_self = jnp.where(lane < ROT_HALF, trig,
                              jnp.where(lane < ROT_DIM, pltpu.roll(trig, ROT_HALF, 1), 1.0))
        coef_lo = jnp.where((lane >= ROT_HALF) & (lane < ROT_DIM), trig, 0.0)
        coef_hi = jnp.where(lane < ROT_HALF, -pltpu.roll(trig, LANES - ROT_HALF, 1), 0.0)

        def rope_chunk(out_ref, chunk, c, scale):
            t = jnp.dot(hn_ref[...], w_bf_ref[:, chunk * N_CHUNK:(chunk + 1) * N_CHUNK],
                        preferred_element_type=F32)
            for h in range(N_CHUNK // LANES):
                th = t[:, h * LANES:(h + 1) * LANES]
                r = (th * coef_self + pltpu.roll(th, ROT_HALF, 1) * coef_lo
                     + pltpu.roll(th, LANES - ROT_HALF, 1) * coef_hi)
                if scale != 1.0:
                    r = r * scale
                out_ref[:, c + h * LANES:c + (h + 1) * LANES] = r.astype(BF16)

        gdim = pool_width // len(POOL_WINDOWS)
        t_in_seq = ti * tm + lax.broadcasted_iota(jnp.int32, (tm, 1), 0)

        def pool_group(g, w):
            c0 = g * gdim
            u = u_ref[:, c0:c0 + gdim]
            ext = jnp.concatenate([carry_ref[g], u], axis=0)
            carry_ref[g] = u[tm - POOL_HALO:, :]
            lvl, off, k = ext, 0, 1
            while k < w:
                new_off = min(off + 8, POOL_HALO)
                cur = lvl[new_off - off:, :]
                shifted = lvl[new_off - off - k: lvl.shape[0] - k, :]
                lvl, off, k = cur + shifted, new_off, 2 * k
            wsum = lvl[POOL_HALO - off:, :]
            cnt = jnp.minimum(t_in_seq + 1, w).astype(F32)
            pooled = (wsum / cnt - u).astype(BF16)
            mixed = jnp.dot(pooled, pw_ref[g].astype(BF16), preferred_element_type=F32)
            pool_ref[:, c0:c0 + gdim] = (mixed * ps_ref[:, c0:c0 + gdim]).astype(BF16)

        n_pool, n_qk = pool_width // N_CHUNK, qk_width // N_CHUNK
        for ci in range(n_pool):
            u_ref[:, ci * N_CHUNK:(ci + 1) * N_CHUNK] = jnp.dot(
                hn_ref[...], w_bf_ref[:, ci * N_CHUNK:(ci + 1) * N_CHUNK], preferred_element_type=F32)

        q_scale = DIFF_HEAD_DIM ** -0.5 * LOG2_E
        chunks = ([(q_ref, n_pool + ci, ci * N_CHUNK, q_scale) for ci in range(n_qk)]
                  + [(k_ref, n_pool + n_qk + ci, ci * N_CHUNK, 1.0) for ci in range(n_qk)])
        groups = list(enumerate(POOL_WINDOWS))
        for idx, args in enumerate(chunks):
            rope_chunk(*args)
            for g, w in groups[idx * len(groups) // len(chunks):(idx + 1) * len(groups) // len(chunks)]:
                pool_group(g, w)

        for c in range(0, vt_ref.shape[0], N_CHUNK):
            vt_ref[c:c + N_CHUNK, :] = lax.dot_general(
                wvt_ref[c:c + N_CHUNK, :], hn_ref[...], (((1,), (1,)), ((), ())),
                preferred_element_type=F32).astype(BF16)


def _in_proj(x2, pos2, inv_freq, g, w_in, pool_w, pool_scale, *, seq, pool_width, qk_width, attn_width):
    n, d = x2.shape
    tm = TM_IN
    n_groups = len(POOL_WINDOWS)
    gdim = pool_width // n_groups
    n_direct = pool_width + 2 * qk_width
    n_cast = IN_CAST_STEPS
    n_tiles = n // tm
    assert n_tiles % n_cast == 0
    tiles_per_cast = n_tiles // n_cast
    const = lambda s: (0, 0)
    row = lambda s: (jnp.maximum(s - n_cast, 0), 0)
    kern = functools.partial(_in_proj_kernel, seq=seq, pool_width=pool_width, qk_width=qk_width)
    return pl.pallas_call(
        kern,
        grid=(n_cast + n_tiles,),
        in_specs=[
            pl.BlockSpec((tm, d), row),
            pl.BlockSpec((tiles_per_cast * tm, 1), lambda s: (jnp.minimum(s, n_cast - 1), 0)),
            pl.BlockSpec((1, LANES), const),
            pl.BlockSpec((1, d), const),
            pl.BlockSpec((d // n_cast, w_in.shape[1]), lambda s: (jnp.minimum(s, n_cast - 1), 0)),
            pl.BlockSpec(pool_w.shape, lambda s: (0, 0, 0)),
            pl.BlockSpec((1, pool_width), const),
        ],
        out_specs=[
            pl.BlockSpec((tm, pool_width), row),
            pl.BlockSpec((tm, qk_width), row),
            pl.BlockSpec((tm, qk_width), row),
            pl.BlockSpec((None, attn_width, tm), lambda s: (jnp.maximum(s - n_cast, 0), 0, 0)),
        ],
        out_shape=[
            jax.ShapeDtypeStruct((n, pool_width), BF16),
            jax.ShapeDtypeStruct((n, qk_width), BF16),
            jax.ShapeDtypeStruct((n, qk_width), BF16),
            jax.ShapeDtypeStruct((n // tm, attn_width, tm), BF16),
        ],
        scratch_shapes=[
            pltpu.VMEM((d, n_direct), BF16),
            pltpu.VMEM((attn_width, d), BF16),
            pltpu.VMEM((n_tiles, tm, LANES), F32),
            pltpu.VMEM((tm, d), BF16),
            pltpu.VMEM((tm, pool_width), F32),
            pltpu.VMEM((n_groups, POOL_HALO, gdim), F32),
            pltpu.VMEM((tm, 1), F32),
        ],
        compiler_params=pltpu.CompilerParams(
            dimension_semantics=("arbitrary",), vmem_limit_bytes=VMEM_LIMIT_BYTES),
        name="in_proj",
    )(x2, pos2, inv_freq, g, w_in, pool_w, pool_scale)


def _attn_kernel(lam_ref, sw_ref, q_ref, k_ref, vt_ref, wg_ref, wu_ref, wd_ref,
                 o_ref, wg_bf_ref, wu_bf_ref, wd_bf_ref,
                 qc_ref, s0_ref, s1_ref, mx0_ref, mx1_ref, m_ref, l_ref, acc_ref, *, lambda_init):
    wg_bf_ref[...] = wg_ref[...].astype(BF16)
    wu_bf_ref[...] = wu_ref[...].astype(BF16)
    wd_bf_ref[...] = wd_ref[...].astype(BF16)

    seq = q_ref.shape[0]
    n_tiles = seq // TQ
    n_unmasked = n_tiles * (n_tiles - 1)
    assert n_tiles >= 2 and n_unmasked % ATTN_UNROLL == 0 and ATTN_UNROLL % 2 == 0
    s_slots, mx_slots = (s0_ref, s1_ref), (mx0_ref, mx1_ref)
    chunks = [slice(a, a + ATTN_CHUNK) for a in range(0, TQ, ATTN_CHUNK)]
    chunk_plan = {
        "full": [(ch, None) for ch in chunks],
        "lo": [(ch, ch.start if ch.start < TK else None) for ch in chunks],
        "hi": [(ch, ch.start - TK) for ch in chunks if ch.start >= TK],
    }

    def rows(i, size):
        return pl.ds(i * size if isinstance(i, int) else pl.multiple_of(i * size, size), size)

    def scores_chunk(qi, j, slot, c, ch, offset):
        s = lax.dot_general(k_ref[rows(j, TK), :], qc_ref[qi, c, ch, :], (((1,), (1,)), ((), ())),
                            preferred_element_type=F32)
        if offset is not None:
            kv = lax.broadcasted_iota(jnp.int32, s.shape, 0)
            r = lax.broadcasted_iota(jnp.int32, s.shape, 1)
            s = jnp.where(kv <= r + offset, s, NEG_INF)
        s_slots[slot][c, :, ch] = s
        mx_slots[slot][c, :, ch] = jnp.max(s, axis=0, keepdims=True)

    def exp_pv_chunk(qi, j, slot, c, ch):
        m_prev = m_ref[qi, c, :, ch]
        m_new = jnp.maximum(m_prev, mx_slots[slot][c, :, ch])
        alpha = jnp.exp2(m_prev - m_new)
        p = jnp.exp2(s_slots[slot][c, :, ch] - m_new)
        l_ref[qi, c, :, ch] = alpha * l_ref[qi, c, :, ch] + jnp.sum(p, axis=0, keepdims=True)
        pv = jnp.dot(vt_ref[j], p.astype(BF16), preferred_element_type=F32)
        acc_ref[qi, c, :, ch] = alpha * acc_ref[qi, c, :, ch] + pv
        m_ref[qi, c, :, ch] = m_new

    def stage(cur, cur_kind, slot, nxt, nxt_kind):
        todo_next = chunk_plan[nxt_kind] if nxt is not None else []
        todo_cur = [ch for ch, _ in chunk_plan[cur_kind]]
        for c in range(2):
            for i in range(max(len(todo_next), len(todo_cur))):
                if i < len(todo_next):
                    scores_chunk(nxt[0], nxt[1], 1 - slot, c, *todo_next[i])
                if i < len(todo_cur):
                    exp_pv_chunk(cur[0], cur[1], slot, c, todo_cur[i])

    def finalize(qi):
        lam_v = lam_ref[...]
        lam = (jnp.exp(jnp.sum(lam_v[0:1] * lam_v[1:2], keepdims=True))
               - jnp.exp(jnp.sum(lam_v[2:3] * lam_v[3:4], keepdims=True)) + lambda_init)
        attn_t = acc_ref[qi, 0] / l_ref[qi, 0] - lam * (acc_ref[qi, 1] / l_ref[qi, 1])
        ms = jnp.mean(attn_t * attn_t, axis=0, keepdims=True)
        y = (attn_t * lax.rsqrt(ms + NORM_EPS)).T
        o_ref[rows(qi, TQ), :] = (y * sw_ref[...] * (1.0 - lambda_init)).astype(BF16)

    lane = lax.broadcasted_iota(jnp.int32, (TQ, DIFF_V_DIM), 1)
    for qi in range(n_tiles):
        q = q_ref[qi * TQ:(qi + 1) * TQ, :]
        zero = jnp.zeros_like(q)
        qc_ref[qi, 0] = jnp.where(lane < DIFF_HEAD_DIM, q, zero)
        qc_ref[qi, 1] = jnp.where(lane >= DIFF_HEAD_DIM, q, zero)
    m_ref[...] = jnp.full_like(m_ref, NEG_INF)
    l_ref[...] = jnp.zeros_like(l_ref)
    acc_ref[...] = jnp.zeros_like(acc_ref)

    def unmasked_pair(n):
        qi = 1
        for t in range(2, n_tiles):
            qi = qi + jnp.where(n >= t * (t - 1), 1, 0)
        return qi, n - qi * (qi - 1)

    for c in range(2):
        for ch, off in chunk_plan["full"]:
            scores_chunk(1, 0, 0, c, ch, off)

    def unmasked_body(it, carry):
        for u in range(ATTN_UNROLL):
            n = it * ATTN_UNROLL + u
            stage(unmasked_pair(n), "full", u % 2, unmasked_pair(jnp.minimum(n + 1, n_unmasked - 1)), "full")
        return carry

    lax.fori_loop(0, n_unmasked // ATTN_UNROLL, unmasked_body, 0)

    def masked_stages(qi, nxt_qi):
        stage((qi, 2 * qi), "lo", 0, (qi, 2 * qi + 1), "hi")
        stage((qi, 2 * qi + 1), "hi", 1, (nxt_qi, 2 * nxt_qi), "lo")

    for c in range(2):
        for ch, off in chunk_plan["lo"]:
            scores_chunk(0, 0, 0, c, ch, off)
    masked_stages(0, 1)

    def masked_body(qi, carry):
        finalize(qi - 1)
        masked_stages(qi, jnp.minimum(qi + 1, n_tiles - 1))
        return carry

    lax.fori_loop(1, n_tiles, masked_body, 0)
    finalize(n_tiles - 1)


def _attention(q, k, vt, lam_vecs, subln_w, ffn_weights, *, batch, seq, n_heads, lambda_init):
    n, width = q.shape
    nk = seq // TK
    n_tiles = seq // TQ
    n_steps = batch * n_heads
    assert TQ == 2 * TK and seq % TQ == 0 and vt.shape == (n // TK, n_heads * DIFF_V_DIM, TK)
    kern = functools.partial(_attn_kernel, lambda_init=lambda_init)
    head = lambda b, h: (b, h)
    step_rows = lambda b, h: (b * n_heads + h, 0)
    w_specs = [pl.BlockSpec((w.shape[0] // n_steps, w.shape[1]), step_rows) for w in ffn_weights]
    outs = pl.pallas_call(
        kern,
        grid=(batch, n_heads),
        in_specs=[
            pl.BlockSpec(lam_vecs.shape, lambda b, h: (0, 0)),
            pl.BlockSpec((1, DIFF_V_DIM), lambda b, h: (0, 0)),
            pl.BlockSpec((seq, DIFF_V_DIM), head),
            pl.BlockSpec((seq, DIFF_V_DIM), head),
            pl.BlockSpec((nk, DIFF_V_DIM, TK), lambda b, h: (b, h, 0)),
        ] + w_specs,
        out_specs=[pl.BlockSpec((seq, DIFF_V_DIM), head)] + w_specs,
        out_shape=[jax.ShapeDtypeStruct((n, width), BF16)]
        + [jax.ShapeDtypeStruct(w.shape, BF16) for w in ffn_weights],
        scratch_shapes=[
            pltpu.VMEM((n_tiles, 2, TQ, DIFF_V_DIM), BF16),
            pltpu.VMEM((2, TK, TQ), F32),
            pltpu.VMEM((2, TK, TQ), F32),
            pltpu.VMEM((2, 1, TQ), F32),
            pltpu.VMEM((2, 1, TQ), F32),
            pltpu.VMEM((n_tiles, 2, 1, TQ), F32),
            pltpu.VMEM((n_tiles, 2, 1, TQ), F32),
            pltpu.VMEM((n_tiles, 2, DIFF_V_DIM, TQ), F32),
        ],
        compiler_params=pltpu.CompilerParams(
            dimension_semantics=("arbitrary", "arbitrary"), vmem_limit_bytes=VMEM_LIMIT_BYTES),
        name="diff_attention",
    )(lam_vecs, subln_w, q, k, vt, *ffn_weights)
    return outs[0], outs[1:]


def _out_proj_kernel(pool_ref, attn_ref, x_ref, w_ref, g_ref, h_ref, w_bf_ref, mix_ref, nscale_ref):
    step = pl.program_id(0)
    rows_per_cast = w_ref.shape[0]
    n_cast = w_bf_ref.shape[0] // rows_per_cast
    pw = pool_ref.shape[1]

    @pl.when(step < n_cast)
    def _():
        r0 = pl.multiple_of(step * rows_per_cast, rows_per_cast)
        w_bf_ref[pl.ds(r0, rows_per_cast), :] = w_ref[...].astype(BF16)

    @pl.when(step >= n_cast)
    def _():
        for c in range(0, mix_ref.shape[1], N_CHUNK):
            mix_ref[:, c:c + N_CHUNK] = (
                jnp.dot(pool_ref[...], w_bf_ref[:pw, c:c + N_CHUNK], preferred_element_type=F32)
                + jnp.dot(attn_ref[...], w_bf_ref[pw:, c:c + N_CHUNK], preferred_element_type=F32))

        def store_h(rows, y):
            h_ref[rows, :] = x_ref[rows, :] + y

        _rms_norm_rows(mix_ref, g_ref, nscale_ref, store_h, unroll=True)


def _out_proj(pool_out, attn_out, x2, w_out, g):
    n, d = x2.shape
    tm = TM_OUT
    n_cast = OUT_CAST_STEPS
    row = lambda s: (jnp.maximum(s - n_cast, 0), 0)
    const = lambda s: (0, 0)
    return pl.pallas_call(
        _out_proj_kernel,
        grid=(n_cast + n // tm,),
        in_specs=[
            pl.BlockSpec((tm, pool_out.shape[1]), row),
            pl.BlockSpec((tm, attn_out.shape[1]), row),
            pl.BlockSpec((tm, d), row),
            pl.BlockSpec((w_out.shape[0] // n_cast, w_out.shape[1]), lambda s: (jnp.minimum(s, n_cast - 1), 0)),
            pl.BlockSpec((1, d), const),
        ],
        out_specs=pl.BlockSpec((tm, d), row),
        out_shape=jax.ShapeDtypeStruct((n, d), F32),
        scratch_shapes=[
            pltpu.VMEM(w_out.shape, BF16),
            pltpu.VMEM((tm, d), F32),
            pltpu.VMEM((tm, 1), F32),
        ],
        compiler_params=pltpu.CompilerParams(
            dimension_semantics=("arbitrary",), vmem_limit_bytes=VMEM_LIMIT_BYTES),
        name="out_proj",
    )(pool_out, attn_out, x2, w_out, g)


def _ffn_kernel(h_ref, gpre_ref, gpost_ref, wg_ref, wu_ref, wd_ref, o_ref, hn_ref, nscale_ref):
    f = pl.program_id(1)

    @pl.when(f == 0)
    def _():
        def store_hn(rows, y):
            hn_ref[rows, :] = y.astype(BF16)
            o_ref[rows, :] = jnp.zeros_like(y)

        _rms_norm_rows(h_ref, gpre_ref, nscale_ref, store_hn, unroll=True)

    hn = hn_ref[...]
    gate = jnp.dot(hn, wg_ref[...], preferred_element_type=F32)
    up = jnp.dot(hn, wu_ref[...], preferred_element_type=F32)
    act = (gate * jax.nn.sigmoid(gate) * up).astype(BF16)
    for c in range(0, o_ref.shape[1], N_CHUNK):
        o_ref[:, c:c + N_CHUNK] += jnp.dot(act, wd_ref[:, c:c + N_CHUNK], preferred_element_type=F32)

    @pl.when(f == pl.num_programs(1) - 1)
    def _():
        def store_out(rows, y):
            o_ref[rows, :] = h_ref[rows, :] + y

        _rms_norm_rows(o_ref, gpost_ref, nscale_ref, store_out, unroll=True)


def _ffn(h, g_pre, g_post, w_gate, w_up, w_down):
    n, d = h.shape
    d_ff = w_gate.shape[1]
    tm, tf = TM_FFN, TF_FFN
    return pl.pallas_call(
        _ffn_kernel,
        grid=(n // tm, d_ff // tf),
        in_specs=[
            pl.BlockSpec((tm, d), lambda i, f: (i, 0), pipeline_mode=pl.Buffered(1)),
            pl.BlockSpec((1, d), lambda i, f: (0, 0)),
            pl.BlockSpec((1, d), lambda i, f: (0, 0)),
            pl.BlockSpec((d, tf), lambda i, f: (0, f)),
            pl.BlockSpec((d, tf), lambda i, f: (0, f)),
            pl.BlockSpec((tf, d), lambda i, f: (f, 0)),
        ],
        out_specs=pl.BlockSpec((tm, d), lambda i, f: (i, 0)),
        out_shape=jax.ShapeDtypeStruct((n, d), F32),
        scratch_shapes=[pltpu.VMEM((tm, d), BF16), pltpu.VMEM((tm, 1), F32)],
        compiler_params=pltpu.CompilerParams(
            dimension_semantics=("arbitrary", "arbitrary"), vmem_limit_bytes=VMEM_LIMIT_BYTES),
        name="ffn",
    )(h, g_pre, g_post, w_gate, w_up, w_down)


def kernel(x, positions, pre_mix_norm, post_mix_norm, w_in, pool_w, pool_scale,
           lam_q1, lam_k1, lam_q2, lam_k2, subln_w, w_out,
           pre_ffn_norm, post_ffn_norm, w_gate, w_up, w_down):
    batch, seq, d_model = x.shape
    depth = w_in.shape[0]
    pool_width = pool_scale.shape[1]
    attn_width = w_out.shape[1] - pool_width
    qk_width = (w_in.shape[2] - pool_width - attn_width) // 2
    n_heads = attn_width // DIFF_V_DIM
    assert qk_width == n_heads * 2 * DIFF_HEAD_DIM
    assert seq % TQ == 0 and seq % TM_IN == 0 and (batch * seq) % TM_FFN == 0
    assert TM_IN == TK

    n = batch * seq
    h = x.reshape(n, d_model)
    pos2 = positions.reshape(n, 1)
    lane_dim = jnp.arange(LANES) % DIFF_HEAD_DIM
    inv_freq = ROPE_THETA ** (-(2 * (lane_dim % ROT_HALF)).astype(F32) / ROT_DIM)
    inv_freq_lane = jnp.where(lane_dim < ROT_DIM, inv_freq, 0.0).reshape(1, LANES)

    for l in range(depth):
        lambda_init = _lambda_init(l)
        pool_out, q, k, vt = _in_proj(
            h, pos2, inv_freq_lane, pre_mix_norm[l].reshape(1, -1), w_in[l], pool_w[l],
            pool_scale[l].reshape(1, -1),
            seq=seq, pool_width=pool_width, qk_width=qk_width, attn_width=attn_width)
        lam_vecs = jnp.stack([lam_q1[l], lam_k1[l], lam_q2[l], lam_k2[l]]).astype(F32)
        attn_out, (wg_bf, wu_bf, wd_bf) = _attention(
            q, k, vt, lam_vecs, subln_w[l].reshape(1, -1), (w_gate[l], w_up[l], w_down[l]),
            batch=batch, seq=seq, n_heads=n_heads, lambda_init=lambda_init)
        h = _out_proj(pool_out, attn_out, h, w_out[l], post_mix_norm[l].reshape(1, -1))
        h = _ffn(h, pre_ffn_norm[l].reshape(1, -1), post_ffn_norm[l].reshape(1, -1), wg_bf, wu_bf, wd_bf)
    return h.reshape(batch, seq, d_model)
```

```python
import functools
import math

import jax
import jax.numpy as jnp
from jax import lax
from jax.experimental import pallas as pl
from jax.experimental.pallas import tpu as pltpu

F32 = jnp.float32
BF16 = jnp.bfloat16

POOL_WINDOWS = (2, 4, 8, 16)
DIFF_HEAD_DIM = 64
DIFF_V_DIM = 2 * DIFF_HEAD_DIM
ROPE_THETA = 500000.0
ROT_DIM = DIFF_HEAD_DIM // 4
ROT_HALF = ROT_DIM // 2
NORM_EPS = 1e-6
NEG_INF = -1e30
LOG2_E = math.log2(math.e)

LANES = 128
POOL_HALO = 32
VMEM_LIMIT_BYTES = 56 * 1024 * 1024

TM_IN = 512
TQ = 1024
TK = 512
ATTN_CHUNK = 256
ATTN_UNROLL = 4
TM_OUT = 512
TM_FFN = 1024
TF_FFN = 512
N_CHUNK = 512
NORM_ROWS = 64
IN_CAST_STEPS = 8
OUT_CAST_STEPS = 4


def _lambda_init(layer_idx):
    return 0.8 - 0.6 * math.exp(-0.3 * layer_idx)


def _rms_norm(xf, g):
    ms = jnp.mean(xf * xf, axis=-1, keepdims=True)
    return xf * lax.rsqrt(ms + NORM_EPS) * g


def _rms_norm_rows(src_ref, g_ref, scale_ref, emit, unroll):
    n = src_ref.shape[0]

    def scale_body(c, carry):
        rows = pl.ds(pl.multiple_of(c * NORM_ROWS, NORM_ROWS), NORM_ROWS)
        x = src_ref[rows, :]
        scale_ref[rows, :] = lax.rsqrt(jnp.mean(x * x, axis=-1, keepdims=True) + NORM_EPS)
        return carry

    lax.fori_loop(0, n // NORM_ROWS, scale_body, 0, unroll=unroll)

    def apply_body(c, carry):
        rows = pl.ds(pl.multiple_of(c * NORM_ROWS, NORM_ROWS), NORM_ROWS)
        emit(rows, src_ref[rows, :] * scale_ref[rows, :] * g_ref[...])
        return carry

    lax.fori_loop(0, n // NORM_ROWS, apply_body, 0, unroll=unroll)


def _in_proj_kernel(x_ref, pos_ref, invf_ref, g_ref, w_ref, pw_ref, ps_ref,
                    pool_ref, q_ref, k_ref, vt_ref, w_bf_ref, wvt_ref, trig_ref, hn_ref, u_ref, carry_ref, nscale_ref,
                    *, seq, pool_width, qk_width):
    tm = x_ref.shape[0]
    step = pl.program_id(0)
    rows_per_cast = w_ref.shape[0]
    n_cast = w_bf_ref.shape[0] // rows_per_cast
    n_direct = w_bf_ref.shape[1]
    tiles_per_cast = pos_ref.shape[0] // tm
    lane = lax.broadcasted_iota(jnp.int32, (1, LANES), 1) % DIFF_HEAD_DIM

    @pl.when(step < n_cast)
    def _():
        r0 = pl.multiple_of(step * rows_per_cast, rows_per_cast)
        w_bf_ref[pl.ds(r0, rows_per_cast), :] = w_ref[:, :n_direct].astype(BF16)
        for t in range(tiles_per_cast):
            ang = pos_ref[t * tm:(t + 1) * tm, :].astype(F32) * invf_ref[...]
            trig_ref[step * tiles_per_cast + t] = jnp.where(
                lane < ROT_HALF, jnp.cos(ang), jnp.where(lane < ROT_DIM, jnp.sin(ang), 0.0))

    for t in range(n_cast):
        @pl.when(step == t)
        def _():
            wvt_ref[:, t * rows_per_cast:(t + 1) * rows_per_cast] = w_ref[:, n_direct:].T.astype(BF16)

    @pl.when(step >= n_cast)
    def _():
        tiles_per_seq = seq // tm
        ti = (step - n_cast) % tiles_per_seq

        hn_ref[...] = _rms_norm(x_ref[...], g_ref[...]).astype(BF16)

        @pl.when(ti == 0)
        def _():
            carry_ref[...] = jnp.zeros_like(carry_ref)

        trig = trig_ref[step - n_cast]
        coef_self = jnp.where(lane < ROT_HALF, trig,
                              jnp.where(lane < ROT_DIM, pltpu.roll(trig, ROT_HALF, 1), 1.0))
        coef_lo = jnp.where((lane >= ROT_HALF) & (lane < ROT_DIM), trig, 0.0)
        coef_hi = jnp.where(lane < ROT_HALF, -pltpu.roll(trig, LANES - ROT_HALF, 1), 0.0)

        def rope_chunk(out_ref, chunk, c, scale):
            t = jnp.dot(hn_ref[...], w_bf_ref[:, chunk * N_CHUNK:(chunk + 1) * N_CHUNK],
                        preferred_element_type=F32)
            for h in range(N_CHUNK // LANES):
                th = t[:, h * LANES:(h + 1) * LANES]
                r = (th * coef_self + pltpu.roll(th, ROT_HALF, 1) * coef_lo
                     + pltpu.roll(th, LANES - ROT_HALF, 1) * coef_hi)
                if scale != 1.0:
                    r = r * scale
                out_ref[:, c + h * LANES:c + (h + 1) * LANES] = r.astype(BF16)

        gdim = pool_width // len(POOL_WINDOWS)
        t_in_seq = ti * tm + lax.broadcasted_iota(jnp.int32, (tm, 1), 0)

        def pool_group(g, w):
            c0 = g * gdim
            u = u_ref[:, c0:c0 + gdim]
            ext = jnp.concatenate([carry_ref[g], u], axis=0)
            carry_ref[g] = u[tm - POOL_HALO:, :]
            lvl, off, k = ext, 0, 1
            while k < w:
                new_off = min(off + 8, POOL_HALO)
                cur = lvl[new_off - off:, :]
                shifted = lvl[new_off - off - k: lvl.shape[0] - k, :]
                lvl, off, k = cur + shifted, new_off, 2 * k
            wsum = lvl[POOL_HALO - off:, :]
            cnt = jnp.minimum(t_in_seq + 1, w).astype(F32)
            pooled = (wsum / cnt - u).astype(BF16)
            mixed = jnp.dot(pooled, pw_ref[g].astype(BF16), preferred_element_type=F32)
            pool_ref[:, c0:c0 + gdim] = (mixed * ps_ref[:, c0:c0 + gdim]).astype(BF16)

        n_pool, n_qk = pool_width // N_CHUNK, qk_width // N_CHUNK
        for ci in range(n_pool):
            u_ref[:, ci * N_CHUNK:(ci + 1) * N_CHUNK] = jnp.dot(
                hn_ref[...], w_bf_ref[:, ci * N_CHUNK:(ci + 1) * N_CHUNK], preferred_element_type=F32)

        q_scale = DIFF_HEAD_DIM ** -0.5 * LOG2_E
        chunks = ([(q_ref, n_pool + ci, ci * N_CHUNK, q_scale) for ci in range(n_qk)]
                  + [(k_ref, n_pool + n_qk + ci, ci * N_CHUNK, 1.0) for ci in range(n_qk)])
        groups = list(enumerate(POOL_WINDOWS))
        for idx, args in enumerate(chunks):
            rope_chunk(*args)
            for g, w in groups[idx * len(groups) // len(chunks):(idx + 1) * len(groups) // len(chunks)]:
                pool_group(g, w)

        for c in range(0, vt_ref.shape[0], N_CHUNK):
            vt_ref[c:c + N_CHUNK, :] = lax.dot_general(
                wvt_ref[c:c + N_CHUNK, :], hn_ref[...], (((1,), (1,)), ((), ())),
                preferred_element_type=F32).astype(BF16)


def _in_proj(x2, pos2, inv_freq, g, w_in, pool_w, pool_scale, *, seq, pool_width, qk_width, attn_width):
    n, d = x2.shape
    tm = TM_IN
    n_groups = len(POOL_WINDOWS)
    gdim = pool_width // n_groups
    n_direct = pool_width + 2 * qk_width
    n_cast = IN_CAST_STEPS
    n_tiles = n // tm
    assert n_tiles % n_cast == 0
    tiles_per_cast = n_tiles // n_cast
    const = lambda s: (0, 0)
    row = lambda s: (jnp.maximum(s - n_cast, 0), 0)
    kern = functools.partial(_in_proj_kernel, seq=seq, pool_width=pool_width, qk_width=qk_width)
    return pl.pallas_call(
        kern,
        grid=(n_cast + n_tiles,),
        in_specs=[
            pl.BlockSpec((tm, d), row),
            pl.BlockSpec((tiles_per_cast * tm, 1), lambda s: (jnp.minimum(s, n_cast - 1), 0)),
            pl.BlockSpec((1, LANES), const),
            pl.BlockSpec((1, d), const),
            pl.BlockSpec((d // n_cast, w_in.shape[1]), lambda s: (jnp.minimum(s, n_cast - 1), 0)),
            pl.BlockSpec(pool_w.shape, lambda s: (0, 0, 0)),
            pl.BlockSpec((1, pool_width), const),
        ],
        out_specs=[
            pl.BlockSpec((tm, pool_width), row),
            pl.BlockSpec((tm, qk_width), row),
            pl.BlockSpec((tm, qk_width), row),
            pl.BlockSpec((None, attn_width, tm), lambda s: (jnp.maximum(s - n_cast, 0), 0, 0)),
        ],
        out_shape=[
            jax.ShapeDtypeStruct((n, pool_width), BF16),
            jax.ShapeDtypeStruct((n, qk_width), BF16),
            jax.ShapeDtypeStruct((n, qk_width), BF16),
            jax.ShapeDtypeStruct((n // tm, attn_width, tm), BF16),
        ],
        scratch_shapes=[
            pltpu.VMEM((d, n_direct), BF16),
            pltpu.VMEM((attn_width, d), BF16),
            pltpu.VMEM((n_tiles, tm, LANES), F32),
            pltpu.VMEM((tm, d), BF16),
            pltpu.VMEM((tm, pool_width), F32),
            pltpu.VMEM((n_groups, POOL_HALO, gdim), F32),
            pltpu.VMEM((tm, 1), F32),
        ],
        compiler_params=pltpu.CompilerParams(
            dimension_semantics=("arbitrary",), vmem_limit_bytes=VMEM_LIMIT_BYTES),
        name="in_proj",
    )(x2, pos2, inv_freq, g, w_in, pool_w, pool_scale)


def _attn_kernel(lam_ref, sw_ref, q_ref, k_ref, vt_ref, wg_ref, wu_ref, wd_ref,
                 o_ref, wg_bf_ref, wu_bf_ref, wd_bf_ref,
                 qc_ref, s0_ref, s1_ref, mx0_ref, mx1_ref, m_ref, l_ref, acc_ref, *, lambda_init):
    wg_bf_ref[...] = wg_ref[...].astype(BF16)
    wu_bf_ref[...] = wu_ref[...].astype(BF16)
    wd_bf_ref[...] = wd_ref[...].astype(BF16)

    seq = q_ref.shape[0]
    n_tiles = seq // TQ
    n_unmasked = n_tiles * (n_tiles - 1)
    assert n_tiles >= 2 and n_unmasked % ATTN_UNROLL == 0 and ATTN_UNROLL % 2 == 0
    s_slots, mx_slots = (s0_ref, s1_ref), (mx0_ref, mx1_ref)
    chunks = [slice(a, a + ATTN_CHUNK) for a in range(0, TQ, ATTN_CHUNK)]
    chunk_plan = {
        "full": [(ch, None) for ch in chunks],
        "lo": [(ch, ch.start if ch.start < TK else None) for ch in chunks],
        "hi": [(ch, ch.start - TK) for ch in chunks if ch.start >= TK],
    }

    def rows(i, size):
        return pl.ds(i * size if isinstance(i, int) else pl.multiple_of(i * size, size), size)

    def scores_chunk(qi, j, slot, c, ch, offset):
        s = lax.dot_general(k_ref[rows(j, TK), :], qc_ref[qi, c, ch, :], (((1,), (1,)), ((), ())),
                            preferred_element_type=F32)
        if offset is not None:
            kv = lax.broadcasted_iota(jnp.int32, s.shape, 0)
            r = lax.broadcasted_iota(jnp.int32, s.shape, 1)
            s = jnp.where(kv <= r + offset, s, NEG_INF)
        s_slots[slot][c, :, ch] = s
        mx_slots[slot][c, :, ch] = jnp.max(s, axis=0, keepdims=True)

    def exp_pv_chunk(qi, j, slot, c, ch):
        m_prev = m_ref[qi, c, :, ch]
        m_new = jnp.maximum(m_prev, mx_slots[slot][c, :, ch])
        alpha = jnp.exp2(m_prev - m_new)
        p = jnp.exp2(s_slots[slot][c, :, ch] - m_new)
        l_ref[qi, c, :, ch] = alpha * l_ref[qi, c, :, ch] + jnp.sum(p, axis=0, keepdims=True)
        pv = jnp.dot(vt_ref[j], p.astype(BF16), preferred_element_type=F32)
        acc_ref[qi, c, :, ch] = alpha * acc_ref[qi, c, :, ch] + pv
        m_ref[qi, c, :, ch] = m_new

    def stage(cur, cur_kind, slot, nxt, nxt_kind):
        todo_next = chunk_plan[nxt_kind] if nxt is not None else []
        todo_cur = [ch for ch, _ in chunk_plan[cur_kind]]
        for c in range(2):
            for i in range(max(len(todo_next), len(todo_cur))):
                if i < len(todo_next):
                    scores_chunk(nxt[0], nxt[1], 1 - slot, c, *todo_next[i])
                if i < len(todo_cur):
                    exp_pv_chunk(cur[0], cur[1], slot, c, todo_cur[i])

    def finalize(qi):
        lam_v = lam_ref[...]
        lam = (jnp.exp(jnp.sum(lam_v[0:1] * lam_v[1:2], keepdims=True))
               - jnp.exp(jnp.sum(lam_v[2:3] * lam_v[3:4], keepdims=True)) + lambda_init)
        attn_t = acc_ref[qi, 0] / l_ref[qi, 0] - lam * (acc_ref[qi, 1] / l_ref[qi, 1])
        ms = jnp.mean(attn_t * attn_t, axis=0, keepdims=True)
        y = (attn_t * lax.rsqrt(ms + NORM_EPS)).T
        o_ref[rows(qi, TQ), :] = (y * sw_ref[...] * (1.0 - lambda_init)).astype(BF16)

    lane = lax.broadcasted_iota(jnp.int32, (TQ, DIFF_V_DIM), 1)
    for qi in range(n_tiles):
        q = q_ref[qi * TQ:(qi + 1) * TQ, :]
        zero = jnp.zeros_like(q)
        qc_ref[qi, 0] = jnp.where(lane < DIFF_HEAD_DIM, q, zero)
        qc_ref[qi, 1] = jnp.where(lane >= DIFF_HEAD_DIM, q, zero)
    m_ref[...] = jnp.full_like(m_ref, NEG_INF)
    l_ref[...] = jnp.zeros_like(l_ref)
    acc_ref[...] = jnp.zeros_like(acc_ref)

    def unmasked_pair(n):
        qi = 1
        for t in range(2, n_tiles):
            qi = qi + jnp.where(n >= t * (t - 1), 1, 0)
        return qi, n - qi * (qi - 1)

    for c in range(2):
        for ch, off in chunk_plan["full"]:
            scores_chunk(1, 0, 0, c, ch, off)

    def unmasked_body(it, carry):
        for u in range(ATTN_UNROLL):
            n = it * ATTN_UNROLL + u
            stage(unmasked_pair(n), "full", u % 2, unmasked_pair(jnp.minimum(n + 1, n_unmasked - 1)), "full")
        return carry

    lax.fori_loop(0, n_unmasked // ATTN_UNROLL, unmasked_body, 0)

    def masked_stages(qi, nxt_qi):
        stage((qi, 2 * qi), "lo", 0, (qi, 2 * qi + 1), "hi")
        stage((qi, 2 * qi + 1), "hi", 1, (nxt_qi, 2 * nxt_qi), "lo")

    for c in range(2):
        for ch, off in chunk_plan["lo"]:
            scores_chunk(0, 0, 0, c, ch, off)
    masked_stages(0, 1)

    def masked_body(qi, carry):
        finalize(qi - 1)
        masked_stages(qi, jnp.minimum(qi + 1, n_tiles - 1))
        return carry

    lax.fori_loop(1, n_tiles, masked_body, 0)
    finalize(n_tiles - 1)


def _attention(q, k, vt, lam_vecs, subln_w, ffn_weights, *, batch, seq, n_heads, lambda_init):
    n, width = q.shape
    nk = seq // TK
    n_tiles = seq // TQ
    n_steps = batch * n_heads
    assert TQ == 2 * TK and seq % TQ == 0 and vt.shape == (n // TK, n_heads * DIFF_V_DIM, TK)
    kern = functools.partial(_attn_kernel, lambda_init=lambda_init)
    head = lambda b, h: (b, h)
    step_rows = lambda b, h: (b * n_heads + h, 0)
    w_specs = [pl.BlockSpec((w.shape[0] // n_steps, w.shape[1]), step_rows) for w in ffn_weights]
    outs = pl.pallas_call(
        kern,
        grid=(batch, n_heads),
        in_specs=[
            pl.BlockSpec(lam_vecs.shape, lambda b, h: (0, 0)),
            pl.BlockSpec((1, DIFF_V_DIM), lambda b, h: (0, 0)),
            pl.BlockSpec((seq, DIFF_V_DIM), head),
            pl.BlockSpec((seq, DIFF_V_DIM), head),
            pl.BlockSpec((nk, DIFF_V_DIM, TK), lambda b, h: (b, h, 0)),
        ] + w_specs,
        out_specs=[pl.BlockSpec((seq, DIFF_V_DIM), head)] + w_specs,
        out_shape=[jax.ShapeDtypeStruct((n, width), BF16)]
        + [jax.ShapeDtypeStruct(w.shape, BF16) for w in ffn_weights],
        scratch_shapes=[
            pltpu.VMEM((n_tiles, 2, TQ, DIFF_V_DIM), BF16),
            pltpu.VMEM((2, TK, TQ), F32),
            pltpu.VMEM((2, TK, TQ), F32),
            pltpu.VMEM((2, 1, TQ), F32),
            pltpu.VMEM((2, 1, TQ), F32),
            pltpu.VMEM((n_tiles, 2, 1, TQ), F32),
            pltpu.VMEM((n_tiles, 2, 1, TQ), F32),
            pltpu.VMEM((n_tiles, 2, DIFF_V_DIM, TQ), F32),
        ],
        compiler_params=pltpu.CompilerParams(
            dimension_semantics=("arbitrary", "arbitrary"), vmem_limit_bytes=VMEM_LIMIT_BYTES),
        name="diff_attention",
    )(lam_vecs, subln_w, q, k, vt, *ffn_weights)
    return outs[0], outs[1:]


def _out_proj_kernel(pool_ref, attn_ref, x_ref, w_ref, g_ref, h_ref, w_bf_ref, mix_ref, nscale_ref):
    step = pl.program_id(0)
    rows_per_cast = w_ref.shape[0]
    n_cast = w_bf_ref.shape[0] // rows_per_cast
    pw = pool_ref.shape[1]

    @pl.when(step < n_cast)
    def _():
        r0 = pl.multiple_of(step * rows_per_cast, rows_per_cast)
        w_bf_ref[pl.ds(r0, rows_per_cast), :] = w_ref[...].astype(BF16)

    @pl.when(step >= n_cast)
    def _():
        for c in range(0, mix_ref.shape[1], N_CHUNK):
            mix_ref[:, c:c + N_CHUNK] = (
                jnp.dot(pool_ref[...], w_bf_ref[:pw, c:c + N_CHUNK], preferred_element_type=F32)
                + jnp.dot(attn_ref[...], w_bf_ref[pw:, c:c + N_CHUNK], preferred_element_type=F32))

        def store_h(rows, y):
            h_ref[rows, :] = x_ref[rows, :] + y

        _rms_norm_rows(mix_ref, g_ref, nscale_ref, store_h, unroll=True)


def _out_proj(pool_out, attn_out, x2, w_out, g):
    n, d = x2.shape
    tm = TM_OUT
    n_cast = OUT_CAST_STEPS
    row = lambda s: (jnp.maximum(s - n_cast, 0), 0)
    const = lambda s: (0, 0)
    return pl.pallas_call(
        _out_proj_kernel,
        grid=(n_cast + n // tm,),
        in_specs=[
            pl.BlockSpec((tm, pool_out.shape[1]), row),
            pl.BlockSpec((tm, attn_out.shape[1]), row),
            pl.BlockSpec((tm, d), row),
            pl.BlockSpec((w_out.shape[0] // n_cast, w_out.shape[1]), lambda s: (jnp.minimum(s, n_cast - 1), 0)),
            pl.BlockSpec((1, d), const),
        ],
        out_specs=pl.BlockSpec((tm, d), row),
        out_shape=jax.ShapeDtypeStruct((n, d), F32),
        scratch_shapes=[
            pltpu.VMEM(w_out.shape, BF16),
            pltpu.VMEM((tm, d), F32),
            pltpu.VMEM((tm, 1), F32),
        ],
        compiler_params=pltpu.CompilerParams(
            dimension_semantics=("arbitrary",), vmem_limit_bytes=VMEM_LIMIT_BYTES),
        name="out_proj",
    )(pool_out, attn_out, x2, w_out, g)


def _ffn_kernel(h_ref, gpre_ref, gpost_ref, wg_ref, wu_ref, wd_ref, o_ref, hn_ref, nscale_ref):
    f = pl.program_id(1)

    @pl.when(f == 0)
    def _():
        def store_hn(rows, y):
            hn_ref[rows, :] = y.astype(BF16)
            o_ref[rows, :] = jnp.zeros_like(y)

        _rms_norm_rows(h_ref, gpre_ref, nscale_ref, store_hn, unroll=True)

    hn = hn_ref[...]
    gate = jnp.dot(hn, wg_ref[...], preferred_element_type=F32)
    up = jnp.dot(hn, wu_ref[...], preferred_element_type=F32)
    act = (gate * jax.nn.sigmoid(gate) * up).astype(BF16)
    for c in range(0, o_ref.shape[1], N_CHUNK):
        o_ref[:, c:c + N_CHUNK] += jnp.dot(act, wd_ref[:, c:c + N_CHUNK], preferred_element_type=F32)

    @pl.when(f == pl.num_programs(1) - 1)
    def _():
        def store_out(rows, y):
            o_ref[rows, :] = h_ref[rows, :] + y

        _rms_norm_rows(o_ref, gpost_ref, nscale_ref, store_out, unroll=True)


def _ffn(h, g_pre, g_post, w_gate, w_up, w_down):
    n, d = h.shape
    d_ff = w_gate.shape[1]
    tm, tf = TM_FFN, TF_FFN
    return pl.pallas_call(
        _ffn_kernel,
        grid=(n // tm, d_ff // tf),
        in_specs=[
            pl.BlockSpec((tm, d), lambda i, f: (i, 0)),
            pl.BlockSpec((1, d), lambda i, f: (0, 0)),
            pl.BlockSpec((1, d), lambda i, f: (0, 0)),
            pl.BlockSpec((d, tf), lambda i, f: (0, f)),
            pl.BlockSpec((d, tf), lambda i, f: (0, f)),
            pl.BlockSpec((tf, d), lambda i, f: (f, 0)),
        ],
        out_specs=pl.BlockSpec((tm, d), lambda i, f: (i, 0)),
        out_shape=jax.ShapeDtypeStruct((n, d), F32),
        scratch_shapes=[pltpu.VMEM((tm, d), BF16), pltpu.VMEM((tm, 1), F32)],
        compiler_params=pltpu.CompilerParams(
            dimension_semantics=("arbitrary", "arbitrary"), vmem_limit_bytes=VMEM_LIMIT_BYTES),
        name="ffn",
    )(h, g_pre, g_post, w_gate, w_up, w_down)


def kernel(x, positions, pre_mix_norm, post_mix_norm, w_in, pool_w, pool_scale,
           lam_q1, lam_k1, lam_q2, lam_k2, subln_w, w_out,
           pre_ffn_norm, post_ffn_norm, w_gate, w_up, w_down):
    batch, seq, d_model = x.shape
    depth = w_in.shape[0]
    pool_width = pool_scale.shape[1]
    attn_width = w_out.shape[1] - pool_width
    qk_width = (w_in.shape[2] - pool_width - attn_width) // 2
    n_heads = attn_width // DIFF_V_DIM
    assert qk_width == n_heads * 2 * DIFF_HEAD_DIM
    assert seq % TQ == 0 and seq % TM_IN == 0 and (batch * seq) % TM_FFN == 0
    assert TM_IN == TK

    n = batch * seq
    h = x.reshape(n, d_model)
    pos2 = positions.reshape(n, 1)
    lane_dim = jnp.arange(LANES) % DIFF_HEAD_DIM
    inv_freq = ROPE_THETA ** (-(2 * (lane_dim % ROT_HALF)).astype(F32) / ROT_DIM)
    inv_freq_lane = jnp.where(lane_dim < ROT_DIM, inv_freq, 0.0).reshape(1, LANES)

    for l in range(depth):
        lambda_init = _lambda_init(l)
        pool_out, q, k, vt = _in_proj(
            h, pos2, inv_freq_lane, pre_mix_norm[l].reshape(1, -1), w_in[l], pool_w[l],
            pool_scale[l].reshape(1, -1),
            seq=seq, pool_width=pool_width, qk_width=qk_width, attn_width=attn_width)
        lam_vecs = jnp.stack([lam_q1[l], lam_k1[l], lam_q2[l], lam_k2[l]]).astype(F32)
        attn_out, (wg_bf, wu_bf, wd_bf) = _attention(
            q, k, vt, lam_vecs, subln_w[l].reshape(1, -1), (w_gate[l], w_up[l], w_down[l]),
            batch=batch, seq=seq, n_heads=n_heads, lambda_init=lambda_init)
        h = _out_proj(pool_out, attn_out, h, w_out[l], post_mix_norm[l].reshape(1, -1))
        h = _ffn(h, pre_ffn_norm[l].reshape(1, -1), post_ffn_norm[l].reshape(1, -1), wg_bf, wu_bf, wd_bf)
    return h.reshape(batch, seq, d_model)
```

```python
import functools
import math

import jax
import jax.numpy as jnp
from jax import lax
from jax.experimental import pallas as pl
from jax.experimental.pallas import tpu as pltpu

F32 = jnp.float32
BF16 = jnp.bfloat16

POOL_WINDOWS = (2, 4, 8, 16)
DIFF_HEAD_DIM = 64
DIFF_V_DIM = 2 * DIFF_HEAD_DIM
ROPE_THETA = 500000.0
ROT_DIM = DIFF_HEAD_DIM // 4
ROT_HALF = ROT_DIM // 2
NORM_EPS = 1e-6
NEG_INF = -1e30
LOG2_E = math.log2(math.e)

LANES = 128
POOL_HALO = 32
VMEM_LIMIT_BYTES = 56 * 1024 * 1024

TM_IN = 512
TQ = 1024
TK = 512
ATTN_CHUNK = 256
ATTN_UNROLL = 6
TM_OUT = 512
TM_FFN = 1024
TF_FFN = 512
N_CHUNK = 512
NORM_ROWS = 64
IN_CAST_STEPS = 8
OUT_CAST_STEPS = 4


def _lambda_init(layer_idx):
    return 0.8 - 0.6 * math.exp(-0.3 * layer_idx)


def _rms_norm(xf, g):
    ms = jnp.mean(xf * xf, axis=-1, keepdims=True)
    return xf * lax.rsqrt(ms + NORM_EPS) * g


def _rms_norm_rows(src_ref, g_ref, scale_ref, emit, unroll):
    n = src_ref.shape[0]

    def scale_body(c, carry):
        rows = pl.ds(pl.multiple_of(c * NORM_ROWS, NORM_ROWS), NORM_ROWS)
        x = src_ref[rows, :]
        scale_ref[rows, :] = lax.rsqrt(jnp.mean(x * x, axis=-1, keepdims=True) + NORM_EPS)
        return carry

    lax.fori_loop(0, n // NORM_ROWS, scale_body, 0, unroll=unroll)

    def apply_body(c, carry):
        rows = pl.ds(pl.multiple_of(c * NORM_ROWS, NORM_ROWS), NORM_ROWS)
        emit(rows, src_ref[rows, :] * scale_ref[rows, :] * g_ref[...])
        return carry

    lax.fori_loop(0, n // NORM_ROWS, apply_body, 0, unroll=unroll)


def _in_proj_kernel(x_ref, pos_ref, invf_ref, g_ref, w_ref, pw_ref, ps_ref,
                    pool_ref, q_ref, k_ref, vt_ref, w_bf_ref, wvt_ref, trig_ref, hn_ref, u_ref, carry_ref, nscale_ref,
                    *, seq, pool_width, qk_width):
    tm = x_ref.shape[0]
    step = pl.program_id(0)
    rows_per_cast = w_ref.shape[0]
    n_cast = w_bf_ref.shape[0] // rows_per_cast
    n_direct = w_bf_ref.shape[1]
    tiles_per_cast = pos_ref.shape[0] // tm
    lane = lax.broadcasted_iota(jnp.int32, (1, LANES), 1) % DIFF_HEAD_DIM

    @pl.when(step < n_cast)
    def _():
        r0 = pl.multiple_of(step * rows_per_cast, rows_per_cast)
        w_bf_ref[pl.ds(r0, rows_per_cast), :] = w_ref[:, :n_direct].astype(BF16)
        for t in range(tiles_per_cast):
            ang = pos_ref[t * tm:(t + 1) * tm, :].astype(F32) * invf_ref[...]
            trig_ref[step * tiles_per_cast + t] = jnp.where(
                lane < ROT_HALF, jnp.cos(ang), jnp.where(lane < ROT_DIM, jnp.sin(ang), 0.0))

    for t in range(n_cast):
        @pl.when(step == t)
        def _():
            wvt_ref[:, t * rows_per_cast:(t + 1) * rows_per_cast] = w_ref[:, n_direct:].T.astype(BF16)

    @pl.when(step >= n_cast)
    def _():
        tiles_per_seq = seq // tm
        ti = (step - n_cast) % tiles_per_seq

        hn_ref[...] = _rms_norm(x_ref[...], g_ref[...]).astype(BF16)

        @pl.when(ti == 0)
        def _():
            carry_ref[...] = jnp.zeros_like(carry_ref)

        trig = trig_ref[step - n_cast]
        coef_self = jnp.where(lane < ROT_HALF, trig,
                              jnp.where(lane < ROT_DIM, pltpu.roll(trig, ROT_HALF, 1), 1.0))
        coef_lo = jnp.where((lane >= ROT_HALF) & (lane < ROT_DIM), trig, 0.0)
        coef_hi = jnp.where(lane < ROT_HALF, -pltpu.roll(trig, LANES - ROT_HALF, 1), 0.0)

        def rope_chunk(out_ref, chunk, c, scale):
            t = jnp.dot(hn_ref[...], w_bf_ref[:, chunk * N_CHUNK:(chunk + 1) * N_CHUNK],
                        preferred_element_type=F32)
            for h in range(N_CHUNK // LANES):
                th = t[:, h * LANES:(h + 1) * LANES]
                r = (th * coef_self + pltpu.roll(th, ROT_HALF, 1) * coef_lo
                     + pltpu.roll(th, LANES - ROT_HALF, 1) * coef_hi)
                if scale != 1.0:
                    r = r * scale
                out_ref[:, c + h * LANES:c + (h + 1) * LANES] = r.astype(BF16)

        gdim = pool_width // len(POOL_WINDOWS)
        t_in_seq = ti * tm + lax.broadcasted_iota(jnp.int32, (tm, 1), 0)

        def pool_group(g, w):
            c0 = g * gdim
            u = u_ref[:, c0:c0 + gdim]
            ext = jnp.concatenate([carry_ref[g], u], axis=0)
            carry_ref[g] = u[tm - POOL_HALO:, :]
            lvl, off, k = ext, 0, 1
            while k < w:
                new_off = min(off + 8, POOL_HALO)
                cur = lvl[new_off - off:, :]
                shifted = lvl[new_off - off - k: lvl.shape[0] - k, :]
                lvl, off, k = cur + shifted, new_off, 2 * k
            wsum = lvl[POOL_HALO - off:, :]
            cnt = jnp.minimum(t_in_seq + 1, w).astype(F32)
            pooled = (wsum / cnt - u).astype(BF16)
            mixed = jnp.dot(pooled, pw_ref[g].astype(BF16), preferred_element_type=F32)
            pool_ref[:, c0:c0 + gdim] = (mixed * ps_ref[:, c0:c0 + gdim]).astype(BF16)

        n_pool, n_qk = pool_width // N_CHUNK, qk_width // N_CHUNK
        for ci in range(n_pool):
            u_ref[:, ci * N_CHUNK:(ci + 1) * N_CHUNK] = jnp.dot(
                hn_ref[...], w_bf_ref[:, ci * N_CHUNK:(ci + 1) * N_CHUNK], preferred_element_type=F32)

        q_scale = DIFF_HEAD_DIM ** -0.5 * LOG2_E
        chunks = ([(q_ref, n_pool + ci, ci * N_CHUNK, q_scale) for ci in range(n_qk)]
                  + [(k_ref, n_pool + n_qk + ci, ci * N_CHUNK, 1.0) for ci in range(n_qk)])
        groups = list(enumerate(POOL_WINDOWS))
        for idx, args in enumerate(chunks):
            rope_chunk(*args)
            for g, w in groups[idx * len(groups) // len(chunks):(idx + 1) * len(groups) // len(chunks)]:
                pool_group(g, w)

        for c in range(0, vt_ref.shape[0], N_CHUNK):
            vt_ref[c:c + N_CHUNK, :] = lax.dot_general(
                wvt_ref[c:c + N_CHUNK, :], hn_ref[...], (((1,), (1,)), ((), ())),
                preferred_element_type=F32).astype(BF16)


def _in_proj(x2, pos2, inv_freq, g, w_in, pool_w, pool_scale, *, seq, pool_width, qk_width, attn_width):
    n, d = x2.shape
    tm = TM_IN
    n_groups = len(POOL_WINDOWS)
    gdim = pool_width // n_groups
    n_direct = pool_width + 2 * qk_width
    n_cast = IN_CAST_STEPS
    n_tiles = n // tm
    assert n_tiles % n_cast == 0
    tiles_per_cast = n_tiles // n_cast
    const = lambda s: (0, 0)
    row = lambda s: (jnp.maximum(s - n_cast, 0), 0)
    kern = functools.partial(_in_proj_kernel, seq=seq, pool_width=pool_width, qk_width=qk_width)
    return pl.pallas_call(
        kern,
        grid=(n_cast + n_tiles,),
        in_specs=[
            pl.BlockSpec((tm, d), row),
            pl.BlockSpec((tiles_per_cast * tm, 1), lambda s: (jnp.minimum(s, n_cast - 1), 0)),
            pl.BlockSpec((1, LANES), const),
            pl.BlockSpec((1, d), const),
            pl.BlockSpec((d // n_cast, w_in.shape[1]), lambda s: (jnp.minimum(s, n_cast - 1), 0)),
            pl.BlockSpec(pool_w.shape, lambda s: (0, 0, 0)),
            pl.BlockSpec((1, pool_width), const),
        ],
        out_specs=[
            pl.BlockSpec((tm, pool_width), row),
            pl.BlockSpec((tm, qk_width), row),
            pl.BlockSpec((tm, qk_width), row),
            pl.BlockSpec((None, attn_width, tm), lambda s: (jnp.maximum(s - n_cast, 0), 0, 0)),
        ],
        out_shape=[
            jax.ShapeDtypeStruct((n, pool_width), BF16),
            jax.ShapeDtypeStruct((n, qk_width), BF16),
            jax.ShapeDtypeStruct((n, qk_width), BF16),
            jax.ShapeDtypeStruct((n // tm, attn_width, tm), BF16),
        ],
        scratch_shapes=[
            pltpu.VMEM((d, n_direct), BF16),
            pltpu.VMEM((attn_width, d), BF16),
            pltpu.VMEM((n_tiles, tm, LANES), F32),
            pltpu.VMEM((tm, d), BF16),
            pltpu.VMEM((tm, pool_width), F32),
            pltpu.VMEM((n_groups, POOL_HALO, gdim), F32),
            pltpu.VMEM((tm, 1), F32),
        ],
        compiler_params=pltpu.CompilerParams(
            dimension_semantics=("arbitrary",), vmem_limit_bytes=VMEM_LIMIT_BYTES),
        name="in_proj",
    )(x2, pos2, inv_freq, g, w_in, pool_w, pool_scale)


def _attn_kernel(lam_ref, sw_ref, q_ref, k_ref, vt_ref, wg_ref, wu_ref, wd_ref,
                 o_ref, wg_bf_ref, wu_bf_ref, wd_bf_ref,
                 qc_ref, s0_ref, s1_ref, mx0_ref, mx1_ref, m_ref, l_ref, acc_ref, *, lambda_init):
    wg_bf_ref[...] = wg_ref[...].astype(BF16)
    wu_bf_ref[...] = wu_ref[...].astype(BF16)
    wd_bf_ref[...] = wd_ref[...].astype(BF16)

    seq = q_ref.shape[0]
    n_tiles = seq // TQ
    n_unmasked = n_tiles * (n_tiles - 1)
    assert n_tiles >= 2 and n_unmasked % ATTN_UNROLL == 0 and ATTN_UNROLL % 2 == 0
    s_slots, mx_slots = (s0_ref, s1_ref), (mx0_ref, mx1_ref)
    chunks = [slice(a, a + ATTN_CHUNK) for a in range(0, TQ, ATTN_CHUNK)]
    chunk_plan = {
        "full": [(ch, None) for ch in chunks],
        "lo": [(ch, ch.start if ch.start < TK else None) for ch in chunks],
        "hi": [(ch, ch.start - TK) for ch in chunks if ch.start >= TK],
    }

    def rows(i, size):
        return pl.ds(i * size if isinstance(i, int) else pl.multiple_of(i * size, size), size)

    def scores_chunk(qi, j, slot, c, ch, offset):
        s = lax.dot_general(k_ref[rows(j, TK), :], qc_ref[qi, c, ch, :], (((1,), (1,)), ((), ())),
                            preferred_element_type=F32)
        if offset is not None:
            kv = lax.broadcasted_iota(jnp.int32, s.shape, 0)
            r = lax.broadcasted_iota(jnp.int32, s.shape, 1)
            s = jnp.where(kv <= r + offset, s, NEG_INF)
        s_slots[slot][c, :, ch] = s
        mx_slots[slot][c, :, ch] = jnp.max(s, axis=0, keepdims=True)

    def exp_pv_chunk(qi, j, slot, c, ch):
        m_prev = m_ref[qi, c, :, ch]
        m_new = jnp.maximum(m_prev, mx_slots[slot][c, :, ch])
        alpha = jnp.exp2(m_prev - m_new)
        p = jnp.exp2(s_slots[slot][c, :, ch] - m_new)
        l_ref[qi, c, :, ch] = alpha * l_ref[qi, c, :, ch] + jnp.sum(p, axis=0, keepdims=True)
        pv = jnp.dot(vt_ref[j], p.astype(BF16), preferred_element_type=F32)
        acc_ref[qi, c, :, ch] = alpha * acc_ref[qi, c, :, ch] + pv
        m_ref[qi, c, :, ch] = m_new

    def stage(cur, cur_kind, slot, nxt, nxt_kind):
        todo_next = chunk_plan[nxt_kind] if nxt is not None else []
        todo_cur = [ch for ch, _ in chunk_plan[cur_kind]]
        for c in range(2):
            for i in range(max(len(todo_next), len(todo_cur))):
                if i < len(todo_next):
                    scores_chunk(nxt[0], nxt[1], 1 - slot, c, *todo_next[i])
                if i < len(todo_cur):
                    exp_pv_chunk(cur[0], cur[1], slot, c, todo_cur[i])

    def finalize(qi):
        lam_v = lam_ref[...]
        lam = (jnp.exp(jnp.sum(lam_v[0:1] * lam_v[1:2], keepdims=True))
               - jnp.exp(jnp.sum(lam_v[2:3] * lam_v[3:4], keepdims=True)) + lambda_init)
        attn_t = acc_ref[qi, 0] / l_ref[qi, 0] - lam * (acc_ref[qi, 1] / l_ref[qi, 1])
        ms = jnp.mean(attn_t * attn_t, axis=0, keepdims=True)
        y = (attn_t * lax.rsqrt(ms + NORM_EPS)).T
        o_ref[rows(qi, TQ), :] = (y * sw_ref[...] * (1.0 - lambda_init)).astype(BF16)

    lane = lax.broadcasted_iota(jnp.int32, (TQ, DIFF_V_DIM), 1)
    for qi in range(n_tiles):
        q = q_ref[qi * TQ:(qi + 1) * TQ, :]
        zero = jnp.zeros_like(q)
        qc_ref[qi, 0] = jnp.where(lane < DIFF_HEAD_DIM, q, zero)
        qc_ref[qi, 1] = jnp.where(lane >= DIFF_HEAD_DIM, q, zero)
    m_ref[...] = jnp.full_like(m_ref, NEG_INF)
    l_ref[...] = jnp.zeros_like(l_ref)
    acc_ref[...] = jnp.zeros_like(acc_ref)

    def unmasked_pair(n):
        qi = 1
        for t in range(2, n_tiles):
            qi = qi + jnp.where(n >= t * (t - 1), 1, 0)
        return qi, n - qi * (qi - 1)

    for c in range(2):
        for ch, off in chunk_plan["full"]:
            scores_chunk(1, 0, 0, c, ch, off)

    def unmasked_body(it, carry):
        for u in range(ATTN_UNROLL):
            n = it * ATTN_UNROLL + u
            stage(unmasked_pair(n), "full", u % 2, unmasked_pair(jnp.minimum(n + 1, n_unmasked - 1)), "full")
        return carry

    lax.fori_loop(0, n_unmasked // ATTN_UNROLL, unmasked_body, 0)

    def masked_stages(qi, nxt_qi):
        stage((qi, 2 * qi), "lo", 0, (qi, 2 * qi + 1), "hi")
        stage((qi, 2 * qi + 1), "hi", 1, (nxt_qi, 2 * nxt_qi), "lo")

    for c in range(2):
        for ch, off in chunk_plan["lo"]:
            scores_chunk(0, 0, 0, c, ch, off)
    masked_stages(0, 1)

    def masked_body(qi, carry):
        finalize(qi - 1)
        masked_stages(qi, jnp.minimum(qi + 1, n_tiles - 1))
        return carry

    lax.fori_loop(1, n_tiles, masked_body, 0)
    finalize(n_tiles - 1)


def _attention(q, k, vt, lam_vecs, subln_w, ffn_weights, *, batch, seq, n_heads, lambda_init):
    n, width = q.shape
    nk = seq // TK
    n_tiles = seq // TQ
    n_steps = batch * n_heads
    assert TQ == 2 * TK and seq % TQ == 0 and vt.shape == (n // TK, n_heads * DIFF_V_DIM, TK)
    kern = functools.partial(_attn_kernel, lambda_init=lambda_init)
    head = lambda b, h: (b, h)
    step_rows = lambda b, h: (b * n_heads + h, 0)
    w_specs = [pl.BlockSpec((w.shape[0] // n_steps, w.shape[1]), step_rows) for w in ffn_weights]
    outs = pl.pallas_call(
        kern,
        grid=(batch, n_heads),
        in_specs=[
            pl.BlockSpec(lam_vecs.shape, lambda b, h: (0, 0)),
            pl.BlockSpec((1, DIFF_V_DIM), lambda b, h: (0, 0)),
            pl.BlockSpec((seq, DIFF_V_DIM), head),
            pl.BlockSpec((seq, DIFF_V_DIM), head),
            pl.BlockSpec((nk, DIFF_V_DIM, TK), lambda b, h: (b, h, 0)),
        ] + w_specs,
        out_specs=[pl.BlockSpec((seq, DIFF_V_DIM), head)] + w_specs,
        out_shape=[jax.ShapeDtypeStruct((n, width), BF16)]
        + [jax.ShapeDtypeStruct(w.shape, BF16) for w in ffn_weights],
        scratch_shapes=[
            pltpu.VMEM((n_tiles, 2, TQ, DIFF_V_DIM), BF16),
            pltpu.VMEM((2, TK, TQ), F32),
            pltpu.VMEM((2, TK, TQ), F32),
            pltpu.VMEM((2, 1, TQ), F32),
            pltpu.VMEM((2, 1, TQ), F32),
            pltpu.VMEM((n_tiles, 2, 1, TQ), F32),
            pltpu.VMEM((n_tiles, 2, 1, TQ), F32),
            pltpu.VMEM((n_tiles, 2, DIFF_V_DIM, TQ), F32),
        ],
        compiler_params=pltpu.CompilerParams(
            dimension_semantics=("arbitrary", "arbitrary"), vmem_limit_bytes=VMEM_LIMIT_BYTES),
        name="diff_attention",
    )(lam_vecs, subln_w, q, k, vt, *ffn_weights)
    return outs[0], outs[1:]


def _out_proj_kernel(pool_ref, attn_ref, x_ref, w_ref, g_ref, h_ref, w_bf_ref, mix_ref, nscale_ref):
    step = pl.program_id(0)
    rows_per_cast = w_ref.shape[0]
    n_cast = w_bf_ref.shape[0] // rows_per_cast
    pw = pool_ref.shape[1]

    @pl.when(step < n_cast)
    def _():
        r0 = pl.multiple_of(step * rows_per_cast, rows_per_cast)
        w_bf_ref[pl.ds(r0, rows_per_cast), :] = w_ref[...].astype(BF16)

    @pl.when(step >= n_cast)
    def _():
        for c in range(0, mix_ref.shape[1], N_CHUNK):
            mix_ref[:, c:c + N_CHUNK] = (
                jnp.dot(pool_ref[...], w_bf_ref[:pw, c:c + N_CHUNK], preferred_element_type=F32)
                + jnp.dot(attn_ref[...], w_bf_ref[pw:, c:c + N_CHUNK], preferred_element_type=F32))

        def store_h(rows, y):
            h_ref[rows, :] = x_ref[rows, :] + y

        _rms_norm_rows(mix_ref, g_ref, nscale_ref, store_h, unroll=True)


def _out_proj(pool_out, attn_out, x2, w_out, g):
    n, d = x2.shape
    tm = TM_OUT
    n_cast = OUT_CAST_STEPS
    row = lambda s: (jnp.maximum(s - n_cast, 0), 0)
    const = lambda s: (0, 0)
    return pl.pallas_call(
        _out_proj_kernel,
        grid=(n_cast + n // tm,),
        in_specs=[
            pl.BlockSpec((tm, pool_out.shape[1]), row),
            pl.BlockSpec((tm, attn_out.shape[1]), row),
            pl.BlockSpec((tm, d), row),
            pl.BlockSpec((w_out.shape[0] // n_cast, w_out.shape[1]), lambda s: (jnp.minimum(s, n_cast - 1), 0)),
            pl.BlockSpec((1, d), const),
        ],
        out_specs=pl.BlockSpec((tm, d), row),
        out_shape=jax.ShapeDtypeStruct((n, d), F32),
        scratch_shapes=[
            pltpu.VMEM(w_out.shape, BF16),
            pltpu.VMEM((tm, d), F32),
            pltpu.VMEM((tm, 1), F32),
        ],
        compiler_params=pltpu.CompilerParams(
            dimension_semantics=("arbitrary",), vmem_limit_bytes=VMEM_LIMIT_BYTES),
        name="out_proj",
    )(pool_out, attn_out, x2, w_out, g)


def _ffn_kernel(h_ref, gpre_ref, gpost_ref, wg_ref, wu_ref, wd_ref, o_ref, hn_ref, nscale_ref):
    f = pl.program_id(1)

    @pl.when(f == 0)
    def _():
        def store_hn(rows, y):
            hn_ref[rows, :] = y.astype(BF16)
            o_ref[rows, :] = jnp.zeros_like(y)

        _rms_norm_rows(h_ref, gpre_ref, nscale_ref, store_hn, unroll=True)

    hn = hn_ref[...]
    gate = jnp.dot(hn, wg_ref[...], preferred_element_type=F32)
    up = jnp.dot(hn, wu_ref[...], preferred_element_type=F32)
    act = (gate * jax.nn.sigmoid(gate) * up).astype(BF16)
    for c in range(0, o_ref.shape[1], N_CHUNK):
        o_ref[:, c:c + N_CHUNK] += jnp.dot(act, wd_ref[:, c:c + N_CHUNK], preferred_element_type=F32)

    @pl.when(f == pl.num_programs(1) - 1)
    def _():
        def store_out(rows, y):
            o_ref[rows, :] = h_ref[rows, :] + y

        _rms_norm_rows(o_ref, gpost_ref, nscale_ref, store_out, unroll=True)


def _ffn(h, g_pre, g_post, w_gate, w_up, w_down):
    n, d = h.shape
    d_ff = w_gate.shape[1]
    tm, tf = TM_FFN, TF_FFN
    return pl.pallas_call(
        _ffn_kernel,
        grid=(n // tm, d_ff // tf),
        in_specs=[
            pl.BlockSpec((tm, d), lambda i, f: (i, 0)),
            pl.BlockSpec((1, d), lambda i, f: (0, 0)),
            pl.BlockSpec((1, d), lambda i, f: (0, 0)),
            pl.BlockSpec((d, tf), lambda i, f: (0, f)),
            pl.BlockSpec((d, tf), lambda i, f: (0, f)),
            pl.BlockSpec((tf, d), lambda i, f: (f, 0)),
        ],
        out_specs=pl.BlockSpec((tm, d), lambda i, f: (i, 0)),
        out_shape=jax.ShapeDtypeStruct((n, d), F32),
        scratch_shapes=[pltpu.VMEM((tm, d), BF16), pltpu.VMEM((tm, 1), F32)],
        compiler_params=pltpu.CompilerParams(
            dimension_semantics=("arbitrary", "arbitrary"), vmem_limit_bytes=VMEM_LIMIT_BYTES),
        name="ffn",
    )(h, g_pre, g_post, w_gate, w_up, w_down)


def kernel(x, positions, pre_mix_norm, post_mix_norm, w_in, pool_w, pool_scale,
           lam_q1, lam_k1, lam_q2, lam_k2, subln_w, w_out,
           pre_ffn_norm, post_ffn_norm, w_gate, w_up, w_down):
    batch, seq, d_model = x.shape
    depth = w_in.shape[0]
    pool_width = pool_scale.shape[1]
    attn_width = w_out.shape[1] - pool_width
    qk_width = (w_in.shape[2] - pool_width - attn_width) // 2
    n_heads = attn_width // DIFF_V_DIM
    assert qk_width == n_heads * 2 * DIFF_HEAD_DIM
    assert seq % TQ == 0 and seq % TM_IN == 0 and (batch * seq) % TM_FFN == 0
    assert TM_IN == TK

    n = batch * seq
    h = x.reshape(n, d_model)
    pos2 = positions.reshape(n, 1)
    lane_dim = jnp.arange(LANES) % DIFF_HEAD_DIM
    inv_freq = ROPE_THETA ** (-(2 * (lane_dim % ROT_HALF)).astype(F32) / ROT_DIM)
    inv_freq_lane = jnp.where(lane_dim < ROT_DIM, inv_freq, 0.0).reshape(1, LANES)

    for l in range(depth):
        lambda_init = _lambda_init(l)
        pool_out, q, k, vt = _in_proj(
            h, pos2, inv_freq_lane, pre_mix_norm[l].reshape(1, -1), w_in[l], pool_w[l],
            pool_scale[l].reshape(1, -1),
            seq=seq, pool_width=pool_width, qk_width=qk_width, attn_width=attn_width)
        lam_vecs = jnp.stack([lam_q1[l], lam_k1[l], lam_q2[l], lam_k2[l]]).astype(F32)
        attn_out, (wg_bf, wu_bf, wd_bf) = _attention(
            q, k, vt, lam_vecs, subln_w[l].reshape(1, -1), (w_gate[l], w_up[l], w_down[l]),
            batch=batch, seq=seq, n_heads=n_heads, lambda_init=lambda_init)
        h = _out_proj(pool_out, attn_out, h, w_out[l], post_mix_norm[l].reshape(1, -1))
        h = _ffn(h, pre_ffn_norm[l].reshape(1, -1), post_ffn_norm[l].reshape(1, -1), wg_bf, wu_bf, wd_bf)
    return h.reshape(batch, seq, d_model)
```

```python
import functools
import math

import jax
import jax.numpy as jnp
from jax import lax
from jax.experimental import pallas as pl
from jax.experimental.pallas import tpu as pltpu

F32 = jnp.float32
BF16 = jnp.bfloat16

POOL_WINDOWS = (2, 4, 8, 16)
DIFF_HEAD_DIM = 64
DIFF_V_DIM = 2 * DIFF_HEAD_DIM
ROPE_THETA = 500000.0
ROT_DIM = DIFF_HEAD_DIM // 4
ROT_HALF = ROT_DIM // 2
NORM_EPS = 1e-6
NEG_INF = -1e30
LOG2_E = math.log2(math.e)

LANES = 128
POOL_HALO = 32
VMEM_LIMIT_BYTES = 56 * 1024 * 1024

TM_IN = 512
TQ = 1024
TK = 512
ATTN_CHUNK = 256
TM_OUT = 512
TM_FFN = 1024
TF_FFN = 512
N_CHUNK = 512
NORM_ROWS = 64
IN_CAST_STEPS = 8
OUT_CAST_STEPS = 4


def _lambda_init(layer_idx):
    return 0.8 - 0.6 * math.exp(-0.3 * layer_idx)


def _rms_norm(xf, g):
    ms = jnp.mean(xf * xf, axis=-1, keepdims=True)
    return xf * lax.rsqrt(ms + NORM_EPS) * g


def _rms_norm_rows(src_ref, g_ref, scale_ref, emit, unroll):
    n = src_ref.shape[0]

    def scale_body(c, carry):
        rows = pl.ds(pl.multiple_of(c * NORM_ROWS, NORM_ROWS), NORM_ROWS)
        x = src_ref[rows, :]
        scale_ref[rows, :] = lax.rsqrt(jnp.mean(x * x, axis=-1, keepdims=True) + NORM_EPS)
        return carry

    lax.fori_loop(0, n // NORM_ROWS, scale_body, 0, unroll=unroll)

    def apply_body(c, carry):
        rows = pl.ds(pl.multiple_of(c * NORM_ROWS, NORM_ROWS), NORM_ROWS)
        emit(rows, src_ref[rows, :] * scale_ref[rows, :] * g_ref[...])
        return carry

    lax.fori_loop(0, n // NORM_ROWS, apply_body, 0, unroll=unroll)


def _in_proj_kernel(x_ref, pos_ref, invf_ref, g_ref, w_ref, pw_ref, ps_ref,
                    pool_ref, q_ref, k_ref, vt_ref, w_bf_ref, wvt_ref, trig_ref, hn_ref, u_ref, carry_ref, nscale_ref,
                    *, seq, pool_width, qk_width):
    tm = x_ref.shape[0]
    step = pl.program_id(0)
    rows_per_cast = w_ref.shape[0]
    n_cast = w_bf_ref.shape[0] // rows_per_cast
    n_direct = w_bf_ref.shape[1]
    tiles_per_cast = pos_ref.shape[0] // tm
    lane = lax.broadcasted_iota(jnp.int32, (1, LANES), 1) % DIFF_HEAD_DIM

    @pl.when(step < n_cast)
    def _():
        r0 = pl.multiple_of(step * rows_per_cast, rows_per_cast)
        w_bf_ref[pl.ds(r0, rows_per_cast), :] = w_ref[:, :n_direct].astype(BF16)
        for t in range(tiles_per_cast):
            ang = pos_ref[t * tm:(t + 1) * tm, :].astype(F32) * invf_ref[...]
            trig_ref[step * tiles_per_cast + t] = jnp.where(
                lane < ROT_HALF, jnp.cos(ang), jnp.where(lane < ROT_DIM, jnp.sin(ang), 0.0))

    for t in range(n_cast):
        @pl.when(step == t)
        def _():
            wvt_ref[:, t * rows_per_cast:(t + 1) * rows_per_cast] = w_ref[:, n_direct:].T.astype(BF16)

    @pl.when(step >= n_cast)
    def _():
        tiles_per_seq = seq // tm
        ti = (step - n_cast) % tiles_per_seq

        hn_ref[...] = _rms_norm(x_ref[...], g_ref[...]).astype(BF16)

        @pl.when(ti == 0)
        def _():
            carry_ref[...] = jnp.zeros_like(carry_ref)

        trig = trig_ref[step - n_cast]
        coef_self = jnp.where(lane < ROT_HALF, trig,
                              jnp.where(lane < ROT_DIM, pltpu.roll(trig, ROT_HALF, 1), 1.0))
        coef_lo = jnp.where((lane >= ROT_HALF) & (lane < ROT_DIM), trig, 0.0)
        coef_hi = jnp.where(lane < ROT_HALF, -pltpu.roll(trig, LANES - ROT_HALF, 1), 0.0)

        def rope_chunk(out_ref, chunk, c, scale):
            t = jnp.dot(hn_ref[...], w_bf_ref[:, chunk * N_CHUNK:(chunk + 1) * N_CHUNK],
                        preferred_element_type=F32)
            for h in range(N_CHUNK // LANES):
                th = t[:, h * LANES:(h + 1) * LANES]
                r = (th * coef_self + pltpu.roll(th, ROT_HALF, 1) * coef_lo
                     + pltpu.roll(th, LANES - ROT_HALF, 1) * coef_hi)
                if scale != 1.0:
                    r = r * scale
                out_ref[:, c + h * LANES:c + (h + 1) * LANES] = r.astype(BF16)

        gdim = pool_width // len(POOL_WINDOWS)
        t_in_seq = ti * tm + lax.broadcasted_iota(jnp.int32, (tm, 1), 0)

        def pool_group(g, w):
            c0 = g * gdim
            u = u_ref[:, c0:c0 + gdim]
            ext = jnp.concatenate([carry_ref[g], u], axis=0)
            carry_ref[g] = u[tm - POOL_HALO:, :]
            lvl, off, k = ext, 0, 1
            while k < w:
                new_off = min(off + 8, POOL_HALO)
                cur = lvl[new_off - off:, :]
                shifted = lvl[new_off - off - k: lvl.shape[0] - k, :]
                lvl, off, k = cur + shifted, new_off, 2 * k
            wsum = lvl[POOL_HALO - off:, :]
            cnt = jnp.minimum(t_in_seq + 1, w).astype(F32)
            pooled = (wsum / cnt - u).astype(BF16)
            mixed = jnp.dot(pooled, pw_ref[g].astype(BF16), preferred_element_type=F32)
            pool_ref[:, c0:c0 + gdim] = (mixed * ps_ref[:, c0:c0 + gdim]).astype(BF16)

        n_pool, n_qk = pool_width // N_CHUNK, qk_width // N_CHUNK
        for ci in range(n_pool):
            u_ref[:, ci * N_CHUNK:(ci + 1) * N_CHUNK] = jnp.dot(
                hn_ref[...], w_bf_ref[:, ci * N_CHUNK:(ci + 1) * N_CHUNK], preferred_element_type=F32)

        q_scale = DIFF_HEAD_DIM ** -0.5 * LOG2_E
        chunks = ([(q_ref, n_pool + ci, ci * N_CHUNK, q_scale) for ci in range(n_qk)]
                  + [(k_ref, n_pool + n_qk + ci, ci * N_CHUNK, 1.0) for ci in range(n_qk)])
        groups = list(enumerate(POOL_WINDOWS))
        for idx, args in enumerate(chunks):
            rope_chunk(*args)
            for g, w in groups[idx * len(groups) // len(chunks):(idx + 1) * len(groups) // len(chunks)]:
                pool_group(g, w)

        for c in range(0, vt_ref.shape[0], N_CHUNK):
            vt_ref[c:c + N_CHUNK, :] = lax.dot_general(
                wvt_ref[c:c + N_CHUNK, :], hn_ref[...], (((1,), (1,)), ((), ())),
                preferred_element_type=F32).astype(BF16)


def _in_proj(x2, pos2, inv_freq, g, w_in, pool_w, pool_scale, *, seq, pool_width, qk_width, attn_width):
    n, d = x2.shape
    tm = TM_IN
    n_groups = len(POOL_WINDOWS)
    gdim = pool_width // n_groups
    n_direct = pool_width + 2 * qk_width
    n_cast = IN_CAST_STEPS
    n_tiles = n // tm
    assert n_tiles % n_cast == 0
    tiles_per_cast = n_tiles // n_cast
    const = lambda s: (0, 0)
    row = lambda s: (jnp.maximum(s - n_cast, 0), 0)
    kern = functools.partial(_in_proj_kernel, seq=seq, pool_width=pool_width, qk_width=qk_width)
    return pl.pallas_call(
        kern,
        grid=(n_cast + n_tiles,),
        in_specs=[
            pl.BlockSpec((tm, d), row),
            pl.BlockSpec((tiles_per_cast * tm, 1), lambda s: (jnp.minimum(s, n_cast - 1), 0)),
            pl.BlockSpec((1, LANES), const),
            pl.BlockSpec((1, d), const),
            pl.BlockSpec((d // n_cast, w_in.shape[1]), lambda s: (jnp.minimum(s, n_cast - 1), 0)),
            pl.BlockSpec(pool_w.shape, lambda s: (0, 0, 0)),
            pl.BlockSpec((1, pool_width), const),
        ],
        out_specs=[
            pl.BlockSpec((tm, pool_width), row),
            pl.BlockSpec((tm, qk_width), row),
            pl.BlockSpec((tm, qk_width), row),
            pl.BlockSpec((None, attn_width, tm), lambda s: (jnp.maximum(s - n_cast, 0), 0, 0)),
        ],
        out_shape=[
            jax.ShapeDtypeStruct((n, pool_width), BF16),
            jax.ShapeDtypeStruct((n, qk_width), BF16),
            jax.ShapeDtypeStruct((n, qk_width), BF16),
            jax.ShapeDtypeStruct((n // tm, attn_width, tm), BF16),
        ],
        scratch_shapes=[
            pltpu.VMEM((d, n_direct), BF16),
            pltpu.VMEM((attn_width, d), BF16),
            pltpu.VMEM((n_tiles, tm, LANES), F32),
            pltpu.VMEM((tm, d), BF16),
            pltpu.VMEM((tm, pool_width), F32),
            pltpu.VMEM((n_groups, POOL_HALO, gdim), F32),
            pltpu.VMEM((tm, 1), F32),
        ],
        compiler_params=pltpu.CompilerParams(
            dimension_semantics=("arbitrary",), vmem_limit_bytes=VMEM_LIMIT_BYTES),
        name="in_proj",
    )(x2, pos2, inv_freq, g, w_in, pool_w, pool_scale)


def _attn_kernel(lam_ref, sw_ref, q_ref, k_ref, vt_ref, wg_ref, wu_ref, wd_ref,
                 o_ref, wg_bf_ref, wu_bf_ref, wd_bf_ref,
                 qc_ref, s0_ref, s1_ref, mx0_ref, mx1_ref, m_ref, l_ref, acc_ref, *, lambda_init):
    wg_bf_ref[...] = wg_ref[...].astype(BF16)
    wu_bf_ref[...] = wu_ref[...].astype(BF16)
    wd_bf_ref[...] = wd_ref[...].astype(BF16)

    seq = q_ref.shape[0]
    n_tiles = seq // TQ
    s_slots, mx_slots = (s0_ref, s1_ref), (mx0_ref, mx1_ref)
    chunks = [slice(a, a + ATTN_CHUNK) for a in range(0, TQ, ATTN_CHUNK)]
    chunk_plan = {
        "full": [(ch, None) for ch in chunks],
        "lo": [(ch, ch.start if ch.start < TK else None) for ch in chunks],
        "hi": [(ch, ch.start - TK) for ch in chunks if ch.start >= TK],
    }

    def kind(qi, j):
        return "full" if j < 2 * qi else ("lo" if j == 2 * qi else "hi")

    lane = lax.broadcasted_iota(jnp.int32, (TQ, DIFF_V_DIM), 1)

    def load_q(qi):
        q = q_ref[qi * TQ:(qi + 1) * TQ, :]
        zero = jnp.zeros_like(q)
        qc_ref[qi % 2, 0] = jnp.where(lane < DIFF_HEAD_DIM, q, zero)
        qc_ref[qi % 2, 1] = jnp.where(lane >= DIFF_HEAD_DIM, q, zero)

    def scores_chunk(qi, j, c, ch, offset):
        s = lax.dot_general(k_ref[j * TK:(j + 1) * TK, :], qc_ref[qi % 2, c, ch, :], (((1,), (1,)), ((), ())),
                            preferred_element_type=F32)
        if offset is not None:
            kv = lax.broadcasted_iota(jnp.int32, s.shape, 0)
            r = lax.broadcasted_iota(jnp.int32, s.shape, 1)
            s = jnp.where(kv <= r + offset, s, NEG_INF)
        s_slots[j % 2][c, :, ch] = s
        mx_slots[j % 2][c, :, ch] = jnp.max(s, axis=0, keepdims=True)

    def exp_pv_chunk(j, c, ch):
        m_prev = m_ref[c, :, ch]
        m_new = jnp.maximum(m_prev, mx_slots[j % 2][c, :, ch])
        alpha = jnp.exp2(m_prev - m_new)
        p = jnp.exp2(s_slots[j % 2][c, :, ch] - m_new)
        l_ref[c, :, ch] = alpha * l_ref[c, :, ch] + jnp.sum(p, axis=0, keepdims=True)
        pv = jnp.dot(vt_ref[j], p.astype(BF16), preferred_element_type=F32)
        acc_ref[c, :, ch] = alpha * acc_ref[c, :, ch] + pv
        m_ref[c, :, ch] = m_new

    def stage(cur, nxt):
        todo_next = chunk_plan[kind(*nxt)] if nxt is not None else []
        todo_cur = [ch for ch, _ in chunk_plan[kind(*cur)]]
        for c in range(2):
            for i in range(max(len(todo_next), len(todo_cur))):
                if i < len(todo_next):
                    scores_chunk(nxt[0], nxt[1], c, *todo_next[i])
                if i < len(todo_cur):
                    exp_pv_chunk(cur[1], c, todo_cur[i])

    def finalize(qi):
        lam_v = lam_ref[...]
        lam = (jnp.exp(jnp.sum(lam_v[0:1] * lam_v[1:2], keepdims=True))
               - jnp.exp(jnp.sum(lam_v[2:3] * lam_v[3:4], keepdims=True)) + lambda_init)
        attn_t = acc_ref[0] / l_ref[0] - lam * (acc_ref[1] / l_ref[1])
        ms = jnp.mean(attn_t * attn_t, axis=0, keepdims=True)
        y = (attn_t * lax.rsqrt(ms + NORM_EPS)).T
        o_ref[qi * TQ:(qi + 1) * TQ, :] = (y * sw_ref[...] * (1.0 - lambda_init)).astype(BF16)

    stages = [(qi, j) for qi in range(n_tiles) for j in range(2 * qi + 2)]
    load_q(0)
    for c in range(2):
        for ch, off in chunk_plan[kind(0, 0)]:
            scores_chunk(0, 0, c, ch, off)
    for n, (qi, j) in enumerate(stages):
        nxt = stages[n + 1] if n + 1 < len(stages) else None
        if j == 0:
            m_ref[...] = jnp.full_like(m_ref, NEG_INF)
            l_ref[...] = jnp.zeros_like(l_ref)
            acc_ref[...] = jnp.zeros_like(acc_ref)
        if nxt is not None and nxt[1] == 0:
            load_q(nxt[0])
        stage((qi, j), nxt)
        if j == 2 * qi + 1:
            finalize(qi)


def _attention(q, k, vt, lam_vecs, subln_w, ffn_weights, *, batch, seq, n_heads, lambda_init):
    n, width = q.shape
    nk = seq // TK
    n_steps = batch * n_heads
    assert TQ == 2 * TK and seq % TQ == 0 and vt.shape == (n // TK, n_heads * DIFF_V_DIM, TK)
    kern = functools.partial(_attn_kernel, lambda_init=lambda_init)
    head = lambda b, h: (b, h)
    step_rows = lambda b, h: (b * n_heads + h, 0)
    w_specs = [pl.BlockSpec((w.shape[0] // n_steps, w.shape[1]), step_rows) for w in ffn_weights]
    outs = pl.pallas_call(
        kern,
        grid=(batch, n_heads),
        in_specs=[
            pl.BlockSpec(lam_vecs.shape, lambda b, h: (0, 0)),
            pl.BlockSpec((1, DIFF_V_DIM), lambda b, h: (0, 0)),
            pl.BlockSpec((seq, DIFF_V_DIM), head),
            pl.BlockSpec((seq, DIFF_V_DIM), head),
            pl.BlockSpec((nk, DIFF_V_DIM, TK), lambda b, h: (b, h, 0)),
        ] + w_specs,
        out_specs=[pl.BlockSpec((seq, DIFF_V_DIM), head)] + w_specs,
        out_shape=[jax.ShapeDtypeStruct((n, width), BF16)]
        + [jax.ShapeDtypeStruct(w.shape, BF16) for w in ffn_weights],
        scratch_shapes=[
            pltpu.VMEM((2, 2, TQ, DIFF_V_DIM), BF16),
            pltpu.VMEM((2, TK, TQ), F32),
            pltpu.VMEM((2, TK, TQ), F32),
            pltpu.VMEM((2, 1, TQ), F32),
            pltpu.VMEM((2, 1, TQ), F32),
            pltpu.VMEM((2, 1, TQ), F32),
            pltpu.VMEM((2, 1, TQ), F32),
            pltpu.VMEM((2, DIFF_V_DIM, TQ), F32),
        ],
        compiler_params=pltpu.CompilerParams(
            dimension_semantics=("arbitrary", "arbitrary"), vmem_limit_bytes=VMEM_LIMIT_BYTES),
        name="diff_attention",
    )(lam_vecs, subln_w, q, k, vt, *ffn_weights)
    return outs[0], outs[1:]


def _out_proj_kernel(pool_ref, attn_ref, x_ref, w_ref, g_ref, h_ref, w_bf_ref, mix_ref, nscale_ref):
    step = pl.program_id(0)
    rows_per_cast = w_ref.shape[0]
    n_cast = w_bf_ref.shape[0] // rows_per_cast
    pw = pool_ref.shape[1]

    @pl.when(step < n_cast)
    def _():
        r0 = pl.multiple_of(step * rows_per_cast, rows_per_cast)
        w_bf_ref[pl.ds(r0, rows_per_cast), :] = w_ref[...].astype(BF16)

    @pl.when(step >= n_cast)
    def _():
        for c in range(0, mix_ref.shape[1], N_CHUNK):
            mix_ref[:, c:c + N_CHUNK] = (
                jnp.dot(pool_ref[...], w_bf_ref[:pw, c:c + N_CHUNK], preferred_element_type=F32)
                + jnp.dot(attn_ref[...], w_bf_ref[pw:, c:c + N_CHUNK], preferred_element_type=F32))

        def store_h(rows, y):
            h_ref[rows, :] = x_ref[rows, :] + y

        _rms_norm_rows(mix_ref, g_ref, nscale_ref, store_h, unroll=True)


def _out_proj(pool_out, attn_out, x2, w_out, g):
    n, d = x2.shape
    tm = TM_OUT
    n_cast = OUT_CAST_STEPS
    row = lambda s: (jnp.maximum(s - n_cast, 0), 0)
    const = lambda s: (0, 0)
    return pl.pallas_call(
        _out_proj_kernel,
        grid=(n_cast + n // tm,),
        in_specs=[
            pl.BlockSpec((tm, pool_out.shape[1]), row),
            pl.BlockSpec((tm, attn_out.shape[1]), row),
            pl.BlockSpec((tm, d), row),
            pl.BlockSpec((w_out.shape[0] // n_cast, w_out.shape[1]), lambda s: (jnp.minimum(s, n_cast - 1), 0)),
            pl.BlockSpec((1, d), const),
        ],
        out_specs=pl.BlockSpec((tm, d), row),
        out_shape=jax.ShapeDtypeStruct((n, d), F32),
        scratch_shapes=[
            pltpu.VMEM(w_out.shape, BF16),
            pltpu.VMEM((tm, d), F32),
            pltpu.VMEM((tm, 1), F32),
        ],
        compiler_params=pltpu.CompilerParams(
            dimension_semantics=("arbitrary",), vmem_limit_bytes=VMEM_LIMIT_BYTES),
        name="out_proj",
    )(pool_out, attn_out, x2, w_out, g)


def _ffn_kernel(h_ref, gpre_ref, gpost_ref, wg_ref, wu_ref, wd_ref, o_ref, hn_ref, nscale_ref):
    f = pl.program_id(1)

    @pl.when(f == 0)
    def _():
        def store_hn(rows, y):
            hn_ref[rows, :] = y.astype(BF16)
            o_ref[rows, :] = jnp.zeros_like(y)

        _rms_norm_rows(h_ref, gpre_ref, nscale_ref, store_hn, unroll=True)

    hn = hn_ref[...]
    gate = jnp.dot(hn, wg_ref[...], preferred_element_type=F32)
    up = jnp.dot(hn, wu_ref[...], preferred_element_type=F32)
    act = (gate * jax.nn.sigmoid(gate) * up).astype(BF16)
    for c in range(0, o_ref.shape[1], N_CHUNK):
        o_ref[:, c:c + N_CHUNK] += jnp.dot(act, wd_ref[:, c:c + N_CHUNK], preferred_element_type=F32)

    @pl.when(f == pl.num_programs(1) - 1)
    def _():
        def store_out(rows, y):
            o_ref[rows, :] = h_ref[rows, :] + y

        _rms_norm_rows(o_ref, gpost_ref, nscale_ref, store_out, unroll=True)


def _ffn(h, g_pre, g_post, w_gate, w_up, w_down):
    n, d = h.shape
    d_ff = w_gate.shape[1]
    tm, tf = TM_FFN, TF_FFN
    return pl.pallas_call(
        _ffn_kernel,
        grid=(n // tm, d_ff // tf),
        in_specs=[
            pl.BlockSpec((tm, d), lambda i, f: (i, 0)),
            pl.BlockSpec((1, d), lambda i, f: (0, 0)),
            pl.BlockSpec((1, d), lambda i, f: (0, 0)),
            pl.BlockSpec((d, tf), lambda i, f: (0, f)),
            pl.BlockSpec((d, tf), lambda i, f: (0, f)),
            pl.BlockSpec((tf, d), lambda i, f: (f, 0)),
        ],
        out_specs=pl.BlockSpec((tm, d), lambda i, f: (i, 0)),
        out_shape=jax.ShapeDtypeStruct((n, d), F32),
        scratch_shapes=[pltpu.VMEM((tm, d), BF16), pltpu.VMEM((tm, 1), F32)],
        compiler_params=pltpu.CompilerParams(
            dimension_semantics=("arbitrary", "arbitrary"), vmem_limit_bytes=VMEM_LIMIT_BYTES),
        name="ffn",
    )(h, g_pre, g_post, w_gate, w_up, w_down)


def kernel(x, positions, pre_mix_norm, post_mix_norm, w_in, pool_w, pool_scale,
           lam_q1, lam_k1, lam_q2, lam_k2, subln_w, w_out,
           pre_ffn_norm, post_ffn_norm, w_gate, w_up, w_down):
    batch, seq, d_model = x.shape
    depth = w_in.shape[0]
    pool_width = pool_scale.shape[1]
    attn_width = w_out.shape[1] - pool_width
    qk_width = (w_in.shape[2] - pool_width - attn_width) // 2
    n_heads = attn_width // DIFF_V_DIM
    assert qk_width == n_heads * 2 * DIFF_HEAD_DIM
    assert seq % TQ == 0 and seq % TM_IN == 0 and (batch * seq) % TM_FFN == 0
    assert TM_IN == TK

    n = batch * seq
    h = x.reshape(n, d_model)
    pos2 = positions.reshape(n, 1)
    lane_dim = jnp.arange(LANES) % DIFF_HEAD_DIM
    inv_freq = ROPE_THETA ** (-(2 * (lane_dim % ROT_HALF)).astype(F32) / ROT_DIM)
    inv_freq_lane = jnp.where(lane_dim < ROT_DIM, inv_freq, 0.0).reshape(1, LANES)

    for l in range(depth):
        lambda_init = _lambda_init(l)
        pool_out, q, k, vt = _in_proj(
            h, pos2, inv_freq_lane, pre_mix_norm[l].reshape(1, -1), w_in[l], pool_w[l],
            pool_scale[l].reshape(1, -1),
            seq=seq, pool_width=pool_width, qk_width=qk_width, attn_width=attn_width)
        lam_vecs = jnp.stack([lam_q1[l], lam_k1[l], lam_q2[l], lam_k2[l]]).astype(F32)
        attn_out, (wg_bf, wu_bf, wd_bf) = _attention(
            q, k, vt, lam_vecs, subln_w[l].reshape(1, -1), (w_gate[l], w_up[l], w_down[l]),
            batch=batch, seq=seq, n_heads=n_heads, lambda_init=lambda_init)
        h = _out_proj(pool_out, attn_out, h, w_out[l], post_mix_norm[l].reshape(1, -1))
        h = _ffn(h, pre_ffn_norm[l].reshape(1, -1), post_ffn_norm[l].reshape(1, -1), wg_bf, wu_bf, wd_bf)
    return h.reshape(batch, seq, d_model)
```

```python
import functools
import math

import jax
import jax.numpy as jnp
from jax import lax
from jax.experimental import pallas as pl
from jax.experimental.pallas import tpu as pltpu

F32 = jnp.float32
BF16 = jnp.bfloat16

POOL_WINDOWS = (2, 4, 8, 16)
DIFF_HEAD_DIM = 64
DIFF_V_DIM = 2 * DIFF_HEAD_DIM
ROPE_THETA = 500000.0
ROT_DIM = DIFF_HEAD_DIM // 4
ROT_HALF = ROT_DIM // 2
NORM_EPS = 1e-6
NEG_INF = -1e30
LOG2_E = math.log2(math.e)

LANES = 128
POOL_HALO = 32
VMEM_LIMIT_BYTES = 56 * 1024 * 1024

TM_IN = 512
TQ = 1024
TK = 512
ATTN_CHUNK = 256
ATTN_UNROLL = 6
TM_OUT = 512
TM_FFN = 1024
TF_FFN = 512
N_CHUNK = 512
NORM_ROWS = 64
IN_CAST_STEPS = 8
OUT_CAST_STEPS = 4


def _lambda_init(layer_idx):
    return 0.8 - 0.6 * math.exp(-0.3 * layer_idx)


def _rms_norm(xf, g):
    ms = jnp.mean(xf * xf, axis=-1, keepdims=True)
    return xf * lax.rsqrt(ms + NORM_EPS) * g


def _rms_norm_rows(src_ref, g_ref, scale_ref, emit, unroll):
    n = src_ref.shape[0]

    def scale_body(c, carry):
        rows = pl.ds(pl.multiple_of(c * NORM_ROWS, NORM_ROWS), NORM_ROWS)
        x = src_ref[rows, :]
        scale_ref[rows, :] = lax.rsqrt(jnp.mean(x * x, axis=-1, keepdims=True) + NORM_EPS)
        return carry

    lax.fori_loop(0, n // NORM_ROWS, scale_body, 0, unroll=unroll)

    def apply_body(c, carry):
        rows = pl.ds(pl.multiple_of(c * NORM_ROWS, NORM_ROWS), NORM_ROWS)
        emit(rows, src_ref[rows, :] * scale_ref[rows, :] * g_ref[...])
        return carry

    lax.fori_loop(0, n // NORM_ROWS, apply_body, 0, unroll=unroll)


def _in_proj_kernel(x_ref, pos_ref, invf_ref, g_ref, w_ref, pw_ref, ps_ref,
                    pool_ref, q_ref, k_ref, vt_ref, w_bf_ref, wvt_ref, trig_ref, hn_ref, u_ref, carry_ref, nscale_ref,
                    *, seq, pool_width, qk_width):
    tm = x_ref.shape[0]
    step = pl.program_id(0)
    rows_per_cast = w_ref.shape[0]
    n_cast = w_bf_ref.shape[0] // rows_per_cast
    n_direct = w_bf_ref.shape[1]
    tiles_per_cast = pos_ref.shape[0] // tm
    lane = lax.broadcasted_iota(jnp.int32, (1, LANES), 1) % DIFF_HEAD_DIM

    @pl.when(step < n_cast)
    def _():
        r0 = pl.multiple_of(step * rows_per_cast, rows_per_cast)
        w_bf_ref[pl.ds(r0, rows_per_cast), :] = w_ref[:, :n_direct].astype(BF16)
        for t in range(tiles_per_cast):
            ang = pos_ref[t * tm:(t + 1) * tm, :].astype(F32) * invf_ref[...]
            trig_ref[step * tiles_per_cast + t] = jnp.where(
                lane < ROT_HALF, jnp.cos(ang), jnp.where(lane < ROT_DIM, jnp.sin(ang), 0.0))

    for t in range(n_cast):
        @pl.when(step == t)
        def _():
            wvt_ref[:, t * rows_per_cast:(t + 1) * rows_per_cast] = w_ref[:, n_direct:].T.astype(BF16)

    @pl.when(step >= n_cast)
    def _():
        tiles_per_seq = seq // tm
        ti = (step - n_cast) % tiles_per_seq

        hn_ref[...] = _rms_norm(x_ref[...], g_ref[...]).astype(BF16)

        @pl.when(ti == 0)
        def _():
            carry_ref[...] = jnp.zeros_like(carry_ref)

        trig = trig_ref[step - n_cast]
        coef_self = jnp.where(lane < ROT_HALF, trig,
                              jnp.where(lane < ROT_DIM, pltpu.roll(trig, ROT_HALF, 1), 1.0))
        coef_lo = jnp.where((lane >= ROT_HALF) & (lane < ROT_DIM), trig, 0.0)
        coef_hi = jnp.where(lane < ROT_HALF, -pltpu.roll(trig, LANES - ROT_HALF, 1), 0.0)

        def rope_chunk(out_ref, chunk, c, scale):
            t = jnp.dot(hn_ref[...], w_bf_ref[:, chunk * N_CHUNK:(chunk + 1) * N_CHUNK],
                        preferred_element_type=F32)
            for h in range(N_CHUNK // LANES):
                th = t[:, h * LANES:(h + 1) * LANES]
                r = (th * coef_self + pltpu.roll(th, ROT_HALF, 1) * coef_lo
                     + pltpu.roll(th, LANES - ROT_HALF, 1) * coef_hi)
                if scale != 1.0:
                    r = r * scale
                out_ref[:, c + h * LANES:c + (h + 1) * LANES] = r.astype(BF16)

        gdim = pool_width // len(POOL_WINDOWS)
        t_in_seq = ti * tm + lax.broadcasted_iota(jnp.int32, (tm, 1), 0)

        def pool_group(g, w):
            c0 = g * gdim
            u = u_ref[:, c0:c0 + gdim]
            ext = jnp.concatenate([carry_ref[g], u], axis=0)
            carry_ref[g] = u[tm - POOL_HALO:, :]
            lvl, off, k = ext, 0, 1
            while k < w:
                new_off = min(off + 8, POOL_HALO)
                cur = lvl[new_off - off:, :]
                shifted = lvl[new_off - off - k: lvl.shape[0] - k, :]
                lvl, off, k = cur + shifted, new_off, 2 * k
            wsum = lvl[POOL_HALO - off:, :]
            cnt = jnp.minimum(t_in_seq + 1, w).astype(F32)
            pooled = (wsum / cnt - u).astype(BF16)
            mixed = jnp.dot(pooled, pw_ref[g].astype(BF16), preferred_element_type=F32)
            pool_ref[:, c0:c0 + gdim] = (mixed * ps_ref[:, c0:c0 + gdim]).astype(BF16)

        n_pool, n_qk = pool_width // N_CHUNK, qk_width // N_CHUNK
        for ci in range(n_pool):
            u_ref[:, ci * N_CHUNK:(ci + 1) * N_CHUNK] = jnp.dot(
                hn_ref[...], w_bf_ref[:, ci * N_CHUNK:(ci + 1) * N_CHUNK], preferred_element_type=F32)

        q_scale = DIFF_HEAD_DIM ** -0.5 * LOG2_E
        chunks = ([(q_ref, n_pool + ci, ci * N_CHUNK, q_scale) for ci in range(n_qk)]
                  + [(k_ref, n_pool + n_qk + ci, ci * N_CHUNK, 1.0) for ci in range(n_qk)])
        groups = list(enumerate(POOL_WINDOWS))
        for idx, args in enumerate(chunks):
            rope_chunk(*args)
            for g, w in groups[idx * len(groups) // len(chunks):(idx + 1) * len(groups) // len(chunks)]:
                pool_group(g, w)

        for c in range(0, vt_ref.shape[0], N_CHUNK):
            vt_ref[c:c + N_CHUNK, :] = lax.dot_general(
                wvt_ref[c:c + N_CHUNK, :], hn_ref[...], (((1,), (1,)), ((), ())),
                preferred_element_type=F32).astype(BF16)


def _in_proj(x2, pos2, inv_freq, g, w_in, pool_w, pool_scale, *, seq, pool_width, qk_width, attn_width):
    n, d = x2.shape
    tm = TM_IN
    n_groups = len(POOL_WINDOWS)
    gdim = pool_width // n_groups
    n_direct = pool_width + 2 * qk_width
    n_cast = IN_CAST_STEPS
    n_tiles = n // tm
    assert n_tiles % n_cast == 0
    tiles_per_cast = n_tiles // n_cast
    const = lambda s: (0, 0)
    row = lambda s: (jnp.maximum(s - n_cast, 0), 0)
    kern = functools.partial(_in_proj_kernel, seq=seq, pool_width=pool_width, qk_width=qk_width)
    return pl.pallas_call(
        kern,
        grid=(n_cast + n_tiles,),
        in_specs=[
            pl.BlockSpec((tm, d), row),
            pl.BlockSpec((tiles_per_cast * tm, 1), lambda s: (jnp.minimum(s, n_cast - 1), 0)),
            pl.BlockSpec((1, LANES), const),
            pl.BlockSpec((1, d), const),
            pl.BlockSpec((d // n_cast, w_in.shape[1]), lambda s: (jnp.minimum(s, n_cast - 1), 0)),
            pl.BlockSpec(pool_w.shape, lambda s: (0, 0, 0)),
            pl.BlockSpec((1, pool_width), const),
        ],
        out_specs=[
            pl.BlockSpec((tm, pool_width), row),
            pl.BlockSpec((tm, qk_width), row),
            pl.BlockSpec((tm, qk_width), row),
            pl.BlockSpec((None, attn_width, tm), lambda s: (jnp.maximum(s - n_cast, 0), 0, 0)),
        ],
        out_shape=[
            jax.ShapeDtypeStruct((n, pool_width), BF16),
            jax.ShapeDtypeStruct((n, qk_width), BF16),
            jax.ShapeDtypeStruct((n, qk_width), BF16),
            jax.ShapeDtypeStruct((n // tm, attn_width, tm), BF16),
        ],
        scratch_shapes=[
            pltpu.VMEM((d, n_direct), BF16),
            pltpu.VMEM((attn_width, d), BF16),
            pltpu.VMEM((n_tiles, tm, LANES), F32),
            pltpu.VMEM((tm, d), BF16),
            pltpu.VMEM((tm, pool_width), F32),
            pltpu.VMEM((n_groups, POOL_HALO, gdim), F32),
            pltpu.VMEM((tm, 1), F32),
        ],
        compiler_params=pltpu.CompilerParams(
            dimension_semantics=("arbitrary",), vmem_limit_bytes=VMEM_LIMIT_BYTES),
        name="in_proj",
    )(x2, pos2, inv_freq, g, w_in, pool_w, pool_scale)


def _attn_kernel(lam_ref, sw_ref, q_ref, k_ref, vt_ref, wg_ref, wu_ref, wd_ref,
                 o_ref, wg_bf_ref, wu_bf_ref, wd_bf_ref,
                 qc_ref, s0_ref, s1_ref, mx0_ref, mx1_ref, m_ref, l_ref, acc_ref, *, lambda_init):
    wg_bf_ref[...] = wg_ref[...].astype(BF16)
    wu_bf_ref[...] = wu_ref[...].astype(BF16)
    wd_bf_ref[...] = wd_ref[...].astype(BF16)

    seq = q_ref.shape[0]
    n_tiles = seq // TQ
    n_unmasked = n_tiles * (n_tiles - 1)
    assert n_tiles >= 2 and n_unmasked % ATTN_UNROLL == 0 and ATTN_UNROLL % 2 == 0
    s_slots, mx_slots = (s0_ref, s1_ref), (mx0_ref, mx1_ref)
    chunks = [slice(a, a + ATTN_CHUNK) for a in range(0, TQ, ATTN_CHUNK)]
    chunk_plan = {
        "full": [(ch, None) for ch in chunks],
        "lo": [(ch, ch.start if ch.start < TK else None) for ch in chunks],
        "hi": [(ch, ch.start - TK) for ch in chunks if ch.start >= TK],
    }

    def rows(i, size):
        return pl.ds(i * size if isinstance(i, int) else pl.multiple_of(i * size, size), size)

    def scores_chunk(qi, j, slot, c, ch, offset):
        s = lax.dot_general(k_ref[rows(j, TK), :], qc_ref[qi, c, ch, :], (((1,), (1,)), ((), ())),
                            preferred_element_type=F32)
        if offset is not None:
            kv = lax.broadcasted_iota(jnp.int32, s.shape, 0)
            r = lax.broadcasted_iota(jnp.int32, s.shape, 1)
            s = jnp.where(kv <= r + offset, s, NEG_INF)
        s_slots[slot][c, :, ch] = s
        mx_slots[slot][c, :, ch] = jnp.max(s, axis=0, keepdims=True)

    def exp_pv_chunk(qi, j, slot, c, ch):
        m_prev = m_ref[qi, c, :, ch]
        m_new = jnp.maximum(m_prev, mx_slots[slot][c, :, ch])
        alpha = jnp.exp2(m_prev - m_new)
        p = jnp.exp2(s_slots[slot][c, :, ch] - m_new)
        l_ref[qi, c, :, ch] = alpha * l_ref[qi, c, :, ch] + jnp.sum(p, axis=0, keepdims=True)
        pv = jnp.dot(vt_ref[j], p.astype(BF16), preferred_element_type=F32)
        acc_ref[qi, c, :, ch] = alpha * acc_ref[qi, c, :, ch] + pv
        m_ref[qi, c, :, ch] = m_new

    def stage(cur, cur_kind, slot, nxt, nxt_kind):
        todo_next = chunk_plan[nxt_kind] if nxt is not None else []
        todo_cur = [ch for ch, _ in chunk_plan[cur_kind]]
        for c in range(2):
            for i in range(max(len(todo_next), len(todo_cur))):
                if i < len(todo_next):
                    scores_chunk(nxt[0], nxt[1], 1 - slot, c, *todo_next[i])
                if i < len(todo_cur):
                    exp_pv_chunk(cur[0], cur[1], slot, c, todo_cur[i])

    def finalize(qi):
        lam_v = lam_ref[...]
        lam = (jnp.exp(jnp.sum(lam_v[0:1] * lam_v[1:2], keepdims=True))
               - jnp.exp(jnp.sum(lam_v[2:3] * lam_v[3:4], keepdims=True)) + lambda_init)
        attn_t = acc_ref[qi, 0] / l_ref[qi, 0] - lam * (acc_ref[qi, 1] / l_ref[qi, 1])
        ms = jnp.mean(attn_t * attn_t, axis=0, keepdims=True)
        y = (attn_t * lax.rsqrt(ms + NORM_EPS)).T
        o_ref[rows(qi, TQ), :] = (y * sw_ref[...] * (1.0 - lambda_init)).astype(BF16)

    lane = lax.broadcasted_iota(jnp.int32, (TQ, DIFF_V_DIM), 1)
    for qi in range(n_tiles):
        q = q_ref[qi * TQ:(qi + 1) * TQ, :]
        zero = jnp.zeros_like(q)
        qc_ref[qi, 0] = jnp.where(lane < DIFF_HEAD_DIM, q, zero)
        qc_ref[qi, 1] = jnp.where(lane >= DIFF_HEAD_DIM, q, zero)
    m_ref[...] = jnp.full_like(m_ref, NEG_INF)
    l_ref[...] = jnp.zeros_like(l_ref)
    acc_ref[...] = jnp.zeros_like(acc_ref)

    def unmasked_pair(n):
        qi = 1
        for t in range(2, n_tiles):
            qi = qi + jnp.where(n >= t * (t - 1), 1, 0)
        return qi, n - qi * (qi - 1)

    for c in range(2):
        for ch, off in chunk_plan["full"]:
            scores_chunk(1, 0, 0, c, ch, off)

    def unmasked_body(it, carry):
        for u in range(ATTN_UNROLL):
            n = it * ATTN_UNROLL + u
            stage(unmasked_pair(n), "full", u % 2, unmasked_pair(jnp.minimum(n + 1, n_unmasked - 1)), "full")
        return carry

    lax.fori_loop(0, n_unmasked // ATTN_UNROLL, unmasked_body, 0)

    def masked_stages(qi, nxt_qi):
        stage((qi, 2 * qi), "lo", 0, (qi, 2 * qi + 1), "hi")
        stage((qi, 2 * qi + 1), "hi", 1, (nxt_qi, 2 * nxt_qi), "lo")

    for c in range(2):
        for ch, off in chunk_plan["lo"]:
            scores_chunk(0, 0, 0, c, ch, off)
    for qi in range(n_tiles):
        if qi > 0:
            finalize(qi - 1)
        masked_stages(qi, min(qi + 1, n_tiles - 1))
    finalize(n_tiles - 1)


def _attention(q, k, vt, lam_vecs, subln_w, ffn_weights, *, batch, seq, n_heads, lambda_init):
    n, width = q.shape
    nk = seq // TK
    n_tiles = seq // TQ
    n_steps = batch * n_heads
    assert TQ == 2 * TK and seq % TQ == 0 and vt.shape == (n // TK, n_heads * DIFF_V_DIM, TK)
    kern = functools.partial(_attn_kernel, lambda_init=lambda_init)
    head = lambda b, h: (b, h)
    step_rows = lambda b, h: (b * n_heads + h, 0)
    w_specs = [pl.BlockSpec((w.shape[0] // n_steps, w.shape[1]), step_rows) for w in ffn_weights]
    outs = pl.pallas_call(
        kern,
        grid=(batch, n_heads),
        in_specs=[
            pl.BlockSpec(lam_vecs.shape, lambda b, h: (0, 0)),
            pl.BlockSpec((1, DIFF_V_DIM), lambda b, h: (0, 0)),
            pl.BlockSpec((seq, DIFF_V_DIM), head),
            pl.BlockSpec((seq, DIFF_V_DIM), head),
            pl.BlockSpec((nk, DIFF_V_DIM, TK), lambda b, h: (b, h, 0)),
        ] + w_specs,
        out_specs=[pl.BlockSpec((seq, DIFF_V_DIM), head)] + w_specs,
        out_shape=[jax.ShapeDtypeStruct((n, width), BF16)]
        + [jax.ShapeDtypeStruct(w.shape, BF16) for w in ffn_weights],
        scratch_shapes=[
            pltpu.VMEM((n_tiles, 2, TQ, DIFF_V_DIM), BF16),
            pltpu.VMEM((2, TK, TQ), F32),
            pltpu.VMEM((2, TK, TQ), F32),
            pltpu.VMEM((2, 1, TQ), F32),
            pltpu.VMEM((2, 1, TQ), F32),
            pltpu.VMEM((n_tiles, 2, 1, TQ), F32),
            pltpu.VMEM((n_tiles, 2, 1, TQ), F32),
            pltpu.VMEM((n_tiles, 2, DIFF_V_DIM, TQ), F32),
        ],
        compiler_params=pltpu.CompilerParams(
            dimension_semantics=("arbitrary", "arbitrary"), vmem_limit_bytes=VMEM_LIMIT_BYTES),
        name="diff_attention",
    )(lam_vecs, subln_w, q, k, vt, *ffn_weights)
    return outs[0], outs[1:]


def _out_proj_kernel(pool_ref, attn_ref, x_ref, w_ref, g_ref, h_ref, w_bf_ref, mix_ref, nscale_ref):
    step = pl.program_id(0)
    rows_per_cast = w_ref.shape[0]
    n_cast = w_bf_ref.shape[0] // rows_per_cast
    pw = pool_ref.shape[1]

    @pl.when(step < n_cast)
    def _():
        r0 = pl.multiple_of(step * rows_per_cast, rows_per_cast)
        w_bf_ref[pl.ds(r0, rows_per_cast), :] = w_ref[...].astype(BF16)

    @pl.when(step >= n_cast)
    def _():
        for c in range(0, mix_ref.shape[1], N_CHUNK):
            mix_ref[:, c:c + N_CHUNK] = (
                jnp.dot(pool_ref[...], w_bf_ref[:pw, c:c + N_CHUNK], preferred_element_type=F32)
                + jnp.dot(attn_ref[...], w_bf_ref[pw:, c:c + N_CHUNK], preferred_element_type=F32))

        def store_h(rows, y):
            h_ref[rows, :] = x_ref[rows, :] + y

        _rms_norm_rows(mix_ref, g_ref, nscale_ref, store_h, unroll=True)


def _out_proj(pool_out, attn_out, x2, w_out, g):
    n, d = x2.shape
    tm = TM_OUT
    n_cast = OUT_CAST_STEPS
    row = lambda s: (jnp.maximum(s - n_cast, 0), 0)
    const = lambda s: (0, 0)
    return pl.pallas_call(
        _out_proj_kernel,
        grid=(n_cast + n // tm,),
        in_specs=[
            pl.BlockSpec((tm, pool_out.shape[1]), row),
            pl.BlockSpec((tm, attn_out.shape[1]), row),
            pl.BlockSpec((tm, d), row),
            pl.BlockSpec((w_out.shape[0] // n_cast, w_out.shape[1]), lambda s: (jnp.minimum(s, n_cast - 1), 0)),
            pl.BlockSpec((1, d), const),
        ],
        out_specs=pl.BlockSpec((tm, d), row),
        out_shape=jax.ShapeDtypeStruct((n, d), F32),
        scratch_shapes=[
            pltpu.VMEM(w_out.shape, BF16),
            pltpu.VMEM((tm, d), F32),
            pltpu.VMEM((tm, 1), F32),
        ],
        compiler_params=pltpu.CompilerParams(
            dimension_semantics=("arbitrary",), vmem_limit_bytes=VMEM_LIMIT_BYTES),
        name="out_proj",
    )(pool_out, attn_out, x2, w_out, g)


def _ffn_kernel(h_ref, gpre_ref, gpost_ref, wg_ref, wu_ref, wd_ref, o_ref, hn_ref, nscale_ref):
    f = pl.program_id(1)

    @pl.when(f == 0)
    def _():
        def store_hn(rows, y):
            hn_ref[rows, :] = y.astype(BF16)
            o_ref[rows, :] = jnp.zeros_like(y)

        _rms_norm_rows(h_ref, gpre_ref, nscale_ref, store_hn, unroll=True)

    hn = hn_ref[...]
    gate = jnp.dot(hn, wg_ref[...], preferred_element_type=F32)
    up = jnp.dot(hn, wu_ref[...], preferred_element_type=F32)
    act = (gate * jax.nn.sigmoid(gate) * up).astype(BF16)
    for c in range(0, o_ref.shape[1], N_CHUNK):
        o_ref[:, c:c + N_CHUNK] += jnp.dot(act, wd_ref[:, c:c + N_CHUNK], preferred_element_type=F32)

    @pl.when(f == pl.num_programs(1) - 1)
    def _():
        def store_out(rows, y):
            o_ref[rows, :] = h_ref[rows, :] + y

        _rms_norm_rows(o_ref, gpost_ref, nscale_ref, store_out, unroll=True)


def _ffn(h, g_pre, g_post, w_gate, w_up, w_down):
    n, d = h.shape
    d_ff = w_gate.shape[1]
    tm, tf = TM_FFN, TF_FFN
    return pl.pallas_call(
        _ffn_kernel,
        grid=(n // tm, d_ff // tf),
        in_specs=[
            pl.BlockSpec((tm, d), lambda i, f: (i, 0)),
            pl.BlockSpec((1, d), lambda i, f: (0, 0)),
            pl.BlockSpec((1, d), lambda i, f: (0, 0)),
            pl.BlockSpec((d, tf), lambda i, f: (0, f)),
            pl.BlockSpec((d, tf), lambda i, f: (0, f)),
            pl.BlockSpec((tf, d), lambda i, f: (f, 0)),
        ],
        out_specs=pl.BlockSpec((tm, d), lambda i, f: (i, 0)),
        out_shape=jax.ShapeDtypeStruct((n, d), F32),
        scratch_shapes=[pltpu.VMEM((tm, d), BF16), pltpu.VMEM((tm, 1), F32)],
        compiler_params=pltpu.CompilerParams(
            dimension_semantics=("arbitrary", "arbitrary"), vmem_limit_bytes=VMEM_LIMIT_BYTES),
        name="ffn",
    )(h, g_pre, g_post, w_gate, w_up, w_down)


def kernel(x, positions, pre_mix_norm, post_mix_norm, w_in, pool_w, pool_scale,
           lam_q1, lam_k1, lam_q2, lam_k2, subln_w, w_out,
           pre_ffn_norm, post_ffn_norm, w_gate, w_up, w_down):
    batch, seq, d_model = x.shape
    depth = w_in.shape[0]
    pool_width = pool_scale.shape[1]
    attn_width = w_out.shape[1] - pool_width
    qk_width = (w_in.shape[2] - pool_width - attn_width) // 2
    n_heads = attn_width // DIFF_V_DIM
    assert qk_width == n_heads * 2 * DIFF_HEAD_DIM
    assert seq % TQ == 0 and seq % TM_IN == 0 and (batch * seq) % TM_FFN == 0
    assert TM_IN == TK

    n = batch * seq
    h = x.reshape(n, d_model)
    pos2 = positions.reshape(n, 1)
    lane_dim = jnp.arange(LANES) % DIFF_HEAD_DIM
    inv_freq = ROPE_THETA ** (-(2 * (lane_dim % ROT_HALF)).astype(F32) / ROT_DIM)
    inv_freq_lane = jnp.where(lane_dim < ROT_DIM, inv_freq, 0.0).reshape(1, LANES)

    for l in range(depth):
        lambda_init = _lambda_init(l)
        pool_out, q, k, vt = _in_proj(
            h, pos2, inv_freq_lane, pre_mix_norm[l].reshape(1, -1), w_in[l], pool_w[l],
            pool_scale[l].reshape(1, -1),
            seq=seq, pool_width=pool_width, qk_width=qk_width, attn_width=attn_width)
        lam_vecs = jnp.stack([lam_q1[l], lam_k1[l], lam_q2[l], lam_k2[l]]).astype(F32)
        attn_out, (wg_bf, wu_bf, wd_bf) = _attention(
            q, k, vt, lam_vecs, subln_w[l].reshape(1, -1), (w_gate[l], w_up[l], w_down[l]),
            batch=batch, seq=seq, n_heads=n_heads, lambda_init=lambda_init)
        h = _out_proj(pool_out, attn_out, h, w_out[l], post_mix_norm[l].reshape(1, -1))
        h = _ffn(h, pre_ffn_norm[l].reshape(1, -1), post_ffn_norm[l].reshape(1, -1), wg_bf, wu_bf, wd_bf)
    return h.reshape(batch, seq, d_model)
```

```python
import functools
import math

import jax
import jax.numpy as jnp
from jax import lax
from jax.experimental import pallas as pl
from jax.experimental.pallas import tpu as pltpu

F32 = jnp.float32
BF16 = jnp.bfloat16

POOL_WINDOWS = (2, 4, 8, 16)
DIFF_HEAD_DIM = 64
DIFF_V_DIM = 2 * DIFF_HEAD_DIM
ROPE_THETA = 500000.0
ROT_DIM = DIFF_HEAD_DIM // 4
ROT_HALF = ROT_DIM // 2
NORM_EPS = 1e-6
NEG_INF = -1e30
LOG2_E = math.log2(math.e)

LANES = 128
POOL_HALO = 32
VMEM_LIMIT_BYTES = 56 * 1024 * 1024

TM_IN = 512
TQ = 1024
TK = 512
ATTN_CHUNK = 256
ATTN_UNROLL = 6
TM_OUT = 512
TM_FFN = 1024
TF_FFN = 512
N_CHUNK = 512
NORM_ROWS = 64
IN_CAST_STEPS = 8
OUT_CAST_STEPS = 4


def _lambda_init(layer_idx):
    return 0.8 - 0.6 * math.exp(-0.3 * layer_idx)


def _rms_norm(xf, g):
    ms = jnp.mean(xf * xf, axis=-1, keepdims=True)
    return xf * lax.rsqrt(ms + NORM_EPS) * g


def _rms_norm_rows(src_ref, g_ref, scale_ref, emit, unroll):
    n = src_ref.shape[0]

    def scale_body(c, carry):
        rows = pl.ds(pl.multiple_of(c * NORM_ROWS, NORM_ROWS), NORM_ROWS)
        x = src_ref[rows, :]
        scale_ref[rows, :] = lax.rsqrt(jnp.mean(x * x, axis=-1, keepdims=True) + NORM_EPS)
        return carry

    lax.fori_loop(0, n // NORM_ROWS, scale_body, 0, unroll=unroll)

    def apply_body(c, carry):
        rows = pl.ds(pl.multiple_of(c * NORM_ROWS, NORM_ROWS), NORM_ROWS)
        emit(rows, src_ref[rows, :] * scale_ref[rows, :] * g_ref[...])
        return carry

    lax.fori_loop(0, n // NORM_ROWS, apply_body, 0, unroll=unroll)


def _in_proj_kernel(x_ref, pos_ref, invf_ref, g_ref, w_ref, pw_ref, ps_ref,
                    pool_ref, q_ref, k_ref, vt_ref, w_bf_ref, wvt_ref, trig_ref, hn_ref, u_ref, carry_ref, nscale_ref,
                    *, seq, pool_width, qk_width):
    tm = x_ref.shape[0]
    step = pl.program_id(0)
    rows_per_cast = w_ref.shape[0]
    n_cast = w_bf_ref.shape[0] // rows_per_cast
    n_direct = w_bf_ref.shape[1]
    tiles_per_cast = pos_ref.shape[0] // tm
    lane = lax.broadcasted_iota(jnp.int32, (1, LANES), 1) % DIFF_HEAD_DIM

    @pl.when(step < n_cast)
    def _():
        r0 = pl.multiple_of(step * rows_per_cast, rows_per_cast)
        w_bf_ref[pl.ds(r0, rows_per_cast), :] = w_ref[:, :n_direct].astype(BF16)
        for t in range(tiles_per_cast):
            ang = pos_ref[t * tm:(t + 1) * tm, :].astype(F32) * invf_ref[...]
            trig_ref[step * tiles_per_cast + t] = jnp.where(
                lane < ROT_HALF, jnp.cos(ang), jnp.where(lane < ROT_DIM, jnp.sin(ang), 0.0))

    for t in range(n_cast):
        @pl.when(step == t)
        def _():
            wvt_ref[:, t * rows_per_cast:(t + 1) * rows_per_cast] = w_ref[:, n_direct:].T.astype(BF16)

    @pl.when(step >= n_cast)
    def _():
        tiles_per_seq = seq // tm
        ti = (step - n_cast) % tiles_per_seq

        hn_ref[...] = _rms_norm(x_ref[...], g_ref[...]).astype(BF16)

        @pl.when(ti == 0)
        def _():
            carry_ref[...] = jnp.zeros_like(carry_ref)

        trig = trig_ref[step - n_cast]
        coef_self = jnp.where(lane < ROT_HALF, trig,
                              jnp.where(lane < ROT_DIM, pltpu.roll(trig, ROT_HALF, 1), 1.0))
        coef_lo = jnp.where((lane >= ROT_HALF) & (lane < ROT_DIM), trig, 0.0)
        coef_hi = jnp.where(lane < ROT_HALF, -pltpu.roll(trig, LANES - ROT_HALF, 1), 0.0)

        def rope_chunk(out_ref, chunk, c, scale):
            t = jnp.dot(hn_ref[...], w_bf_ref[:, chunk * N_CHUNK:(chunk + 1) * N_CHUNK],
                        preferred_element_type=F32)
            for h in range(N_CHUNK // LANES):
                th = t[:, h * LANES:(h + 1) * LANES]
                r = (th * coef_self + pltpu.roll(th, ROT_HALF, 1) * coef_lo
                     + pltpu.roll(th, LANES - ROT_HALF, 1) * coef_hi)
                if scale != 1.0:
                    r = r * scale
                out_ref[:, c + h * LANES:c + (h + 1) * LANES] = r.astype(BF16)

        gdim = pool_width // len(POOL_WINDOWS)
        t_in_seq = ti * tm + lax.broadcasted_iota(jnp.int32, (tm, 1), 0)

        def pool_group(g, w):
            c0 = g * gdim
            u = u_ref[:, c0:c0 + gdim]
            ext = jnp.concatenate([carry_ref[g], u], axis=0)
            carry_ref[g] = u[tm - POOL_HALO:, :]
            lvl, off, k = ext, 0, 1
            while k < w:
                new_off = min(off + 8, POOL_HALO)
                cur = lvl[new_off - off:, :]
                shifted = lvl[new_off - off - k: lvl.shape[0] - k, :]
                lvl, off, k = cur + shifted, new_off, 2 * k
            wsum = lvl[POOL_HALO - off:, :]
            cnt = jnp.minimum(t_in_seq + 1, w).astype(F32)
            pooled = (wsum / cnt - u).astype(BF16)
            mixed = jnp.dot(pooled, pw_ref[g].astype(BF16), preferred_element_type=F32)
            pool_ref[:, c0:c0 + gdim] = (mixed * ps_ref[:, c0:c0 + gdim]).astype(BF16)

        n_pool, n_qk = pool_width // N_CHUNK, qk_width // N_CHUNK
        for ci in range(n_pool):
            u_ref[:, ci * N_CHUNK:(ci + 1) * N_CHUNK] = jnp.dot(
                hn_ref[...], w_bf_ref[:, ci * N_CHUNK:(ci + 1) * N_CHUNK], preferred_element_type=F32)

        q_scale = DIFF_HEAD_DIM ** -0.5 * LOG2_E
        chunks = ([(q_ref, n_pool + ci, ci * N_CHUNK, q_scale) for ci in range(n_qk)]
                  + [(k_ref, n_pool + n_qk + ci, ci * N_CHUNK, 1.0) for ci in range(n_qk)])
        groups = list(enumerate(POOL_WINDOWS))
        for idx, args in enumerate(chunks):
            rope_chunk(*args)
            for g, w in groups[idx * len(groups) // len(chunks):(idx + 1) * len(groups) // len(chunks)]:
                pool_group(g, w)

        for c in range(0, vt_ref.shape[0], N_CHUNK):
            vt_ref[c:c + N_CHUNK, :] = lax.dot_general(
                wvt_ref[c:c + N_CHUNK, :], hn_ref[...], (((1,), (1,)), ((), ())),
                preferred_element_type=F32).astype(BF16)


def _in_proj(x2, pos2, inv_freq, g, w_in, pool_w, pool_scale, *, seq, pool_width, qk_width, attn_width):
    n, d = x2.shape
    tm = TM_IN
    n_groups = len(POOL_WINDOWS)
    gdim = pool_width // n_groups
    n_direct = pool_width + 2 * qk_width
    n_cast = IN_CAST_STEPS
    n_tiles = n // tm
    assert n_tiles % n_cast == 0
    tiles_per_cast = n_tiles // n_cast
    const = lambda s: (0, 0)
    row = lambda s: (jnp.maximum(s - n_cast, 0), 0)
    kern = functools.partial(_in_proj_kernel, seq=seq, pool_width=pool_width, qk_width=qk_width)
    return pl.pallas_call(
        kern,
        grid=(n_cast + n_tiles,),
        in_specs=[
            pl.BlockSpec((tm, d), row),
            pl.BlockSpec((tiles_per_cast * tm, 1), lambda s: (jnp.minimum(s, n_cast - 1), 0)),
            pl.BlockSpec((1, LANES), const),
            pl.BlockSpec((1, d), const),
            pl.BlockSpec((d // n_cast, w_in.shape[1]), lambda s: (jnp.minimum(s, n_cast - 1), 0)),
            pl.BlockSpec(pool_w.shape, lambda s: (0, 0, 0)),
            pl.BlockSpec((1, pool_width), const),
        ],
        out_specs=[
            pl.BlockSpec((tm, pool_width), row),
            pl.BlockSpec((tm, qk_width), row),
            pl.BlockSpec((tm, qk_width), row),
            pl.BlockSpec((None, attn_width, tm), lambda s: (jnp.maximum(s - n_cast, 0), 0, 0)),
        ],
        out_shape=[
            jax.ShapeDtypeStruct((n, pool_width), BF16),
            jax.ShapeDtypeStruct((n, qk_width), BF16),
            jax.ShapeDtypeStruct((n, qk_width), BF16),
            jax.ShapeDtypeStruct((n // tm, attn_width, tm), BF16),
        ],
        scratch_shapes=[
            pltpu.VMEM((d, n_direct), BF16),
            pltpu.VMEM((attn_width, d), BF16),
            pltpu.VMEM((n_tiles, tm, LANES), F32),
            pltpu.VMEM((tm, d), BF16),
            pltpu.VMEM((tm, pool_width), F32),
            pltpu.VMEM((n_groups, POOL_HALO, gdim), F32),
            pltpu.VMEM((tm, 1), F32),
        ],
        compiler_params=pltpu.CompilerParams(
            dimension_semantics=("arbitrary",), vmem_limit_bytes=VMEM_LIMIT_BYTES),
        name="in_proj",
    )(x2, pos2, inv_freq, g, w_in, pool_w, pool_scale)


def _attn_kernel(lam_ref, sw_ref, q_ref, k_ref, vt_ref, wg_ref, wu_ref, wd_ref,
                 o_ref, wg_bf_ref, wu_bf_ref, wd_bf_ref,
                 qc_ref, s0_ref, s1_ref, mx0_ref, mx1_ref, m_ref, l_ref, acc_ref, *, lambda_init):
    wg_bf_ref[...] = wg_ref[...].astype(BF16)
    wu_bf_ref[...] = wu_ref[...].astype(BF16)
    wd_bf_ref[...] = wd_ref[...].astype(BF16)

    seq = q_ref.shape[0]
    n_tiles = seq // TQ
    n_unmasked = n_tiles * (n_tiles - 1)
    assert n_tiles >= 2 and n_unmasked % ATTN_UNROLL == 0 and ATTN_UNROLL % 2 == 0
    s_slots, mx_slots = (s0_ref, s1_ref), (mx0_ref, mx1_ref)
    chunks = [slice(a, a + ATTN_CHUNK) for a in range(0, TQ, ATTN_CHUNK)]
    chunk_plan = {
        "full": [(ch, None, TK) for ch in chunks],
        "lo": [(ch, ch.start, min(TK, ch.stop)) if ch.start < TK else (ch, None, TK) for ch in chunks],
        "hi": [(ch, ch.start - TK, min(TK, ch.stop - TK)) for ch in chunks if ch.start >= TK],
    }

    def rows(i, size):
        return pl.ds(i * size if isinstance(i, int) else pl.multiple_of(i * size, size), size)

    def key_rows(j, n_keys):
        start = j * TK if isinstance(j, int) else pl.multiple_of(j * TK, TK)
        return pl.ds(start, n_keys)

    def scores_chunk(qi, j, slot, c, ch, offset, n_keys):
        s = lax.dot_general(k_ref[key_rows(j, n_keys), :], qc_ref[qi, c, ch, :], (((1,), (1,)), ((), ())),
                            preferred_element_type=F32)
        if offset is not None:
            kv = lax.broadcasted_iota(jnp.int32, s.shape, 0)
            r = lax.broadcasted_iota(jnp.int32, s.shape, 1)
            s = jnp.where(kv <= r + offset, s, NEG_INF)
        s_slots[slot][c, 0:n_keys, ch] = s
        mx_slots[slot][c, :, ch] = jnp.max(s, axis=0, keepdims=True)

    def exp_pv_chunk(qi, j, slot, c, ch, n_keys):
        m_prev = m_ref[qi, c, :, ch]
        m_new = jnp.maximum(m_prev, mx_slots[slot][c, :, ch])
        alpha = jnp.exp2(m_prev - m_new)
        p = jnp.exp2(s_slots[slot][c, 0:n_keys, ch] - m_new)
        l_ref[qi, c, :, ch] = alpha * l_ref[qi, c, :, ch] + jnp.sum(p, axis=0, keepdims=True)
        pv = jnp.dot(vt_ref[j, :, 0:n_keys], p.astype(BF16), preferred_element_type=F32)
        acc_ref[qi, c, :, ch] = alpha * acc_ref[qi, c, :, ch] + pv
        m_ref[qi, c, :, ch] = m_new

    def stage(cur, cur_kind, slot, nxt, nxt_kind):
        todo_next = chunk_plan[nxt_kind] if nxt is not None else []
        todo_cur = [(ch, n_keys) for ch, _, n_keys in chunk_plan[cur_kind]]
        for c in range(2):
            for i in range(max(len(todo_next), len(todo_cur))):
                if i < len(todo_next):
                    scores_chunk(nxt[0], nxt[1], 1 - slot, c, *todo_next[i])
                if i < len(todo_cur):
                    exp_pv_chunk(cur[0], cur[1], slot, c, *todo_cur[i])

    def finalize(qi):
        lam_v = lam_ref[...]
        lam = (jnp.exp(jnp.sum(lam_v[0:1] * lam_v[1:2], keepdims=True))
               - jnp.exp(jnp.sum(lam_v[2:3] * lam_v[3:4], keepdims=True)) + lambda_init)
        attn_t = acc_ref[qi, 0] / l_ref[qi, 0] - lam * (acc_ref[qi, 1] / l_ref[qi, 1])
        ms = jnp.mean(attn_t * attn_t, axis=0, keepdims=True)
        y = (attn_t * lax.rsqrt(ms + NORM_EPS)).T
        o_ref[rows(qi, TQ), :] = (y * sw_ref[...] * (1.0 - lambda_init)).astype(BF16)

    lane = lax.broadcasted_iota(jnp.int32, (TQ, DIFF_V_DIM), 1)
    for qi in range(n_tiles):
        q = q_ref[qi * TQ:(qi + 1) * TQ, :]
        zero = jnp.zeros_like(q)
        qc_ref[qi, 0] = jnp.where(lane < DIFF_HEAD_DIM, q, zero)
        qc_ref[qi, 1] = jnp.where(lane >= DIFF_HEAD_DIM, q, zero)
    m_ref[...] = jnp.full_like(m_ref, NEG_INF)
    l_ref[...] = jnp.zeros_like(l_ref)
    acc_ref[...] = jnp.zeros_like(acc_ref)

    def unmasked_pair(n):
        qi = 1
        for t in range(2, n_tiles):
            qi = qi + jnp.where(n >= t * (t - 1), 1, 0)
        return qi, n - qi * (qi - 1)

    for c in range(2):
        for plan in chunk_plan["full"]:
            scores_chunk(1, 0, 0, c, *plan)

    def unmasked_body(it, carry):
        for u in range(ATTN_UNROLL):
            n = it * ATTN_UNROLL + u
            stage(unmasked_pair(n), "full", u % 2, unmasked_pair(jnp.minimum(n + 1, n_unmasked - 1)), "full")
        return carry

    lax.fori_loop(0, n_unmasked // ATTN_UNROLL, unmasked_body, 0)

    def masked_stages(qi, nxt_qi):
        stage((qi, 2 * qi), "lo", 0, (qi, 2 * qi + 1), "hi")
        stage((qi, 2 * qi + 1), "hi", 1, (nxt_qi, 2 * nxt_qi), "lo")

    for c in range(2):
        for plan in chunk_plan["lo"]:
            scores_chunk(0, 0, 0, c, *plan)
    for qi in range(n_tiles):
        if qi > 0:
            finalize(qi - 1)
        masked_stages(qi, min(qi + 1, n_tiles - 1))
    finalize(n_tiles - 1)


def _attention(q, k, vt, lam_vecs, subln_w, ffn_weights, *, batch, seq, n_heads, lambda_init):
    n, width = q.shape
    nk = seq // TK
    n_tiles = seq // TQ
    n_steps = batch * n_heads
    assert TQ == 2 * TK and seq % TQ == 0 and vt.shape == (n // TK, n_heads * DIFF_V_DIM, TK)
    kern = functools.partial(_attn_kernel, lambda_init=lambda_init)
    head = lambda b, h: (b, h)
    step_rows = lambda b, h: (b * n_heads + h, 0)
    w_specs = [pl.BlockSpec((w.shape[0] // n_steps, w.shape[1]), step_rows) for w in ffn_weights]
    outs = pl.pallas_call(
        kern,
        grid=(batch, n_heads),
        in_specs=[
            pl.BlockSpec(lam_vecs.shape, lambda b, h: (0, 0)),
            pl.BlockSpec((1, DIFF_V_DIM), lambda b, h: (0, 0)),
            pl.BlockSpec((seq, DIFF_V_DIM), head),
            pl.BlockSpec((seq, DIFF_V_DIM), head),
            pl.BlockSpec((nk, DIFF_V_DIM, TK), lambda b, h: (b, h, 0)),
        ] + w_specs,
        out_specs=[pl.BlockSpec((seq, DIFF_V_DIM), head)] + w_specs,
        out_shape=[jax.ShapeDtypeStruct((n, width), BF16)]
        + [jax.ShapeDtypeStruct(w.shape, BF16) for w in ffn_weights],
        scratch_shapes=[
            pltpu.VMEM((n_tiles, 2, TQ, DIFF_V_DIM), BF16),
            pltpu.VMEM((2, TK, TQ), F32),
            pltpu.VMEM((2, TK, TQ), F32),
            pltpu.VMEM((2, 1, TQ), F32),
            pltpu.VMEM((2, 1, TQ), F32),
            pltpu.VMEM((n_tiles, 2, 1, TQ), F32),
            pltpu.VMEM((n_tiles, 2, 1, TQ), F32),
            pltpu.VMEM((n_tiles, 2, DIFF_V_DIM, TQ), F32),
        ],
        compiler_params=pltpu.CompilerParams(
            dimension_semantics=("arbitrary", "arbitrary"), vmem_limit_bytes=VMEM_LIMIT_BYTES),
        name="diff_attention",
    )(lam_vecs, subln_w, q, k, vt, *ffn_weights)
    return outs[0], outs[1:]


def _out_proj_kernel(pool_ref, attn_ref, x_ref, w_ref, g_ref, h_ref, w_bf_ref, mix_ref, nscale_ref):
    step = pl.program_id(0)
    rows_per_cast = w_ref.shape[0]
    n_cast = w_bf_ref.shape[0] // rows_per_cast
    pw = pool_ref.shape[1]

    @pl.when(step < n_cast)
    def _():
        r0 = pl.multiple_of(step * rows_per_cast, rows_per_cast)
        w_bf_ref[pl.ds(r0, rows_per_cast), :] = w_ref[...].astype(BF16)

    @pl.when(step >= n_cast)
    def _():
        for c in range(0, mix_ref.shape[1], N_CHUNK):
            mix_ref[:, c:c + N_CHUNK] = (
                jnp.dot(pool_ref[...], w_bf_ref[:pw, c:c + N_CHUNK], preferred_element_type=F32)
                + jnp.dot(attn_ref[...], w_bf_ref[pw:, c:c + N_CHUNK], preferred_element_type=F32))

        def store_h(rows, y):
            h_ref[rows, :] = x_ref[rows, :] + y

        _rms_norm_rows(mix_ref, g_ref, nscale_ref, store_h, unroll=True)


def _out_proj(pool_out, attn_out, x2, w_out, g):
    n, d = x2.shape
    tm = TM_OUT
    n_cast = OUT_CAST_STEPS
    row = lambda s: (jnp.maximum(s - n_cast, 0), 0)
    const = lambda s: (0, 0)
    return pl.pallas_call(
        _out_proj_kernel,
        grid=(n_cast + n // tm,),
        in_specs=[
            pl.BlockSpec((tm, pool_out.shape[1]), row),
            pl.BlockSpec((tm, attn_out.shape[1]), row),
            pl.BlockSpec((tm, d), row),
            pl.BlockSpec((w_out.shape[0] // n_cast, w_out.shape[1]), lambda s: (jnp.minimum(s, n_cast - 1), 0)),
            pl.BlockSpec((1, d), const),
        ],
        out_specs=pl.BlockSpec((tm, d), row),
        out_shape=jax.ShapeDtypeStruct((n, d), F32),
        scratch_shapes=[
            pltpu.VMEM(w_out.shape, BF16),
            pltpu.VMEM((tm, d), F32),
            pltpu.VMEM((tm, 1), F32),
        ],
        compiler_params=pltpu.CompilerParams(
            dimension_semantics=("arbitrary",), vmem_limit_bytes=VMEM_LIMIT_BYTES),
        name="out_proj",
    )(pool_out, attn_out, x2, w_out, g)


def _ffn_kernel(h_ref, gpre_ref, gpost_ref, wg_ref, wu_ref, wd_ref, o_ref, hn_ref, nscale_ref):
    f = pl.program_id(1)

    @pl.when(f == 0)
    def _():
        def store_hn(rows, y):
            hn_ref[rows, :] = y.astype(BF16)
            o_ref[rows, :] = jnp.zeros_like(y)

        _rms_norm_rows(h_ref, gpre_ref, nscale_ref, store_hn, unroll=True)

    hn = hn_ref[...]
    gate = jnp.dot(hn, wg_ref[...], preferred_element_type=F32)
    up = jnp.dot(hn, wu_ref[...], preferred_element_type=F32)
    act = (gate * jax.nn.sigmoid(gate) * up).astype(BF16)
    for c in range(0, o_ref.shape[1], N_CHUNK):
        o_ref[:, c:c + N_CHUNK] += jnp.dot(act, wd_ref[:, c:c + N_CHUNK], preferred_element_type=F32)

    @pl.when(f == pl.num_programs(1) - 1)
    def _():
        def store_out(rows, y):
            o_ref[rows, :] = h_ref[rows, :] + y

        _rms_norm_rows(o_ref, gpost_ref, nscale_ref, store_out, unroll=True)


def _ffn(h, g_pre, g_post, w_gate, w_up, w_down):
    n, d = h.shape
    d_ff = w_gate.shape[1]
    tm, tf = TM_FFN, TF_FFN
    return pl.pallas_call(
        _ffn_kernel,
        grid=(n // tm, d_ff // tf),
        in_specs=[
            pl.BlockSpec((tm, d), lambda i, f: (i, 0)),
            pl.BlockSpec((1, d), lambda i, f: (0, 0)),
            pl.BlockSpec((1, d), lambda i, f: (0, 0)),
            pl.BlockSpec((d, tf), lambda i, f: (0, f)),
            pl.BlockSpec((d, tf), lambda i, f: (0, f)),
            pl.BlockSpec((tf, d), lambda i, f: (f, 0)),
        ],
        out_specs=pl.BlockSpec((tm, d), lambda i, f: (i, 0)),
        out_shape=jax.ShapeDtypeStruct((n, d), F32),
        scratch_shapes=[pltpu.VMEM((tm, d), BF16), pltpu.VMEM((tm, 1), F32)],
        compiler_params=pltpu.CompilerParams(
            dimension_semantics=("arbitrary", "arbitrary"), vmem_limit_bytes=VMEM_LIMIT_BYTES),
        name="ffn",
    )(h, g_pre, g_post, w_gate, w_up, w_down)


def kernel(x, positions, pre_mix_norm, post_mix_norm, w_in, pool_w, pool_scale,
           lam_q1, lam_k1, lam_q2, lam_k2, subln_w, w_out,
           pre_ffn_norm, post_ffn_norm, w_gate, w_up, w_down):
    batch, seq, d_model = x.shape
    depth = w_in.shape[0]
    pool_width = pool_scale.shape[1]
    attn_width = w_out.shape[1] - pool_width
    qk_width = (w_in.shape[2] - pool_width - attn_width) // 2
    n_heads = attn_width // DIFF_V_DIM
    assert qk_width == n_heads * 2 * DIFF_HEAD_DIM
    assert seq % TQ == 0 and seq % TM_IN == 0 and (batch * seq) % TM_FFN == 0
    assert TM_IN == TK

    n = batch * seq
    h = x.reshape(n, d_model)
    pos2 = positions.reshape(n, 1)
    lane_dim = jnp.arange(LANES) % DIFF_HEAD_DIM
    inv_freq = ROPE_THETA ** (-(2 * (lane_dim % ROT_HALF)).astype(F32) / ROT_DIM)
    inv_freq_lane = jnp.where(lane_dim < ROT_DIM, inv_freq, 0.0).reshape(1, LANES)

    for l in range(depth):
        lambda_init = _lambda_init(l)
        pool_out, q, k, vt = _in_proj(
            h, pos2, inv_freq_lane, pre_mix_norm[l].reshape(1, -1), w_in[l], pool_w[l],
            pool_scale[l].reshape(1, -1),
            seq=seq, pool_width=pool_width, qk_width=qk_width, attn_width=attn_width)
        lam_vecs = jnp.stack([lam_q1[l], lam_k1[l], lam_q2[l], lam_k2[l]]).astype(F32)
        attn_out, (wg_bf, wu_bf, wd_bf) = _attention(
            q, k, vt, lam_vecs, subln_w[l].reshape(1, -1), (w_gate[l], w_up[l], w_down[l]),
            batch=batch, seq=seq, n_heads=n_heads, lambda_init=lambda_init)
        h = _out_proj(pool_out, attn_out, h, w_out[l], post_mix_norm[l].reshape(1, -1))
        h = _ffn(h, pre_ffn_norm[l].reshape(1, -1), post_ffn_norm[l].reshape(1, -1), wg_bf, wu_bf, wd_bf)
    return h.reshape(batch, seq, d_model)
```

```python
import functools
import math

import jax
import jax.numpy as jnp
from jax import lax
from jax.experimental import pallas as pl
from jax.experimental.pallas import tpu as pltpu

F32 = jnp.float32
BF16 = jnp.bfloat16

POOL_WINDOWS = (2, 4, 8, 16)
DIFF_HEAD_DIM = 64
DIFF_V_DIM = 2 * DIFF_HEAD_DIM
ROPE_THETA = 500000.0
ROT_DIM = DIFF_HEAD_DIM // 4
ROT_HALF = ROT_DIM // 2
NORM_EPS = 1e-6
NEG_INF = -1e30
LOG2_E = math.log2(math.e)

LANES = 128
POOL_HALO = 32
VMEM_LIMIT_BYTES = 56 * 1024 * 1024

TM_IN = 512
TQ = 1024
TK = 512
ATTN_CHUNK = 256
ATTN_UNROLL = 6
TM_OUT = 512
TM_FFN = 1024
TF_FFN = 512
N_CHUNK = 512
NORM_ROWS = 64
IN_CAST_STEPS = 8
OUT_CAST_STEPS = 4


def _lambda_init(layer_idx):
    return 0.8 - 0.6 * math.exp(-0.3 * layer_idx)


def _rms_norm(xf, g):
    ms = jnp.mean(xf * xf, axis=-1, keepdims=True)
    return xf * lax.rsqrt(ms + NORM_EPS) * g


def _rms_norm_rows(src_ref, g_ref, scale_ref, emit, unroll):
    n = src_ref.shape[0]

    def scale_body(c, carry):
        rows = pl.ds(pl.multiple_of(c * NORM_ROWS, NORM_ROWS), NORM_ROWS)
        x = src_ref[rows, :]
        scale_ref[rows, :] = lax.rsqrt(jnp.mean(x * x, axis=-1, keepdims=True) + NORM_EPS)
        return carry

    lax.fori_loop(0, n // NORM_ROWS, scale_body, 0, unroll=unroll)

    def apply_body(c, carry):
        rows = pl.ds(pl.multiple_of(c * NORM_ROWS, NORM_ROWS), NORM_ROWS)
        emit(rows, src_ref[rows, :] * scale_ref[rows, :] * g_ref[...])
        return carry

    lax.fori_loop(0, n // NORM_ROWS, apply_body, 0, unroll=unroll)


def _in_proj_kernel(x_ref, pos_ref, invf_ref, g_ref, w_ref, pw_ref, ps_ref,
                    pool_ref, q_ref, k_ref, vt_ref, w_bf_ref, wvt_ref, trig_ref, hn_ref, u_ref, carry_ref, nscale_ref,
                    *, seq, pool_width, qk_width):
    tm = x_ref.shape[0]
    step = pl.program_id(0)
    rows_per_cast = w_ref.shape[0]
    n_cast = w_bf_ref.shape[0] // rows_per_cast
    n_direct = w_bf_ref.shape[1]
    tiles_per_cast = pos_ref.shape[0]
    lane = lax.broadcasted_iota(jnp.int32, (1, LANES), 1) % DIFF_HEAD_DIM

    @pl.when(step < n_cast)
    def _():
        r0 = pl.multiple_of(step * rows_per_cast, rows_per_cast)
        w_bf_ref[pl.ds(r0, rows_per_cast), :] = w_ref[:, :n_direct].astype(BF16)
        for t in range(tiles_per_cast):
            ang = invf_ref[...] * pos_ref[t].astype(F32)
            comp = jnp.concatenate(
                [jnp.cos(ang), jnp.sin(ang), jnp.zeros((DIFF_HEAD_DIM - ROT_DIM, tm), F32)], axis=0)
            trig_ref[step * tiles_per_cast + t] = jnp.concatenate([comp, comp], axis=0).T

    for t in range(n_cast):
        @pl.when(step == t)
        def _():
            wvt_ref[:, t * rows_per_cast:(t + 1) * rows_per_cast] = w_ref[:, n_direct:].T.astype(BF16)

    @pl.when(step >= n_cast)
    def _():
        tiles_per_seq = seq // tm
        ti = (step - n_cast) % tiles_per_seq

        hn_ref[...] = _rms_norm(x_ref[...], g_ref[...]).astype(BF16)

        @pl.when(ti == 0)
        def _():
            carry_ref[...] = jnp.zeros_like(carry_ref)

        trig = trig_ref[step - n_cast]
        coef_self = jnp.where(lane < ROT_HALF, trig,
                              jnp.where(lane < ROT_DIM, pltpu.roll(trig, ROT_HALF, 1), 1.0))
        coef_lo = jnp.where((lane >= ROT_HALF) & (lane < ROT_DIM), trig, 0.0)
        coef_hi = jnp.where(lane < ROT_HALF, -pltpu.roll(trig, LANES - ROT_HALF, 1), 0.0)

        def rope_chunk(out_ref, chunk, c, scale):
            t = jnp.dot(hn_ref[...], w_bf_ref[:, chunk * N_CHUNK:(chunk + 1) * N_CHUNK],
                        preferred_element_type=F32)
            for h in range(N_CHUNK // LANES):
                th = t[:, h * LANES:(h + 1) * LANES]
                r = (th * coef_self + pltpu.roll(th, ROT_HALF, 1) * coef_lo
                     + pltpu.roll(th, LANES - ROT_HALF, 1) * coef_hi)
                if scale != 1.0:
                    r = r * scale
                out_ref[:, c + h * LANES:c + (h + 1) * LANES] = r.astype(BF16)

        gdim = pool_width // len(POOL_WINDOWS)
        t_in_seq = ti * tm + lax.broadcasted_iota(jnp.int32, (tm, 1), 0)

        def pool_group(g, w):
            c0 = g * gdim
            u = u_ref[:, c0:c0 + gdim]
            ext = jnp.concatenate([carry_ref[g], u], axis=0)
            carry_ref[g] = u[tm - POOL_HALO:, :]
            lvl, off, k = ext, 0, 1
            while k < w:
                new_off = min(off + 8, POOL_HALO)
                cur = lvl[new_off - off:, :]
                shifted = lvl[new_off - off - k: lvl.shape[0] - k, :]
                lvl, off, k = cur + shifted, new_off, 2 * k
            wsum = lvl[POOL_HALO - off:, :]
            cnt = jnp.minimum(t_in_seq + 1, w).astype(F32)
            pooled = (wsum / cnt - u).astype(BF16)
            mixed = jnp.dot(pooled, pw_ref[g].astype(BF16), preferred_element_type=F32)
            pool_ref[:, c0:c0 + gdim] = (mixed * ps_ref[:, c0:c0 + gdim]).astype(BF16)

        n_pool, n_qk = pool_width // N_CHUNK, qk_width // N_CHUNK
        for ci in range(n_pool):
            u_ref[:, ci * N_CHUNK:(ci + 1) * N_CHUNK] = jnp.dot(
                hn_ref[...], w_bf_ref[:, ci * N_CHUNK:(ci + 1) * N_CHUNK], preferred_element_type=F32)

        q_scale = DIFF_HEAD_DIM ** -0.5 * LOG2_E
        chunks = ([(q_ref, n_pool + ci, ci * N_CHUNK, q_scale) for ci in range(n_qk)]
                  + [(k_ref, n_pool + n_qk + ci, ci * N_CHUNK, 1.0) for ci in range(n_qk)])
        groups = list(enumerate(POOL_WINDOWS))
        for idx, args in enumerate(chunks):
            rope_chunk(*args)
            for g, w in groups[idx * len(groups) // len(chunks):(idx + 1) * len(groups) // len(chunks)]:
                pool_group(g, w)

        for c in range(0, vt_ref.shape[0], N_CHUNK):
            vt_ref[c:c + N_CHUNK, :] = lax.dot_general(
                wvt_ref[c:c + N_CHUNK, :], hn_ref[...], (((1,), (1,)), ((), ())),
                preferred_element_type=F32).astype(BF16)


def _in_proj(x2, pos2, inv_freq, g, w_in, pool_w, pool_scale, *, seq, pool_width, qk_width, attn_width):
    n, d = x2.shape
    tm = TM_IN
    n_groups = len(POOL_WINDOWS)
    gdim = pool_width // n_groups
    n_direct = pool_width + 2 * qk_width
    n_cast = IN_CAST_STEPS
    n_tiles = n // tm
    assert n_tiles % n_cast == 0
    tiles_per_cast = n_tiles // n_cast
    const = lambda s: (0, 0)
    row = lambda s: (jnp.maximum(s - n_cast, 0), 0)
    kern = functools.partial(_in_proj_kernel, seq=seq, pool_width=pool_width, qk_width=qk_width)
    return pl.pallas_call(
        kern,
        grid=(n_cast + n_tiles,),
        in_specs=[
            pl.BlockSpec((tm, d), row),
            pl.BlockSpec((tiles_per_cast, 1, tm), lambda s: (jnp.minimum(s, n_cast - 1), 0, 0)),
            pl.BlockSpec((ROT_HALF, 1), const),
            pl.BlockSpec((1, d), const),
            pl.BlockSpec((d // n_cast, w_in.shape[1]), lambda s: (jnp.minimum(s, n_cast - 1), 0)),
            pl.BlockSpec(pool_w.shape, lambda s: (0, 0, 0)),
            pl.BlockSpec((1, pool_width), const),
        ],
        out_specs=[
            pl.BlockSpec((tm, pool_width), row),
            pl.BlockSpec((tm, qk_width), row),
            pl.BlockSpec((tm, qk_width), row),
            pl.BlockSpec((None, attn_width, tm), lambda s: (jnp.maximum(s - n_cast, 0), 0, 0)),
        ],
        out_shape=[
            jax.ShapeDtypeStruct((n, pool_width), BF16),
            jax.ShapeDtypeStruct((n, qk_width), BF16),
            jax.ShapeDtypeStruct((n, qk_width), BF16),
            jax.ShapeDtypeStruct((n // tm, attn_width, tm), BF16),
        ],
        scratch_shapes=[
            pltpu.VMEM((d, n_direct), BF16),
            pltpu.VMEM((attn_width, d), BF16),
            pltpu.VMEM((n_tiles, tm, LANES), F32),
            pltpu.VMEM((tm, d), BF16),
            pltpu.VMEM((tm, pool_width), F32),
            pltpu.VMEM((n_groups, POOL_HALO, gdim), F32),
            pltpu.VMEM((tm, 1), F32),
        ],
        compiler_params=pltpu.CompilerParams(
            dimension_semantics=("arbitrary",), vmem_limit_bytes=VMEM_LIMIT_BYTES),
        name="in_proj",
    )(x2, pos2, inv_freq, g, w_in, pool_w, pool_scale)


def _attn_kernel(lam_ref, sw_ref, q_ref, k_ref, vt_ref, wg_ref, wu_ref, wd_ref,
                 o_ref, wg_bf_ref, wu_bf_ref, wd_bf_ref,
                 qc_ref, s0_ref, s1_ref, mx0_ref, mx1_ref, m_ref, l_ref, acc_ref, *, lambda_init):
    wg_bf_ref[...] = wg_ref[...].astype(BF16)
    wu_bf_ref[...] = wu_ref[...].astype(BF16)
    wd_bf_ref[...] = wd_ref[...].astype(BF16)

    seq = q_ref.shape[0]
    n_tiles = seq // TQ
    n_unmasked = n_tiles * (n_tiles - 1)
    assert n_tiles >= 2 and n_unmasked % ATTN_UNROLL == 0 and ATTN_UNROLL % 2 == 0
    s_slots, mx_slots = (s0_ref, s1_ref), (mx0_ref, mx1_ref)
    chunks = [slice(a, a + ATTN_CHUNK) for a in range(0, TQ, ATTN_CHUNK)]
    chunk_plan = {
        "full": [(ch, None, TK) for ch in chunks],
        "lo": [(ch, ch.start, min(TK, ch.stop)) if ch.start < TK else (ch, None, TK) for ch in chunks],
        "hi": [(ch, ch.start - TK, min(TK, ch.stop - TK)) for ch in chunks if ch.start >= TK],
    }

    def rows(i, size):
        return pl.ds(i * size if isinstance(i, int) else pl.multiple_of(i * size, size), size)

    def key_rows(j, n_keys):
        start = j * TK if isinstance(j, int) else pl.multiple_of(j * TK, TK)
        return pl.ds(start, n_keys)

    def scores_chunk(qi, j, slot, c, ch, offset, n_keys):
        s = lax.dot_general(k_ref[key_rows(j, n_keys), :], qc_ref[qi, c, ch, :], (((1,), (1,)), ((), ())),
                            preferred_element_type=F32)
        if offset is not None:
            kv = lax.broadcasted_iota(jnp.int32, s.shape, 0)
            r = lax.broadcasted_iota(jnp.int32, s.shape, 1)
            s = jnp.where(kv <= r + offset, s, NEG_INF)
        s_slots[slot][c, 0:n_keys, ch] = s
        mx_slots[slot][c, :, ch] = jnp.max(s, axis=0, keepdims=True)

    def exp_pv_chunk(qi, j, slot, c, ch, n_keys):
        m_prev = m_ref[qi, c, :, ch]
        m_new = jnp.maximum(m_prev, mx_slots[slot][c, :, ch])
        alpha = jnp.exp2(m_prev - m_new)
        p = jnp.exp2(s_slots[slot][c, 0:n_keys, ch] - m_new)
        l_ref[qi, c, :, ch] = alpha * l_ref[qi, c, :, ch] + jnp.sum(p, axis=0, keepdims=True)
        pv = jnp.dot(vt_ref[j, :, 0:n_keys], p.astype(BF16), preferred_element_type=F32)
        acc_ref[qi, c, :, ch] = alpha * acc_ref[qi, c, :, ch] + pv
        m_ref[qi, c, :, ch] = m_new

    def stage(cur, cur_kind, slot, nxt, nxt_kind):
        todo_next = chunk_plan[nxt_kind] if nxt is not None else []
        todo_cur = [(ch, n_keys) for ch, _, n_keys in chunk_plan[cur_kind]]
        for c in range(2):
            for i in range(max(len(todo_next), len(todo_cur))):
                if i < len(todo_next):
                    scores_chunk(nxt[0], nxt[1], 1 - slot, c, *todo_next[i])
                if i < len(todo_cur):
                    exp_pv_chunk(cur[0], cur[1], slot, c, *todo_cur[i])

    def finalize(qi):
        lam_v = lam_ref[...]
        lam = (jnp.exp(jnp.sum(lam_v[0:1] * lam_v[1:2], keepdims=True))
               - jnp.exp(jnp.sum(lam_v[2:3] * lam_v[3:4], keepdims=True)) + lambda_init)
        attn_t = acc_ref[qi, 0] / l_ref[qi, 0] - lam * (acc_ref[qi, 1] / l_ref[qi, 1])
        ms = jnp.mean(attn_t * attn_t, axis=0, keepdims=True)
        y = (attn_t * lax.rsqrt(ms + NORM_EPS)).T
        o_ref[rows(qi, TQ), :] = (y * sw_ref[...] * (1.0 - lambda_init)).astype(BF16)

    lane = lax.broadcasted_iota(jnp.int32, (TQ, DIFF_V_DIM), 1)
    for qi in range(n_tiles):
        q = q_ref[qi * TQ:(qi + 1) * TQ, :]
        zero = jnp.zeros_like(q)
        qc_ref[qi, 0] = jnp.where(lane < DIFF_HEAD_DIM, q, zero)
        qc_ref[qi, 1] = jnp.where(lane >= DIFF_HEAD_DIM, q, zero)
    m_ref[...] = jnp.full_like(m_ref, NEG_INF)
    l_ref[...] = jnp.zeros_like(l_ref)
    acc_ref[...] = jnp.zeros_like(acc_ref)

    def unmasked_pair(n):
        qi = 1
        for t in range(2, n_tiles):
            qi = qi + jnp.where(n >= t * (t - 1), 1, 0)
        return qi, n - qi * (qi - 1)

    for c in range(2):
        for plan in chunk_plan["full"]:
            scores_chunk(1, 0, 0, c, *plan)

    def unmasked_body(it, carry):
        for u in range(ATTN_UNROLL):
            n = it * ATTN_UNROLL + u
            stage(unmasked_pair(n), "full", u % 2, unmasked_pair(jnp.minimum(n + 1, n_unmasked - 1)), "full")
        return carry

    lax.fori_loop(0, n_unmasked // ATTN_UNROLL, unmasked_body, 0)

    def masked_stages(qi, nxt_qi):
        stage((qi, 2 * qi), "lo", 0, (qi, 2 * qi + 1), "hi")
        stage((qi, 2 * qi + 1), "hi", 1, (nxt_qi, 2 * nxt_qi), "lo")

    for c in range(2):
        for plan in chunk_plan["lo"]:
            scores_chunk(0, 0, 0, c, *plan)
    for qi in range(n_tiles):
        if qi > 0:
            finalize(qi - 1)
        masked_stages(qi, min(qi + 1, n_tiles - 1))
    finalize(n_tiles - 1)


def _attention(q, k, vt, lam_vecs, subln_w, ffn_weights, *, batch, seq, n_heads, lambda_init):
    n, width = q.shape
    nk = seq // TK
    n_tiles = seq // TQ
    n_steps = batch * n_heads
    assert TQ == 2 * TK and seq % TQ == 0 and vt.shape == (n // TK, n_heads * DIFF_V_DIM, TK)
    kern = functools.partial(_attn_kernel, lambda_init=lambda_init)
    head = lambda b, h: (b, h)
    step_rows = lambda b, h: (b * n_heads + h, 0)
    w_specs = [pl.BlockSpec((w.shape[0] // n_steps, w.shape[1]), step_rows) for w in ffn_weights]
    outs = pl.pallas_call(
        kern,
        grid=(batch, n_heads),
        in_specs=[
            pl.BlockSpec(lam_vecs.shape, lambda b, h: (0, 0)),
            pl.BlockSpec((1, DIFF_V_DIM), lambda b, h: (0, 0)),
            pl.BlockSpec((seq, DIFF_V_DIM), head),
            pl.BlockSpec((seq, DIFF_V_DIM), head),
            pl.BlockSpec((nk, DIFF_V_DIM, TK), lambda b, h: (b, h, 0)),
        ] + w_specs,
        out_specs=[pl.BlockSpec((seq, DIFF_V_DIM), head)] + w_specs,
        out_shape=[jax.ShapeDtypeStruct((n, width), BF16)]
        + [jax.ShapeDtypeStruct(w.shape, BF16) for w in ffn_weights],
        scratch_shapes=[
            pltpu.VMEM((n_tiles, 2, TQ, DIFF_V_DIM), BF16),
            pltpu.VMEM((2, TK, TQ), F32),
            pltpu.VMEM((2, TK, TQ), F32),
            pltpu.VMEM((2, 1, TQ), F32),
            pltpu.VMEM((2, 1, TQ), F32),
            pltpu.VMEM((n_tiles, 2, 1, TQ), F32),
            pltpu.VMEM((n_tiles, 2, 1, TQ), F32),
            pltpu.VMEM((n_tiles, 2, DIFF_V_DIM, TQ), F32),
        ],
        compiler_params=pltpu.CompilerParams(
            dimension_semantics=("arbitrary", "arbitrary"), vmem_limit_bytes=VMEM_LIMIT_BYTES),
        name="diff_attention",
    )(lam_vecs, subln_w, q, k, vt, *ffn_weights)
    return outs[0], outs[1:]


def _out_proj_kernel(pool_ref, attn_ref, x_ref, w_ref, g_ref, h_ref, w_bf_ref, mix_ref, nscale_ref):
    step = pl.program_id(0)
    rows_per_cast = w_ref.shape[0]
    n_cast = w_bf_ref.shape[0] // rows_per_cast
    pw = pool_ref.shape[1]

    @pl.when(step < n_cast)
    def _():
        r0 = pl.multiple_of(step * rows_per_cast, rows_per_cast)
        w_bf_ref[pl.ds(r0, rows_per_cast), :] = w_ref[...].astype(BF16)

    @pl.when(step >= n_cast)
    def _():
        for c in range(0, mix_ref.shape[1], N_CHUNK):
            mix_ref[:, c:c + N_CHUNK] = (
                jnp.dot(pool_ref[...], w_bf_ref[:pw, c:c + N_CHUNK], preferred_element_type=F32)
                + jnp.dot(attn_ref[...], w_bf_ref[pw:, c:c + N_CHUNK], preferred_element_type=F32))

        def store_h(rows, y):
            h_ref[rows, :] = x_ref[rows, :] + y

        _rms_norm_rows(mix_ref, g_ref, nscale_ref, store_h, unroll=True)


def _out_proj(pool_out, attn_out, x2, w_out, g):
    n, d = x2.shape
    tm = TM_OUT
    n_cast = OUT_CAST_STEPS
    row = lambda s: (jnp.maximum(s - n_cast, 0), 0)
    const = lambda s: (0, 0)
    return pl.pallas_call(
        _out_proj_kernel,
        grid=(n_cast + n // tm,),
        in_specs=[
            pl.BlockSpec((tm, pool_out.shape[1]), row),
            pl.BlockSpec((tm, attn_out.shape[1]), row),
            pl.BlockSpec((tm, d), row),
            pl.BlockSpec((w_out.shape[0] // n_cast, w_out.shape[1]), lambda s: (jnp.minimum(s, n_cast - 1), 0)),
            pl.BlockSpec((1, d), const),
        ],
        out_specs=pl.BlockSpec((tm, d), row),
        out_shape=jax.ShapeDtypeStruct((n, d), F32),
        scratch_shapes=[
            pltpu.VMEM(w_out.shape, BF16),
            pltpu.VMEM((tm, d), F32),
            pltpu.VMEM((tm, 1), F32),
        ],
        compiler_params=pltpu.CompilerParams(
            dimension_semantics=("arbitrary",), vmem_limit_bytes=VMEM_LIMIT_BYTES),
        name="out_proj",
    )(pool_out, attn_out, x2, w_out, g)


def _ffn_kernel(h_ref, gpre_ref, gpost_ref, wg_ref, wu_ref, wd_ref, o_ref, hn_ref, nscale_ref):
    f = pl.program_id(1)

    @pl.when(f == 0)
    def _():
        def store_hn(rows, y):
            hn_ref[rows, :] = y.astype(BF16)
            o_ref[rows, :] = jnp.zeros_like(y)

        _rms_norm_rows(h_ref, gpre_ref, nscale_ref, store_hn, unroll=True)

    hn = hn_ref[...]
    gate = jnp.dot(hn, wg_ref[...], preferred_element_type=F32)
    up = jnp.dot(hn, wu_ref[...], preferred_element_type=F32)
    act = (gate * jax.nn.sigmoid(gate) * up).astype(BF16)
    for c in range(0, o_ref.shape[1], N_CHUNK):
        o_ref[:, c:c + N_CHUNK] += jnp.dot(act, wd_ref[:, c:c + N_CHUNK], preferred_element_type=F32)

    @pl.when(f == pl.num_programs(1) - 1)
    def _():
        def store_out(rows, y):
            o_ref[rows, :] = h_ref[rows, :] + y

        _rms_norm_rows(o_ref, gpost_ref, nscale_ref, store_out, unroll=True)


def _ffn(h, g_pre, g_post, w_gate, w_up, w_down):
    n, d = h.shape
    d_ff = w_gate.shape[1]
    tm, tf = TM_FFN, TF_FFN
    return pl.pallas_call(
        _ffn_kernel,
        grid=(n // tm, d_ff // tf),
        in_specs=[
            pl.BlockSpec((tm, d), lambda i, f: (i, 0)),
            pl.BlockSpec((1, d), lambda i, f: (0, 0)),
            pl.BlockSpec((1, d), lambda i, f: (0, 0)),
            pl.BlockSpec((d, tf), lambda i, f: (0, f)),
            pl.BlockSpec((d, tf), lambda i, f: (0, f)),
            pl.BlockSpec((tf, d), lambda i, f: (f, 0)),
        ],
        out_specs=pl.BlockSpec((tm, d), lambda i, f: (i, 0)),
        out_shape=jax.ShapeDtypeStruct((n, d), F32),
        scratch_shapes=[pltpu.VMEM((tm, d), BF16), pltpu.VMEM((tm, 1), F32)],
        compiler_params=pltpu.CompilerParams(
            dimension_semantics=("arbitrary", "arbitrary"), vmem_limit_bytes=VMEM_LIMIT_BYTES),
        name="ffn",
    )(h, g_pre, g_post, w_gate, w_up, w_down)


def kernel(x, positions, pre_mix_norm, post_mix_norm, w_in, pool_w, pool_scale,
           lam_q1, lam_k1, lam_q2, lam_k2, subln_w, w_out,
           pre_ffn_norm, post_ffn_norm, w_gate, w_up, w_down):
    batch, seq, d_model = x.shape
    depth = w_in.shape[0]
    pool_width = pool_scale.shape[1]
    attn_width = w_out.shape[1] - pool_width
    qk_width = (w_in.shape[2] - pool_width - attn_width) // 2
    n_heads = attn_width // DIFF_V_DIM
    assert qk_width == n_heads * 2 * DIFF_HEAD_DIM
    assert seq % TQ == 0 and seq % TM_IN == 0 and (batch * seq) % TM_FFN == 0
    assert TM_IN == TK

    n = batch * seq
    h = x.reshape(n, d_model)
    pos_rows = positions.reshape(n // TM_IN, 1, TM_IN)
    inv_freq = (ROPE_THETA ** (-jnp.arange(0, ROT_DIM, 2, dtype=F32) / ROT_DIM)).reshape(ROT_HALF, 1)

    for l in range(depth):
        lambda_init = _lambda_init(l)
        pool_out, q, k, vt = _in_proj(
            h, pos_rows, inv_freq, pre_mix_norm[l].reshape(1, -1), w_in[l], pool_w[l],
            pool_scale[l].reshape(1, -1),
            seq=seq, pool_width=pool_width, qk_width=qk_width, attn_width=attn_width)
        lam_vecs = jnp.stack([lam_q1[l], lam_k1[l], lam_q2[l], lam_k2[l]]).astype(F32)
        attn_out, (wg_bf, wu_bf, wd_bf) = _attention(
            q, k, vt, lam_vecs, subln_w[l].reshape(1, -1), (w_gate[l], w_up[l], w_down[l]),
            batch=batch, seq=seq, n_heads=n_heads, lambda_init=lambda_init)
        h = _out_proj(pool_out, attn_out, h, w_out[l], post_mix_norm[l].reshape(1, -1))
        h = _ffn(h, pre_ffn_norm[l].reshape(1, -1), post_ffn_norm[l].reshape(1, -1), wg_bf, wu_bf, wd_bf)
    return h.reshape(batch, seq, d_model)
```

```python
import functools
import math

import jax
import jax.numpy as jnp
from jax import lax
from jax.experimental import pallas as pl
from jax.experimental.pallas import tpu as pltpu

F32 = jnp.float32
BF16 = jnp.bfloat16

POOL_WINDOWS = (2, 4, 8, 16)
DIFF_HEAD_DIM = 64
DIFF_V_DIM = 2 * DIFF_HEAD_DIM
ROPE_THETA = 500000.0
ROT_DIM = DIFF_HEAD_DIM // 4
ROT_HALF = ROT_DIM // 2
NORM_EPS = 1e-6
NEG_INF = -1e30
LOG2_E = math.log2(math.e)

LANES = 128
POOL_HALO = 32
VMEM_LIMIT_BYTES = 56 * 1024 * 1024

TM_IN = 512
TQ = 1024
TK = 512
ATTN_CHUNK = 256
ATTN_UNROLL = 6
TM_OUT = 512
TM_FFN = 1024
TF_FFN = 512
FFN_ACT_CHUNK = 256
N_CHUNK = 512
NORM_ROWS = 64
IN_CAST_STEPS = 8


def _lambda_init(layer_idx):
    return 0.8 - 0.6 * math.exp(-0.3 * layer_idx)


def _rms_norm(xf, g):
    ms = jnp.mean(xf * xf, axis=-1, keepdims=True)
    return xf * lax.rsqrt(ms + NORM_EPS) * g


def _rms_norm_rows(src_ref, g_ref, scale_ref, emit, unroll):
    n = src_ref.shape[0]

    def scale_body(c, carry):
        rows = pl.ds(pl.multiple_of(c * NORM_ROWS, NORM_ROWS), NORM_ROWS)
        x = src_ref[rows, :]
        scale_ref[rows, :] = lax.rsqrt(jnp.mean(x * x, axis=-1, keepdims=True) + NORM_EPS)
        return carry

    lax.fori_loop(0, n // NORM_ROWS, scale_body, 0, unroll=unroll)

    def apply_body(c, carry):
        rows = pl.ds(pl.multiple_of(c * NORM_ROWS, NORM_ROWS), NORM_ROWS)
        emit(rows, src_ref[rows, :] * scale_ref[rows, :] * g_ref[...])
        return carry

    lax.fori_loop(0, n // NORM_ROWS, apply_body, 0, unroll=unroll)


def _in_proj_kernel(x_ref, pos_ref, invf_ref, g_ref, w_ref, pw_ref, ps_ref, wo_ref,
                    pool_ref, q_ref, k_ref, vt_ref, wo_bf_ref,
                    w_bf_ref, wvt_ref, trig_ref, hn_ref, u_ref, carry_ref, *, seq, pool_width, qk_width):
    tm = x_ref.shape[0]
    step = pl.program_id(0)
    rows_per_cast = w_ref.shape[0]
    n_cast = w_bf_ref.shape[0] // rows_per_cast
    n_direct = w_bf_ref.shape[1]
    tiles_per_cast = pos_ref.shape[0]
    lane = lax.broadcasted_iota(jnp.int32, (1, LANES), 1) % DIFF_HEAD_DIM

    @pl.when(step < n_cast)
    def _():
        r0 = pl.multiple_of(step * rows_per_cast, rows_per_cast)
        w_bf_ref[pl.ds(r0, rows_per_cast), :] = w_ref[:, :n_direct].astype(BF16)
        for t in range(tiles_per_cast):
            ang = invf_ref[...] * pos_ref[t].astype(F32)
            comp = jnp.concatenate(
                [jnp.cos(ang), jnp.sin(ang), jnp.zeros((DIFF_HEAD_DIM - ROT_DIM, tm), F32)], axis=0)
            trig_ref[step * tiles_per_cast + t] = jnp.concatenate([comp, comp], axis=0).T

    for t in range(n_cast):
        @pl.when(step == t)
        def _():
            wvt_ref[:, t * rows_per_cast:(t + 1) * rows_per_cast] = w_ref[:, n_direct:].T.astype(BF16)

    @pl.when(step >= n_cast)
    def _():
        tiles_per_seq = seq // tm
        ti = (step - n_cast) % tiles_per_seq

        wo_bf_ref[...] = wo_ref[...].astype(BF16)
        hn_ref[...] = _rms_norm(x_ref[...], g_ref[...]).astype(BF16)

        @pl.when(ti == 0)
        def _():
            carry_ref[...] = jnp.zeros_like(carry_ref)

        trig = trig_ref[step - n_cast]
        coef_self = jnp.where(lane < ROT_HALF, trig,
                              jnp.where(lane < ROT_DIM, pltpu.roll(trig, ROT_HALF, 1), 1.0))
        coef_lo = jnp.where((lane >= ROT_HALF) & (lane < ROT_DIM), trig, 0.0)
        coef_hi = jnp.where(lane < ROT_HALF, -pltpu.roll(trig, LANES - ROT_HALF, 1), 0.0)

        def rope_chunk(out_ref, chunk, c, scale):
            t = jnp.dot(hn_ref[...], w_bf_ref[:, chunk * N_CHUNK:(chunk + 1) * N_CHUNK],
                        preferred_element_type=F32)
            for h in range(N_CHUNK // LANES):
                th = t[:, h * LANES:(h + 1) * LANES]
                r = (th * coef_self + pltpu.roll(th, ROT_HALF, 1) * coef_lo
                     + pltpu.roll(th, LANES - ROT_HALF, 1) * coef_hi)
                if scale != 1.0:
                    r = r * scale
                out_ref[:, c + h * LANES:c + (h + 1) * LANES] = r.astype(BF16)

        gdim = pool_width // len(POOL_WINDOWS)
        t_in_seq = ti * tm + lax.broadcasted_iota(jnp.int32, (tm, 1), 0)

        def pool_group(g, w):
            c0 = g * gdim
            u = u_ref[:, c0:c0 + gdim]
            ext = jnp.concatenate([carry_ref[g], u], axis=0)
            carry_ref[g] = u[tm - POOL_HALO:, :]
            lvl, off, k = ext, 0, 1
            while k < w:
                new_off = min(off + 8, POOL_HALO)
                cur = lvl[new_off - off:, :]
                shifted = lvl[new_off - off - k: lvl.shape[0] - k, :]
                lvl, off, k = cur + shifted, new_off, 2 * k
            wsum = lvl[POOL_HALO - off:, :]
            cnt = jnp.minimum(t_in_seq + 1, w).astype(F32)
            pooled = (wsum / cnt - u).astype(BF16)
            mixed = jnp.dot(pooled, pw_ref[g].astype(BF16), preferred_element_type=F32)
            pool_ref[:, c0:c0 + gdim] = (mixed * ps_ref[:, c0:c0 + gdim]).astype(BF16)

        n_pool, n_qk = pool_width // N_CHUNK, qk_width // N_CHUNK
        for ci in range(n_pool):
            u_ref[:, ci * N_CHUNK:(ci + 1) * N_CHUNK] = jnp.dot(
                hn_ref[...], w_bf_ref[:, ci * N_CHUNK:(ci + 1) * N_CHUNK], preferred_element_type=F32)

        q_scale = DIFF_HEAD_DIM ** -0.5 * LOG2_E
        chunks = ([(q_ref, n_pool + ci, ci * N_CHUNK, q_scale) for ci in range(n_qk)]
                  + [(k_ref, n_pool + n_qk + ci, ci * N_CHUNK, 1.0) for ci in range(n_qk)])
        groups = list(enumerate(POOL_WINDOWS))
        for idx, args in enumerate(chunks):
            rope_chunk(*args)
            for g, w in groups[idx * len(groups) // len(chunks):(idx + 1) * len(groups) // len(chunks)]:
                pool_group(g, w)

        for c in range(0, vt_ref.shape[0], N_CHUNK):
            vt_ref[c:c + N_CHUNK, :] = lax.dot_general(
                wvt_ref[c:c + N_CHUNK, :], hn_ref[...], (((1,), (1,)), ((), ())),
                preferred_element_type=F32).astype(BF16)


def _in_proj(x2, pos2, inv_freq, g, w_in, pool_w, pool_scale, w_out, *, seq, pool_width, qk_width, attn_width):
    n, d = x2.shape
    tm = TM_IN
    n_groups = len(POOL_WINDOWS)
    gdim = pool_width // n_groups
    n_direct = pool_width + 2 * qk_width
    n_cast = IN_CAST_STEPS
    n_tiles = n // tm
    assert n_tiles % n_cast == 0
    tiles_per_cast = n_tiles // n_cast
    const = lambda s: (0, 0)
    row = lambda s: (jnp.maximum(s - n_cast, 0), 0)
    kern = functools.partial(_in_proj_kernel, seq=seq, pool_width=pool_width, qk_width=qk_width)
    return pl.pallas_call(
        kern,
        grid=(n_cast + n_tiles,),
        in_specs=[
            pl.BlockSpec((tm, d), row),
            pl.BlockSpec((tiles_per_cast, 1, tm), lambda s: (jnp.minimum(s, n_cast - 1), 0, 0)),
            pl.BlockSpec((ROT_HALF, 1), const),
            pl.BlockSpec((1, d), const),
            pl.BlockSpec((d // n_cast, w_in.shape[1]), lambda s: (jnp.minimum(s, n_cast - 1), 0)),
            pl.BlockSpec(pool_w.shape, lambda s: (0, 0, 0)),
            pl.BlockSpec((1, pool_width), const),
            pl.BlockSpec((w_out.shape[0] // n_tiles, w_out.shape[1]), row),
        ],
        out_specs=[
            pl.BlockSpec((tm, pool_width), row),
            pl.BlockSpec((tm, qk_width), row),
            pl.BlockSpec((tm, qk_width), row),
            pl.BlockSpec((None, attn_width, tm), lambda s: (jnp.maximum(s - n_cast, 0), 0, 0)),
            pl.BlockSpec((w_out.shape[0] // n_tiles, w_out.shape[1]), row),
        ],
        out_shape=[
            jax.ShapeDtypeStruct((n, pool_width), BF16),
            jax.ShapeDtypeStruct((n, qk_width), BF16),
            jax.ShapeDtypeStruct((n, qk_width), BF16),
            jax.ShapeDtypeStruct((n // tm, attn_width, tm), BF16),
            jax.ShapeDtypeStruct(w_out.shape, BF16),
        ],
        scratch_shapes=[
            pltpu.VMEM((d, n_direct), BF16),
            pltpu.VMEM((attn_width, d), BF16),
            pltpu.VMEM((n_tiles, tm, LANES), F32),
            pltpu.VMEM((tm, d), BF16),
            pltpu.VMEM((tm, pool_width), F32),
            pltpu.VMEM((n_groups, POOL_HALO, gdim), F32),
        ],
        compiler_params=pltpu.CompilerParams(
            dimension_semantics=("arbitrary",), vmem_limit_bytes=VMEM_LIMIT_BYTES),
        name="in_proj",
    )(x2, pos2, inv_freq, g, w_in, pool_w, pool_scale, w_out)


def _attn_kernel(lam_ref, sw_ref, q_ref, k_ref, vt_ref, wg_ref, wu_ref, wd_ref,
                 o_ref, wg_bf_ref, wu_bf_ref, wd_bf_ref,
                 qc_ref, s0_ref, s1_ref, mx0_ref, mx1_ref, m_ref, l_ref, acc_ref, *, lambda_init):
    wg_bf_ref[...] = wg_ref[...].astype(BF16)
    wu_bf_ref[...] = wu_ref[...].astype(BF16)
    wd_bf_ref[...] = wd_ref[...].astype(BF16)

    seq = q_ref.shape[0]
    n_tiles = seq // TQ
    n_unmasked = n_tiles * (n_tiles - 1)
    assert n_tiles >= 2 and n_unmasked % ATTN_UNROLL == 0 and ATTN_UNROLL % 2 == 0
    s_slots, mx_slots = (s0_ref, s1_ref), (mx0_ref, mx1_ref)
    chunks = [slice(a, a + ATTN_CHUNK) for a in range(0, TQ, ATTN_CHUNK)]
    chunk_plan = {
        "full": [(ch, None, TK) for ch in chunks],
        "lo": [(ch, ch.start, min(TK, ch.stop)) if ch.start < TK else (ch, None, TK) for ch in chunks],
        "hi": [(ch, ch.start - TK, min(TK, ch.stop - TK)) for ch in chunks if ch.start >= TK],
    }

    def rows(i, size):
        return pl.ds(i * size if isinstance(i, int) else pl.multiple_of(i * size, size), size)

    def key_rows(j, n_keys):
        start = j * TK if isinstance(j, int) else pl.multiple_of(j * TK, TK)
        return pl.ds(start, n_keys)

    def scores_chunk(qi, j, slot, c, ch, offset, n_keys):
        s = lax.dot_general(k_ref[key_rows(j, n_keys), :], qc_ref[qi, c, ch, :], (((1,), (1,)), ((), ())),
                            preferred_element_type=F32)
        if offset is not None:
            kv = lax.broadcasted_iota(jnp.int32, s.shape, 0)
            r = lax.broadcasted_iota(jnp.int32, s.shape, 1)
            s = jnp.where(kv <= r + offset, s, NEG_INF)
        s_slots[slot][c, 0:n_keys, ch] = s
        mx_slots[slot][c, :, ch] = jnp.max(s, axis=0, keepdims=True)

    def exp_pv_chunk(qi, j, slot, c, ch, n_keys):
        m_prev = m_ref[qi, c, :, ch]
        m_new = jnp.maximum(m_prev, mx_slots[slot][c, :, ch])
        alpha = jnp.exp2(m_prev - m_new)
        p = jnp.exp2(s_slots[slot][c, 0:n_keys, ch] - m_new)
        l_ref[qi, c, :, ch] = alpha * l_ref[qi, c, :, ch] + jnp.sum(p, axis=0, keepdims=True)
        pv = jnp.dot(vt_ref[j, :, 0:n_keys], p.astype(BF16), preferred_element_type=F32)
        acc_ref[qi, c, :, ch] = alpha * acc_ref[qi, c, :, ch] + pv
        m_ref[qi, c, :, ch] = m_new

    def stage(cur, cur_kind, slot, nxt, nxt_kind):
        todo_next = chunk_plan[nxt_kind] if nxt is not None else []
        todo_cur = [(ch, n_keys) for ch, _, n_keys in chunk_plan[cur_kind]]
        for c in range(2):
            for i in range(max(len(todo_next), len(todo_cur))):
                if i < len(todo_next):
                    scores_chunk(nxt[0], nxt[1], 1 - slot, c, *todo_next[i])
                if i < len(todo_cur):
                    exp_pv_chunk(cur[0], cur[1], slot, c, *todo_cur[i])

    def finalize(qi):
        lam_v = lam_ref[...]
        lam = (jnp.exp(jnp.sum(lam_v[0:1] * lam_v[1:2], keepdims=True))
               - jnp.exp(jnp.sum(lam_v[2:3] * lam_v[3:4], keepdims=True)) + lambda_init)
        attn_t = acc_ref[qi, 0] / l_ref[qi, 0] - lam * (acc_ref[qi, 1] / l_ref[qi, 1])
        ms = jnp.mean(attn_t * attn_t, axis=0, keepdims=True)
        y = (attn_t * lax.rsqrt(ms + NORM_EPS)).T
        o_ref[rows(qi, TQ), :] = (y * sw_ref[...] * (1.0 - lambda_init)).astype(BF16)

    lane = lax.broadcasted_iota(jnp.int32, (TQ, DIFF_V_DIM), 1)
    for qi in range(n_tiles):
        q = q_ref[qi * TQ:(qi + 1) * TQ, :]
        zero = jnp.zeros_like(q)
        qc_ref[qi, 0] = jnp.where(lane < DIFF_HEAD_DIM, q, zero)
        qc_ref[qi, 1] = jnp.where(lane >= DIFF_HEAD_DIM, q, zero)
    m_ref[...] = jnp.full_like(m_ref, NEG_INF)
    l_ref[...] = jnp.zeros_like(l_ref)
    acc_ref[...] = jnp.zeros_like(acc_ref)

    def unmasked_pair(n):
        qi = 1
        for t in range(2, n_tiles):
            qi = qi + jnp.where(n >= t * (t - 1), 1, 0)
        return qi, n - qi * (qi - 1)

    for c in range(2):
        for plan in chunk_plan["full"]:
            scores_chunk(1, 0, 0, c, *plan)

    def unmasked_body(it, carry):
        for u in range(ATTN_UNROLL):
            n = it * ATTN_UNROLL + u
            stage(unmasked_pair(n), "full", u % 2, unmasked_pair(jnp.minimum(n + 1, n_unmasked - 1)), "full")
        return carry

    lax.fori_loop(0, n_unmasked // ATTN_UNROLL, unmasked_body, 0)

    def masked_stages(qi, nxt_qi):
        stage((qi, 2 * qi), "lo", 0, (qi, 2 * qi + 1), "hi")
        stage((qi, 2 * qi + 1), "hi", 1, (nxt_qi, 2 * nxt_qi), "lo")

    for c in range(2):
        for plan in chunk_plan["lo"]:
            scores_chunk(0, 0, 0, c, *plan)
    for qi in range(n_tiles):
        if qi > 0:
            finalize(qi - 1)
        masked_stages(qi, min(qi + 1, n_tiles - 1))
    finalize(n_tiles - 1)


def _attention(q, k, vt, lam_vecs, subln_w, ffn_weights, *, batch, seq, n_heads, lambda_init):
    n, width = q.shape
    nk = seq // TK
    n_tiles = seq // TQ
    n_steps = batch * n_heads
    assert TQ == 2 * TK and seq % TQ == 0 and vt.shape == (n // TK, n_heads * DIFF_V_DIM, TK)
    kern = functools.partial(_attn_kernel, lambda_init=lambda_init)
    head = lambda b, h: (b, h)
    step_rows = lambda b, h: (b * n_heads + h, 0)
    w_specs = [pl.BlockSpec((w.shape[0] // n_steps, w.shape[1]), step_rows) for w in ffn_weights]
    outs = pl.pallas_call(
        kern,
        grid=(batch, n_heads),
        in_specs=[
            pl.BlockSpec(lam_vecs.shape, lambda b, h: (0, 0)),
            pl.BlockSpec((1, DIFF_V_DIM), lambda b, h: (0, 0)),
            pl.BlockSpec((seq, DIFF_V_DIM), head),
            pl.BlockSpec((seq, DIFF_V_DIM), head),
            pl.BlockSpec((nk, DIFF_V_DIM, TK), lambda b, h: (b, h, 0)),
        ] + w_specs,
        out_specs=[pl.BlockSpec((seq, DIFF_V_DIM), head)] + w_specs,
        out_shape=[jax.ShapeDtypeStruct((n, width), BF16)]
        + [jax.ShapeDtypeStruct(w.shape, BF16) for w in ffn_weights],
        scratch_shapes=[
            pltpu.VMEM((n_tiles, 2, TQ, DIFF_V_DIM), BF16),
            pltpu.VMEM((2, TK, TQ), F32),
            pltpu.VMEM((2, TK, TQ), F32),
            pltpu.VMEM((2, 1, TQ), F32),
            pltpu.VMEM((2, 1, TQ), F32),
            pltpu.VMEM((n_tiles, 2, 1, TQ), F32),
            pltpu.VMEM((n_tiles, 2, 1, TQ), F32),
            pltpu.VMEM((n_tiles, 2, DIFF_V_DIM, TQ), F32),
        ],
        compiler_params=pltpu.CompilerParams(
            dimension_semantics=("arbitrary", "arbitrary"), vmem_limit_bytes=VMEM_LIMIT_BYTES),
        name="diff_attention",
    )(lam_vecs, subln_w, q, k, vt, *ffn_weights)
    return outs[0], outs[1:]


def _out_proj_kernel(pool_ref, attn_ref, x_ref, w_ref, g_ref, h_ref, mix_ref, nscale_ref):
    pw = pool_ref.shape[1]
    for c in range(0, mix_ref.shape[1], N_CHUNK):
        mix_ref[:, c:c + N_CHUNK] = (
            jnp.dot(pool_ref[...], w_ref[:pw, c:c + N_CHUNK], preferred_element_type=F32)
            + jnp.dot(attn_ref[...], w_ref[pw:, c:c + N_CHUNK], preferred_element_type=F32))

    def store_h(rows, y):
        h_ref[rows, :] = x_ref[rows, :] + y

    _rms_norm_rows(mix_ref, g_ref, nscale_ref, store_h, unroll=True)


def _out_proj(pool_out, attn_out, x2, w_out_bf, g):
    n, d = x2.shape
    tm = TM_OUT
    row = lambda i: (i, 0)
    const = lambda i: (0, 0)
    return pl.pallas_call(
        _out_proj_kernel,
        grid=(n // tm,),
        in_specs=[
            pl.BlockSpec((tm, pool_out.shape[1]), row),
            pl.BlockSpec((tm, attn_out.shape[1]), row),
            pl.BlockSpec((tm, d), row),
            pl.BlockSpec(w_out_bf.shape, const),
            pl.BlockSpec((1, d), const),
        ],
        out_specs=pl.BlockSpec((tm, d), row),
        out_shape=jax.ShapeDtypeStruct((n, d), F32),
        scratch_shapes=[
            pltpu.VMEM((tm, d), F32),
            pltpu.VMEM((tm, 1), F32),
        ],
        compiler_params=pltpu.CompilerParams(
            dimension_semantics=("arbitrary",), vmem_limit_bytes=VMEM_LIMIT_BYTES),
        name="out_proj",
    )(pool_out, attn_out, x2, w_out_bf, g)


def _ffn_kernel(h_ref, gpre_ref, gpost_ref, wg_ref, wu_ref, wd_ref, o_ref, hn_ref, nscale_ref, act_ref):
    f = pl.program_id(1)

    @pl.when(f == 0)
    def _():
        def store_hn(rows, y):
            hn_ref[rows, :] = y.astype(BF16)
            o_ref[rows, :] = jnp.zeros_like(y)

        _rms_norm_rows(h_ref, gpre_ref, nscale_ref, store_hn, unroll=True)

    hn = hn_ref[...]
    for cc in range(0, wg_ref.shape[1], FFN_ACT_CHUNK):
        gate = jnp.dot(hn, wg_ref[:, cc:cc + FFN_ACT_CHUNK], preferred_element_type=F32)
        up = jnp.dot(hn, wu_ref[:, cc:cc + FFN_ACT_CHUNK], preferred_element_type=F32)
        act_ref[:, cc:cc + FFN_ACT_CHUNK] = (gate * jax.nn.sigmoid(gate) * up).astype(BF16)
    for c in range(0, o_ref.shape[1], N_CHUNK):
        o_ref[:, c:c + N_CHUNK] += jnp.dot(act_ref[...], wd_ref[:, c:c + N_CHUNK], preferred_element_type=F32)

    @pl.when(f == pl.num_programs(1) - 1)
    def _():
        def store_out(rows, y):
            o_ref[rows, :] = h_ref[rows, :] + y

        _rms_norm_rows(o_ref, gpost_ref, nscale_ref, store_out, unroll=True)


def _ffn(h, g_pre, g_post, w_gate, w_up, w_down):
    n, d = h.shape
    d_ff = w_gate.shape[1]
    tm, tf = TM_FFN, TF_FFN
    return pl.pallas_call(
        _ffn_kernel,
        grid=(n // tm, d_ff // tf),
        in_specs=[
            pl.BlockSpec((tm, d), lambda i, f: (i, 0)),
            pl.BlockSpec((1, d), lambda i, f: (0, 0)),
            pl.BlockSpec((1, d), lambda i, f: (0, 0)),
            pl.BlockSpec((d, tf), lambda i, f: (0, f)),
            pl.BlockSpec((d, tf), lambda i, f: (0, f)),
            pl.BlockSpec((tf, d), lambda i, f: (f, 0)),
        ],
        out_specs=pl.BlockSpec((tm, d), lambda i, f: (i, 0)),
        out_shape=jax.ShapeDtypeStruct((n, d), F32),
        scratch_shapes=[pltpu.VMEM((tm, d), BF16), pltpu.VMEM((tm, 1), F32), pltpu.VMEM((tm, tf), BF16)],
        compiler_params=pltpu.CompilerParams(
            dimension_semantics=("arbitrary", "arbitrary"), vmem_limit_bytes=VMEM_LIMIT_BYTES),
        name="ffn",
    )(h, g_pre, g_post, w_gate, w_up, w_down)


def kernel(x, positions, pre_mix_norm, post_mix_norm, w_in, pool_w, pool_scale,
           lam_q1, lam_k1, lam_q2, lam_k2, subln_w, w_out,
           pre_ffn_norm, post_ffn_norm, w_gate, w_up, w_down):
    batch, seq, d_model = x.shape
    depth = w_in.shape[0]
    pool_width = pool_scale.shape[1]
    attn_width = w_out.shape[1] - pool_width
    qk_width = (w_in.shape[2] - pool_width - attn_width) // 2
    n_heads = attn_width // DIFF_V_DIM
    assert qk_width == n_heads * 2 * DIFF_HEAD_DIM
    assert seq % TQ == 0 and seq % TM_IN == 0 and (batch * seq) % TM_FFN == 0
    assert TM_IN == TK

    n = batch * seq
    h = x.reshape(n, d_model)
    pos_rows = positions.reshape(n // TM_IN, 1, TM_IN)
    inv_freq = (ROPE_THETA ** (-jnp.arange(0, ROT_DIM, 2, dtype=F32) / ROT_DIM)).reshape(ROT_HALF, 1)

    for l in range(depth):
        lambda_init = _lambda_init(l)
        pool_out, q, k, vt, w_out_bf = _in_proj(
            h, pos_rows, inv_freq, pre_mix_norm[l].reshape(1, -1), w_in[l], pool_w[l],
            pool_scale[l].reshape(1, -1), w_out[l],
            seq=seq, pool_width=pool_width, qk_width=qk_width, attn_width=attn_width)
        lam_vecs = jnp.stack([lam_q1[l], lam_k1[l], lam_q2[l], lam_k2[l]]).astype(F32)
        attn_out, (wg_bf, wu_bf, wd_bf) = _attention(
            q, k, vt, lam_vecs, subln_w[l].reshape(1, -1), (w_gate[l], w_up[l], w_down[l]),
            batch=batch, seq=seq, n_heads=n_heads, lambda_init=lambda_init)
        h = _out_proj(pool_out, attn_out, h, w_out_bf, post_mix_norm[l].reshape(1, -1))
        h = _ffn(h, pre_ffn_norm[l].reshape(1, -1), post_ffn_norm[l].reshape(1, -1), wg_bf, wu_bf, wd_bf)
    return h.reshape(batch, seq, d_model)
```

```python
import functools
import math

import jax
import jax.numpy as jnp
from jax import lax
from jax.experimental import pallas as pl
from jax.experimental.pallas import tpu as pltpu

F32 = jnp.float32
BF16 = jnp.bfloat16

POOL_WINDOWS = (2, 4, 8, 16)
DIFF_HEAD_DIM = 64
DIFF_V_DIM = 2 * DIFF_HEAD_DIM
ROPE_THETA = 500000.0
ROT_DIM = DIFF_HEAD_DIM // 4
ROT_HALF = ROT_DIM // 2
NORM_EPS = 1e-6
NEG_INF = -1e30
LOG2_E = math.log2(math.e)

LANES = 128
POOL_HALO = 32
VMEM_LIMIT_BYTES = 56 * 1024 * 1024

TM_IN = 512
TQ = 1024
TK = 512
ATTN_CHUNK = 256
ATTN_UNROLL = 6
TM_OUT = 512
TM_FFN = 1024
TF_FFN = 512
FFN_ACT_CHUNK = 256
N_CHUNK = 512
NORM_ROWS = 64
IN_CAST_STEPS = 8


def _lambda_init(layer_idx):
    return 0.8 - 0.6 * math.exp(-0.3 * layer_idx)


def _rms_norm(xf, g):
    ms = jnp.mean(xf * xf, axis=-1, keepdims=True)
    return xf * lax.rsqrt(ms + NORM_EPS) * g


def _rms_norm_rows(src_ref, g_ref, scale_ref, emit):
    n = src_ref.shape[0]

    def scale_body(c, carry):
        rows = pl.ds(pl.multiple_of(c * NORM_ROWS, NORM_ROWS), NORM_ROWS)
        x = src_ref[rows, :]
        scale_ref[rows, :] = lax.rsqrt(jnp.mean(x * x, axis=-1, keepdims=True) + NORM_EPS)
        return carry

    lax.fori_loop(0, n // NORM_ROWS, scale_body, 0, unroll=True)

    def apply_body(c, carry):
        rows = pl.ds(pl.multiple_of(c * NORM_ROWS, NORM_ROWS), NORM_ROWS)
        emit(rows, src_ref[rows, :] * scale_ref[rows, :] * g_ref[...])
        return carry

    lax.fori_loop(0, n // NORM_ROWS, apply_body, 0, unroll=True)


def _in_proj_kernel(x_ref, pos_ref, invf_ref, g_ref, w_ref, pw_ref, ps_ref, wo_ref,
                    pool_ref, q_ref, k_ref, vt_ref, wo_bf_ref,
                    w_bf_ref, wvt_ref, trig_ref, hn_ref, u_ref, carry_ref, *, seq, pool_width, qk_width):
    tm = x_ref.shape[0]
    step = pl.program_id(0)
    rows_per_cast = w_ref.shape[0]
    n_cast = w_bf_ref.shape[0] // rows_per_cast
    n_direct = w_bf_ref.shape[1]
    tiles_per_cast = pos_ref.shape[0]
    lane = lax.broadcasted_iota(jnp.int32, (1, LANES), 1) % DIFF_HEAD_DIM

    @pl.when(step < n_cast)
    def _():
        r0 = pl.multiple_of(step * rows_per_cast, rows_per_cast)
        w_bf_ref[pl.ds(r0, rows_per_cast), :] = w_ref[:, :n_direct].astype(BF16)
        for t in range(tiles_per_cast):
            ang = invf_ref[...] * pos_ref[t].astype(F32)
            comp = jnp.concatenate(
                [jnp.cos(ang), jnp.sin(ang), jnp.zeros((DIFF_HEAD_DIM - ROT_DIM, tm), F32)], axis=0)
            trig_ref[step * tiles_per_cast + t] = jnp.concatenate([comp, comp], axis=0).T

    for t in range(n_cast):
        @pl.when(step == t)
        def _():
            wvt_ref[:, t * rows_per_cast:(t + 1) * rows_per_cast] = w_ref[:, n_direct:].T.astype(BF16)

    @pl.when(step >= n_cast)
    def _():
        tiles_per_seq = seq // tm
        ti = (step - n_cast) % tiles_per_seq

        wo_bf_ref[...] = wo_ref[...].astype(BF16)
        hn_ref[...] = _rms_norm(x_ref[...], g_ref[...]).astype(BF16)

        @pl.when(ti == 0)
        def _():
            carry_ref[...] = jnp.zeros_like(carry_ref)

        trig = trig_ref[step - n_cast]
        coef_self = jnp.where(lane < ROT_HALF, trig,
                              jnp.where(lane < ROT_DIM, pltpu.roll(trig, ROT_HALF, 1), 1.0))
        coef_lo = jnp.where((lane >= ROT_HALF) & (lane < ROT_DIM), trig, 0.0)
        coef_hi = jnp.where(lane < ROT_HALF, -pltpu.roll(trig, LANES - ROT_HALF, 1), 0.0)

        def rope_chunk(out_ref, chunk, c, scale):
            t = jnp.dot(hn_ref[...], w_bf_ref[:, chunk * N_CHUNK:(chunk + 1) * N_CHUNK],
                        preferred_element_type=F32)
            for h in range(N_CHUNK // LANES):
                th = t[:, h * LANES:(h + 1) * LANES]
                r = (th * coef_self + pltpu.roll(th, ROT_HALF, 1) * coef_lo
                     + pltpu.roll(th, LANES - ROT_HALF, 1) * coef_hi)
                if scale != 1.0:
                    r = r * scale
                out_ref[:, c + h * LANES:c + (h + 1) * LANES] = r.astype(BF16)

        gdim = pool_width // len(POOL_WINDOWS)
        t_in_seq = ti * tm + lax.broadcasted_iota(jnp.int32, (tm, 1), 0)

        def pool_group(g, w):
            c0 = g * gdim
            u = u_ref[:, c0:c0 + gdim]
            ext = jnp.concatenate([carry_ref[g], u], axis=0)
            carry_ref[g] = u[tm - POOL_HALO:, :]
            lvl, off, k = ext, 0, 1
            while k < w:
                new_off = min(off + 8, POOL_HALO)
                cur = lvl[new_off - off:, :]
                shifted = lvl[new_off - off - k: lvl.shape[0] - k, :]
                lvl, off, k = cur + shifted, new_off, 2 * k
            wsum = lvl[POOL_HALO - off:, :]
            cnt = jnp.minimum(t_in_seq + 1, w).astype(F32)
            pooled = (wsum / cnt - u).astype(BF16)
            mixed = jnp.dot(pooled, pw_ref[g].astype(BF16), preferred_element_type=F32)
            pool_ref[:, c0:c0 + gdim] = (mixed * ps_ref[:, c0:c0 + gdim]).astype(BF16)

        n_pool, n_qk = pool_width // N_CHUNK, qk_width // N_CHUNK
        for ci in range(n_pool):
            u_ref[:, ci * N_CHUNK:(ci + 1) * N_CHUNK] = jnp.dot(
                hn_ref[...], w_bf_ref[:, ci * N_CHUNK:(ci + 1) * N_CHUNK], preferred_element_type=F32)

        q_scale = DIFF_HEAD_DIM ** -0.5 * LOG2_E
        chunks = ([(q_ref, n_pool + ci, ci * N_CHUNK, q_scale) for ci in range(n_qk)]
                  + [(k_ref, n_pool + n_qk + ci, ci * N_CHUNK, 1.0) for ci in range(n_qk)])
        groups = list(enumerate(POOL_WINDOWS))
        for idx, args in enumerate(chunks):
            rope_chunk(*args)
            for g, w in groups[idx * len(groups) // len(chunks):(idx + 1) * len(groups) // len(chunks)]:
                pool_group(g, w)

        for c in range(0, vt_ref.shape[0], N_CHUNK):
            vt_ref[c:c + N_CHUNK, :] = lax.dot_general(
                wvt_ref[c:c + N_CHUNK, :], hn_ref[...], (((1,), (1,)), ((), ())),
                preferred_element_type=F32).astype(BF16)


def _in_proj(x2, pos2, inv_freq, g, w_in, pool_w, pool_scale, w_out, *, seq, pool_width, qk_width, attn_width):
    n, d = x2.shape
    tm = TM_IN
    n_groups = len(POOL_WINDOWS)
    gdim = pool_width // n_groups
    n_direct = pool_width + 2 * qk_width
    n_cast = IN_CAST_STEPS
    n_tiles = n // tm
    assert n_tiles % n_cast == 0
    tiles_per_cast = n_tiles // n_cast
    const = lambda s: (0, 0)
    row = lambda s: (jnp.maximum(s - n_cast, 0), 0)
    kern = functools.partial(_in_proj_kernel, seq=seq, pool_width=pool_width, qk_width=qk_width)
    return pl.pallas_call(
        kern,
        grid=(n_cast + n_tiles,),
        in_specs=[
            pl.BlockSpec((tm, d), row),
            pl.BlockSpec((tiles_per_cast, 1, tm), lambda s: (jnp.minimum(s, n_cast - 1), 0, 0)),
            pl.BlockSpec((ROT_HALF, 1), const),
            pl.BlockSpec((1, d), const),
            pl.BlockSpec((d // n_cast, w_in.shape[1]), lambda s: (jnp.minimum(s, n_cast - 1), 0)),
            pl.BlockSpec(pool_w.shape, lambda s: (0, 0, 0)),
            pl.BlockSpec((1, pool_width), const),
            pl.BlockSpec((w_out.shape[0] // n_tiles, w_out.shape[1]), row),
        ],
        out_specs=[
            pl.BlockSpec((tm, pool_width), row),
            pl.BlockSpec((tm, qk_width), row),
            pl.BlockSpec((tm, qk_width), row),
            pl.BlockSpec((None, attn_width, tm), lambda s: (jnp.maximum(s - n_cast, 0), 0, 0)),
            pl.BlockSpec((w_out.shape[0] // n_tiles, w_out.shape[1]), row),
        ],
        out_shape=[
            jax.ShapeDtypeStruct((n, pool_width), BF16),
            jax.ShapeDtypeStruct((n, qk_width), BF16),
            jax.ShapeDtypeStruct((n, qk_width), BF16),
            jax.ShapeDtypeStruct((n // tm, attn_width, tm), BF16),
            jax.ShapeDtypeStruct(w_out.shape, BF16),
        ],
        scratch_shapes=[
            pltpu.VMEM((d, n_direct), BF16),
            pltpu.VMEM((attn_width, d), BF16),
            pltpu.VMEM((n_tiles, tm, LANES), F32),
            pltpu.VMEM((tm, d), BF16),
            pltpu.VMEM((tm, pool_width), F32),
            pltpu.VMEM((n_groups, POOL_HALO, gdim), F32),
        ],
        compiler_params=pltpu.CompilerParams(
            dimension_semantics=("arbitrary",), vmem_limit_bytes=VMEM_LIMIT_BYTES),
        name="in_proj",
    )(x2, pos2, inv_freq, g, w_in, pool_w, pool_scale, w_out)


def _attn_kernel(lam_ref, sw_ref, q_ref, k_ref, vt_ref, wg_ref, wu_ref, wd_ref,
                 o_ref, wg_bf_ref, wu_bf_ref, wd_bf_ref,
                 qc_ref, s0_ref, s1_ref, mx0_ref, mx1_ref, m_ref, l_ref, acc_ref, *, lambda_init):
    seq = q_ref.shape[0]
    n_tiles = seq // TQ
    n_unmasked = n_tiles * (n_tiles - 1)
    assert n_tiles >= 2 and n_unmasked % ATTN_UNROLL == 0 and ATTN_UNROLL % 2 == 0
    assert all(w.shape[0] % (16 * (n_unmasked // ATTN_UNROLL)) == 0 for w in (wg_ref, wu_ref, wd_ref))
    s_slots, mx_slots = (s0_ref, s1_ref), (mx0_ref, mx1_ref)
    chunks = [slice(a, a + ATTN_CHUNK) for a in range(0, TQ, ATTN_CHUNK)]
    chunk_plan = {
        "full": [(ch, None, TK) for ch in chunks],
        "lo": [(ch, ch.start, min(TK, ch.stop)) if ch.start < TK else (ch, None, TK) for ch in chunks],
        "hi": [(ch, ch.start - TK, min(TK, ch.stop - TK)) for ch in chunks if ch.start >= TK],
    }

    def rows(i, size):
        return pl.ds(i * size if isinstance(i, int) else pl.multiple_of(i * size, size), size)

    def key_rows(j, n_keys):
        start = j * TK if isinstance(j, int) else pl.multiple_of(j * TK, TK)
        return pl.ds(start, n_keys)

    def scores_chunk(qi, j, slot, c, ch, offset, n_keys):
        s = lax.dot_general(k_ref[key_rows(j, n_keys), :], qc_ref[qi, c, ch, :], (((1,), (1,)), ((), ())),
                            preferred_element_type=F32)
        if offset is not None:
            kv = lax.broadcasted_iota(jnp.int32, s.shape, 0)
            r = lax.broadcasted_iota(jnp.int32, s.shape, 1)
            s = jnp.where(kv <= r + offset, s, NEG_INF)
        s_slots[slot][c, 0:n_keys, ch] = s
        mx_slots[slot][c, :, ch] = jnp.max(s, axis=0, keepdims=True)

    def exp_pv_chunk(qi, j, slot, c, ch, n_keys):
        m_prev = m_ref[qi, c, :, ch]
        m_new = jnp.maximum(m_prev, mx_slots[slot][c, :, ch])
        alpha = jnp.exp2(m_prev - m_new)
        p = jnp.exp2(s_slots[slot][c, 0:n_keys, ch] - m_new)
        l_ref[qi, c, :, ch] = alpha * l_ref[qi, c, :, ch] + jnp.sum(p, axis=0, keepdims=True)
        pv = jnp.dot(vt_ref[j, :, 0:n_keys], p.astype(BF16), preferred_element_type=F32)
        acc_ref[qi, c, :, ch] = alpha * acc_ref[qi, c, :, ch] + pv
        m_ref[qi, c, :, ch] = m_new

    def stage(cur, cur_kind, slot, nxt, nxt_kind):
        todo_next = chunk_plan[nxt_kind] if nxt is not None else []
        todo_cur = [(ch, n_keys) for ch, _, n_keys in chunk_plan[cur_kind]]
        for c in range(2):
            for i in range(max(len(todo_next), len(todo_cur))):
                if i < len(todo_next):
                    scores_chunk(nxt[0], nxt[1], 1 - slot, c, *todo_next[i])
                if i < len(todo_cur):
                    exp_pv_chunk(cur[0], cur[1], slot, c, *todo_cur[i])

    def finalize(qi):
        lam_v = lam_ref[...]
        lam = (jnp.exp(jnp.sum(lam_v[0:1] * lam_v[1:2], keepdims=True))
               - jnp.exp(jnp.sum(lam_v[2:3] * lam_v[3:4], keepdims=True)) + lambda_init)
        attn_t = acc_ref[qi, 0] / l_ref[qi, 0] - lam * (acc_ref[qi, 1] / l_ref[qi, 1])
        ms = jnp.mean(attn_t * attn_t, axis=0, keepdims=True)
        y = (attn_t * lax.rsqrt(ms + NORM_EPS)).T
        o_ref[rows(qi, TQ), :] = (y * sw_ref[...] * (1.0 - lambda_init)).astype(BF16)

    lane = lax.broadcasted_iota(jnp.int32, (TQ, DIFF_V_DIM), 1)
    for qi in range(n_tiles):
        q = q_ref[qi * TQ:(qi + 1) * TQ, :]
        zero = jnp.zeros_like(q)
        qc_ref[qi, 0] = jnp.where(lane < DIFF_HEAD_DIM, q, zero)
        qc_ref[qi, 1] = jnp.where(lane >= DIFF_HEAD_DIM, q, zero)
    m_ref[...] = jnp.full_like(m_ref, NEG_INF)
    l_ref[...] = jnp.zeros_like(l_ref)
    acc_ref[...] = jnp.zeros_like(acc_ref)

    def unmasked_pair(n):
        qi = 1
        for t in range(2, n_tiles):
            qi = qi + jnp.where(n >= t * (t - 1), 1, 0)
        return qi, n - qi * (qi - 1)

    for c in range(2):
        for plan in chunk_plan["full"]:
            scores_chunk(1, 0, 0, c, *plan)

    n_iters = n_unmasked // ATTN_UNROLL

    def convert_weights(it):
        for src, dst in ((wg_ref, wg_bf_ref), (wu_ref, wu_bf_ref), (wd_ref, wd_bf_ref)):
            per = src.shape[0] // n_iters
            rows_it = pl.ds(pl.multiple_of(it * per, per), per)
            dst[rows_it, :] = src[rows_it, :].astype(BF16)

    def unmasked_body(it, carry):
        convert_weights(it)
        for u in range(ATTN_UNROLL):
            n = it * ATTN_UNROLL + u
            stage(unmasked_pair(n), "full", u % 2, unmasked_pair(jnp.minimum(n + 1, n_unmasked - 1)), "full")
        return carry

    lax.fori_loop(0, n_iters, unmasked_body, 0)

    def masked_stages(qi, nxt_qi):
        stage((qi, 2 * qi), "lo", 0, (qi, 2 * qi + 1), "hi")
        stage((qi, 2 * qi + 1), "hi", 1, (nxt_qi, 2 * nxt_qi), "lo")

    for c in range(2):
        for plan in chunk_plan["lo"]:
            scores_chunk(0, 0, 0, c, *plan)
    for qi in range(n_tiles):
        if qi > 0:
            finalize(qi - 1)
        masked_stages(qi, min(qi + 1, n_tiles - 1))
    finalize(n_tiles - 1)


def _attention(q, k, vt, lam_vecs, subln_w, ffn_weights, *, batch, seq, n_heads, lambda_init):
    n, width = q.shape
    nk = seq // TK
    n_tiles = seq // TQ
    n_steps = batch * n_heads
    assert TQ == 2 * TK and seq % TQ == 0 and vt.shape == (n // TK, n_heads * DIFF_V_DIM, TK)
    kern = functools.partial(_attn_kernel, lambda_init=lambda_init)
    head = lambda b, h: (b, h)
    step_rows = lambda b, h: (b * n_heads + h, 0)
    w_specs = [pl.BlockSpec((w.shape[0] // n_steps, w.shape[1]), step_rows) for w in ffn_weights]
    outs = pl.pallas_call(
        kern,
        grid=(batch, n_heads),
        in_specs=[
            pl.BlockSpec(lam_vecs.shape, lambda b, h: (0, 0)),
            pl.BlockSpec((1, DIFF_V_DIM), lambda b, h: (0, 0)),
            pl.BlockSpec((seq, DIFF_V_DIM), head),
            pl.BlockSpec((seq, DIFF_V_DIM), head),
            pl.BlockSpec((nk, DIFF_V_DIM, TK), lambda b, h: (b, h, 0)),
        ] + w_specs,
        out_specs=[pl.BlockSpec((seq, DIFF_V_DIM), head)] + w_specs,
        out_shape=[jax.ShapeDtypeStruct((n, width), BF16)]
        + [jax.ShapeDtypeStruct(w.shape, BF16) for w in ffn_weights],
        scratch_shapes=[
            pltpu.VMEM((n_tiles, 2, TQ, DIFF_V_DIM), BF16),
            pltpu.VMEM((2, TK, TQ), F32),
            pltpu.VMEM((2, TK, TQ), F32),
            pltpu.VMEM((2, 1, TQ), F32),
            pltpu.VMEM((2, 1, TQ), F32),
            pltpu.VMEM((n_tiles, 2, 1, TQ), F32),
            pltpu.VMEM((n_tiles, 2, 1, TQ), F32),
            pltpu.VMEM((n_tiles, 2, DIFF_V_DIM, TQ), F32),
        ],
        compiler_params=pltpu.CompilerParams(
            dimension_semantics=("arbitrary", "arbitrary"), vmem_limit_bytes=VMEM_LIMIT_BYTES),
        name="diff_attention",
    )(lam_vecs, subln_w, q, k, vt, *ffn_weights)
    return outs[0], outs[1:]


def _out_proj_kernel(pool_ref, attn_ref, x_ref, w_ref, g_ref, h_ref, mix_ref, nscale_ref):
    pw = pool_ref.shape[1]
    for c in range(0, mix_ref.shape[1], N_CHUNK):
        mix_ref[:, c:c + N_CHUNK] = (
            jnp.dot(pool_ref[...], w_ref[:pw, c:c + N_CHUNK], preferred_element_type=F32)
            + jnp.dot(attn_ref[...], w_ref[pw:, c:c + N_CHUNK], preferred_element_type=F32))

    def store_h(rows, y):
        h_ref[rows, :] = x_ref[rows, :] + y

    _rms_norm_rows(mix_ref, g_ref, nscale_ref, store_h)


def _out_proj(pool_out, attn_out, x2, w_out_bf, g):
    n, d = x2.shape
    tm = TM_OUT
    row = lambda i: (i, 0)
    const = lambda i: (0, 0)
    return pl.pallas_call(
        _out_proj_kernel,
        grid=(n // tm,),
        in_specs=[
            pl.BlockSpec((tm, pool_out.shape[1]), row),
            pl.BlockSpec((tm, attn_out.shape[1]), row),
            pl.BlockSpec((tm, d), row),
            pl.BlockSpec(w_out_bf.shape, const),
            pl.BlockSpec((1, d), const),
        ],
        out_specs=pl.BlockSpec((tm, d), row),
        out_shape=jax.ShapeDtypeStruct((n, d), F32),
        scratch_shapes=[
            pltpu.VMEM((tm, d), F32),
            pltpu.VMEM((tm, 1), F32),
        ],
        compiler_params=pltpu.CompilerParams(
            dimension_semantics=("arbitrary",), vmem_limit_bytes=VMEM_LIMIT_BYTES),
        name="out_proj",
    )(pool_out, attn_out, x2, w_out_bf, g)


def _ffn_kernel(h_ref, gpre_ref, gpost_ref, wg_ref, wu_ref, wd_ref, o_ref, hn_ref, nscale_ref, act_ref):
    f = pl.program_id(1)

    @pl.when(f == 0)
    def _():
        def store_hn(rows, y):
            hn_ref[rows, :] = y.astype(BF16)
            o_ref[rows, :] = jnp.zeros_like(y)

        _rms_norm_rows(h_ref, gpre_ref, nscale_ref, store_hn)

    hn = hn_ref[...]
    for cc in range(0, wg_ref.shape[1], FFN_ACT_CHUNK):
        gate = jnp.dot(hn, wg_ref[:, cc:cc + FFN_ACT_CHUNK], preferred_element_type=F32)
        up = jnp.dot(hn, wu_ref[:, cc:cc + FFN_ACT_CHUNK], preferred_element_type=F32)
        act_ref[:, cc:cc + FFN_ACT_CHUNK] = (gate * jax.nn.sigmoid(gate) * up).astype(BF16)
    for c in range(0, o_ref.shape[1], N_CHUNK):
        o_ref[:, c:c + N_CHUNK] += jnp.dot(act_ref[...], wd_ref[:, c:c + N_CHUNK], preferred_element_type=F32)

    @pl.when(f == pl.num_programs(1) - 1)
    def _():
        def store_out(rows, y):
            o_ref[rows, :] = h_ref[rows, :] + y

        _rms_norm_rows(o_ref, gpost_ref, nscale_ref, store_out)


def _ffn(h, g_pre, g_post, w_gate, w_up, w_down):
    n, d = h.shape
    d_ff = w_gate.shape[1]
    tm, tf = TM_FFN, TF_FFN
    return pl.pallas_call(
        _ffn_kernel,
        grid=(n // tm, d_ff // tf),
        in_specs=[
            pl.BlockSpec((tm, d), lambda i, f: (i, 0)),
            pl.BlockSpec((1, d), lambda i, f: (0, 0)),
            pl.BlockSpec((1, d), lambda i, f: (0, 0)),
            pl.BlockSpec((d, tf), lambda i, f: (0, f)),
            pl.BlockSpec((d, tf), lambda i, f: (0, f)),
            pl.BlockSpec((tf, d), lambda i, f: (f, 0)),
        ],
        out_specs=pl.BlockSpec((tm, d), lambda i, f: (i, 0)),
        out_shape=jax.ShapeDtypeStruct((n, d), F32),
        scratch_shapes=[pltpu.VMEM((tm, d), BF16), pltpu.VMEM((tm, 1), F32), pltpu.VMEM((tm, tf), BF16)],
        compiler_params=pltpu.CompilerParams(
            dimension_semantics=("arbitrary", "arbitrary"), vmem_limit_bytes=VMEM_LIMIT_BYTES),
        name="ffn",
    )(h, g_pre, g_post, w_gate, w_up, w_down)


def kernel(x, positions, pre_mix_norm, post_mix_norm, w_in, pool_w, pool_scale,
           lam_q1, lam_k1, lam_q2, lam_k2, subln_w, w_out,
           pre_ffn_norm, post_ffn_norm, w_gate, w_up, w_down):
    batch, seq, d_model = x.shape
    depth = w_in.shape[0]
    pool_width = pool_scale.shape[1]
    attn_width = w_out.shape[1] - pool_width
    qk_width = (w_in.shape[2] - pool_width - attn_width) // 2
    n_heads = attn_width // DIFF_V_DIM
    assert qk_width == n_heads * 2 * DIFF_HEAD_DIM
    assert seq % TQ == 0 and seq % TM_IN == 0 and (batch * seq) % TM_FFN == 0
    assert TM_IN == TK

    n = batch * seq
    h = x.reshape(n, d_model)
    pos_rows = positions.reshape(n // TM_IN, 1, TM_IN)
    inv_freq = (ROPE_THETA ** (-jnp.arange(0, ROT_DIM, 2, dtype=F32) / ROT_DIM)).reshape(ROT_HALF, 1)

    for l in range(depth):
        lambda_init = _lambda_init(l)
        pool_out, q, k, vt, w_out_bf = _in_proj(
            h, pos_rows, inv_freq, pre_mix_norm[l].reshape(1, -1), w_in[l], pool_w[l],
            pool_scale[l].reshape(1, -1), w_out[l],
            seq=seq, pool_width=pool_width, qk_width=qk_width, attn_width=attn_width)
        lam_vecs = jnp.stack([lam_q1[l], lam_k1[l], lam_q2[l], lam_k2[l]]).astype(F32)
        attn_out, (wg_bf, wu_bf, wd_bf) = _attention(
            q, k, vt, lam_vecs, subln_w[l].reshape(1, -1), (w_gate[l], w_up[l], w_down[l]),
            batch=batch, seq=seq, n_heads=n_heads, lambda_init=lambda_init)
        h = _out_proj(pool_out, attn_out, h, w_out_bf, post_mix_norm[l].reshape(1, -1))
        h = _ffn(h, pre_ffn_norm[l].reshape(1, -1), post_ffn_norm[l].reshape(1, -1), wg_bf, wu_bf, wd_bf)
    return h.reshape(batch, seq, d_model)
```

```python
import functools
import math

import jax
import jax.numpy as jnp
from jax import lax
from jax.experimental import pallas as pl
from jax.experimental.pallas import tpu as pltpu

F32 = jnp.float32
BF16 = jnp.bfloat16

POOL_WINDOWS = (2, 4, 8, 16)
DIFF_HEAD_DIM = 64
DIFF_V_DIM = 2 * DIFF_HEAD_DIM
ROPE_THETA = 500000.0
ROT_DIM = DIFF_HEAD_DIM // 4
ROT_HALF = ROT_DIM // 2
NORM_EPS = 1e-6
NEG_INF = -1e30
LOG2_E = math.log2(math.e)

LANES = 128
POOL_HALO = 32
VMEM_LIMIT_BYTES = 56 * 1024 * 1024

TM_IN = 512
TQ = 1024
TK = 512
ATTN_CHUNK = 256
ATTN_UNROLL = 12
TM_OUT = 512
TM_FFN = 1024
TF_FFN = 512
FFN_ACT_CHUNK = 256
N_CHUNK = 512
NORM_ROWS = 64
IN_CAST_STEPS = 8


def _lambda_init(layer_idx):
    return 0.8 - 0.6 * math.exp(-0.3 * layer_idx)


def _rms_norm(xf, g):
    ms = jnp.mean(xf * xf, axis=-1, keepdims=True)
    return xf * lax.rsqrt(ms + NORM_EPS) * g


def _rms_norm_rows(src_ref, g_ref, scale_ref, emit):
    n = src_ref.shape[0]

    def scale_body(c, carry):
        rows = pl.ds(pl.multiple_of(c * NORM_ROWS, NORM_ROWS), NORM_ROWS)
        x = src_ref[rows, :]
        scale_ref[rows, :] = lax.rsqrt(jnp.mean(x * x, axis=-1, keepdims=True) + NORM_EPS)
        return carry

    lax.fori_loop(0, n // NORM_ROWS, scale_body, 0, unroll=True)

    def apply_body(c, carry):
        rows = pl.ds(pl.multiple_of(c * NORM_ROWS, NORM_ROWS), NORM_ROWS)
        emit(rows, src_ref[rows, :] * scale_ref[rows, :] * g_ref[...])
        return carry

    lax.fori_loop(0, n // NORM_ROWS, apply_body, 0, unroll=True)


def _in_proj_kernel(x_ref, pos_ref, invf_ref, g_ref, w_ref, pw_ref, ps_ref, wo_ref,
                    pool_ref, q_ref, k_ref, vt_ref, wo_bf_ref,
                    w_bf_ref, wvt_ref, trig_ref, hn_ref, u_ref, carry_ref, *, seq, pool_width, qk_width):
    tm = x_ref.shape[0]
    step = pl.program_id(0)
    rows_per_cast = w_ref.shape[0]
    n_cast = w_bf_ref.shape[0] // rows_per_cast
    n_direct = w_bf_ref.shape[1]
    tiles_per_cast = pos_ref.shape[0]
    lane = lax.broadcasted_iota(jnp.int32, (1, LANES), 1) % DIFF_HEAD_DIM

    @pl.when(step < n_cast)
    def _():
        r0 = pl.multiple_of(step * rows_per_cast, rows_per_cast)
        w_bf_ref[pl.ds(r0, rows_per_cast), :] = w_ref[:, :n_direct].astype(BF16)
        for t in range(tiles_per_cast):
            ang = invf_ref[...] * pos_ref[t].astype(F32)
            comp = jnp.concatenate(
                [jnp.cos(ang), jnp.sin(ang), jnp.zeros((DIFF_HEAD_DIM - ROT_DIM, tm), F32)], axis=0)
            trig_ref[step * tiles_per_cast + t] = jnp.concatenate([comp, comp], axis=0).T

    for t in range(n_cast):
        @pl.when(step == t)
        def _():
            wvt_ref[:, t * rows_per_cast:(t + 1) * rows_per_cast] = w_ref[:, n_direct:].T.astype(BF16)

    @pl.when(step >= n_cast)
    def _():
        tiles_per_seq = seq // tm
        ti = (step - n_cast) % tiles_per_seq

        wo_bf_ref[...] = wo_ref[...].astype(BF16)
        hn_ref[...] = _rms_norm(x_ref[...], g_ref[...]).astype(BF16)

        @pl.when(ti == 0)
        def _():
            carry_ref[...] = jnp.zeros_like(carry_ref)

        trig = trig_ref[step - n_cast]
        coef_self = jnp.where(lane < ROT_HALF, trig,
                              jnp.where(lane < ROT_DIM, pltpu.roll(trig, ROT_HALF, 1), 1.0))
        coef_lo = jnp.where((lane >= ROT_HALF) & (lane < ROT_DIM), trig, 0.0)
        coef_hi = jnp.where(lane < ROT_HALF, -pltpu.roll(trig, LANES - ROT_HALF, 1), 0.0)

        def rope_chunk(out_ref, chunk, c, scale):
            t = jnp.dot(hn_ref[...], w_bf_ref[:, chunk * N_CHUNK:(chunk + 1) * N_CHUNK],
                        preferred_element_type=F32)
            for h in range(N_CHUNK // LANES):
                th = t[:, h * LANES:(h + 1) * LANES]
                r = (th * coef_self + pltpu.roll(th, ROT_HALF, 1) * coef_lo
                     + pltpu.roll(th, LANES - ROT_HALF, 1) * coef_hi)
                if scale != 1.0:
                    r = r * scale
                out_ref[:, c + h * LANES:c + (h + 1) * LANES] = r.astype(BF16)

        gdim = pool_width // len(POOL_WINDOWS)
        t_in_seq = ti * tm + lax.broadcasted_iota(jnp.int32, (tm, 1), 0)

        def pool_group(g, w):
            c0 = g * gdim
            u = u_ref[:, c0:c0 + gdim]
            ext = jnp.concatenate([carry_ref[g], u], axis=0)
            carry_ref[g] = u[tm - POOL_HALO:, :]
            lvl, off, k = ext, 0, 1
            while k < w:
                new_off = min(off + 8, POOL_HALO)
                cur = lvl[new_off - off:, :]
                shifted = lvl[new_off - off - k: lvl.shape[0] - k, :]
                lvl, off, k = cur + shifted, new_off, 2 * k
            wsum = lvl[POOL_HALO - off:, :]
            cnt = jnp.minimum(t_in_seq + 1, w).astype(F32)
            pooled = (wsum / cnt - u).astype(BF16)
            mixed = jnp.dot(pooled, pw_ref[g].astype(BF16), preferred_element_type=F32)
            pool_ref[:, c0:c0 + gdim] = (mixed * ps_ref[:, c0:c0 + gdim]).astype(BF16)

        n_pool, n_qk = pool_width // N_CHUNK, qk_width // N_CHUNK
        for ci in range(n_pool):
            u_ref[:, ci * N_CHUNK:(ci + 1) * N_CHUNK] = jnp.dot(
                hn_ref[...], w_bf_ref[:, ci * N_CHUNK:(ci + 1) * N_CHUNK], preferred_element_type=F32)

        q_scale = DIFF_HEAD_DIM ** -0.5 * LOG2_E
        chunks = ([(q_ref, n_pool + ci, ci * N_CHUNK, q_scale) for ci in range(n_qk)]
                  + [(k_ref, n_pool + n_qk + ci, ci * N_CHUNK, 1.0) for ci in range(n_qk)])
        groups = list(enumerate(POOL_WINDOWS))
        for idx, args in enumerate(chunks):
            rope_chunk(*args)
            for g, w in groups[idx * len(groups) // len(chunks):(idx + 1) * len(groups) // len(chunks)]:
                pool_group(g, w)

        for c in range(0, vt_ref.shape[0], N_CHUNK):
            vt_ref[c:c + N_CHUNK, :] = lax.dot_general(
                wvt_ref[c:c + N_CHUNK, :], hn_ref[...], (((1,), (1,)), ((), ())),
                preferred_element_type=F32).astype(BF16)


def _in_proj(x2, pos2, inv_freq, g, w_in, pool_w, pool_scale, w_out, *, seq, pool_width, qk_width, attn_width):
    n, d = x2.shape
    tm = TM_IN
    n_groups = len(POOL_WINDOWS)
    gdim = pool_width // n_groups
    n_direct = pool_width + 2 * qk_width
    n_cast = IN_CAST_STEPS
    n_tiles = n // tm
    assert n_tiles % n_cast == 0
    tiles_per_cast = n_tiles // n_cast
    const = lambda s: (0, 0)
    row = lambda s: (jnp.maximum(s - n_cast, 0), 0)
    kern = functools.partial(_in_proj_kernel, seq=seq, pool_width=pool_width, qk_width=qk_width)
    return pl.pallas_call(
        kern,
        grid=(n_cast + n_tiles,),
        in_specs=[
            pl.BlockSpec((tm, d), row),
            pl.BlockSpec((tiles_per_cast, 1, tm), lambda s: (jnp.minimum(s, n_cast - 1), 0, 0)),
            pl.BlockSpec((ROT_HALF, 1), const),
            pl.BlockSpec((1, d), const),
            pl.BlockSpec((d // n_cast, w_in.shape[1]), lambda s: (jnp.minimum(s, n_cast - 1), 0)),
            pl.BlockSpec(pool_w.shape, lambda s: (0, 0, 0)),
            pl.BlockSpec((1, pool_width), const),
            pl.BlockSpec((w_out.shape[0] // n_tiles, w_out.shape[1]), row),
        ],
        out_specs=[
            pl.BlockSpec((tm, pool_width), row),
            pl.BlockSpec((tm, qk_width), row),
            pl.BlockSpec((tm, qk_width), row),
            pl.BlockSpec((None, attn_width, tm), lambda s: (jnp.maximum(s - n_cast, 0), 0, 0)),
            pl.BlockSpec((w_out.shape[0] // n_tiles, w_out.shape[1]), row),
        ],
        out_shape=[
            jax.ShapeDtypeStruct((n, pool_width), BF16),
            jax.ShapeDtypeStruct((n, qk_width), BF16),
            jax.ShapeDtypeStruct((n, qk_width), BF16),
            jax.ShapeDtypeStruct((n // tm, attn_width, tm), BF16),
            jax.ShapeDtypeStruct(w_out.shape, BF16),
        ],
        scratch_shapes=[
            pltpu.VMEM((d, n_direct), BF16),
            pltpu.VMEM((attn_width, d), BF16),
            pltpu.VMEM((n_tiles, tm, LANES), F32),
            pltpu.VMEM((tm, d), BF16),
            pltpu.VMEM((tm, pool_width), F32),
            pltpu.VMEM((n_groups, POOL_HALO, gdim), F32),
        ],
        compiler_params=pltpu.CompilerParams(
            dimension_semantics=("arbitrary",), vmem_limit_bytes=VMEM_LIMIT_BYTES),
        name="in_proj",
    )(x2, pos2, inv_freq, g, w_in, pool_w, pool_scale, w_out)


def _attn_kernel(lam_ref, sw_ref, q_ref, k_ref, vt_ref, wg_ref, wu_ref, wd_ref,
                 o_ref, wg_bf_ref, wu_bf_ref, wd_bf_ref,
                 qc_ref, s0_ref, s1_ref, mx0_ref, mx1_ref, m_ref, l_ref, acc_ref, *, lambda_init):
    seq = q_ref.shape[0]
    n_tiles = seq // TQ
    n_unmasked = n_tiles * (n_tiles - 1)
    assert n_tiles >= 2 and n_unmasked % ATTN_UNROLL == 0 and ATTN_UNROLL % 2 == 0
    assert all(w.shape[0] % (16 * (n_unmasked // ATTN_UNROLL)) == 0 for w in (wg_ref, wu_ref, wd_ref))
    s_slots, mx_slots = (s0_ref, s1_ref), (mx0_ref, mx1_ref)
    chunks = [slice(a, a + ATTN_CHUNK) for a in range(0, TQ, ATTN_CHUNK)]
    chunk_plan = {
        "full": [(ch, None, TK) for ch in chunks],
        "lo": [(ch, ch.start, min(TK, ch.stop)) if ch.start < TK else (ch, None, TK) for ch in chunks],
        "hi": [(ch, ch.start - TK, min(TK, ch.stop - TK)) for ch in chunks if ch.start >= TK],
    }

    def rows(i, size):
        return pl.ds(i * size if isinstance(i, int) else pl.multiple_of(i * size, size), size)

    def key_rows(j, n_keys):
        start = j * TK if isinstance(j, int) else pl.multiple_of(j * TK, TK)
        return pl.ds(start, n_keys)

    def scores_chunk(qi, j, slot, c, ch, offset, n_keys):
        s = lax.dot_general(k_ref[key_rows(j, n_keys), :], qc_ref[qi, c, ch, :], (((1,), (1,)), ((), ())),
                            preferred_element_type=F32)
        if offset is not None:
            kv = lax.broadcasted_iota(jnp.int32, s.shape, 0)
            r = lax.broadcasted_iota(jnp.int32, s.shape, 1)
            s = jnp.where(kv <= r + offset, s, NEG_INF)
        s_slots[slot][c, 0:n_keys, ch] = s
        mx_slots[slot][c, :, ch] = jnp.max(s, axis=0, keepdims=True)

    def exp_pv_chunk(qi, j, slot, c, ch, n_keys):
        m_prev = m_ref[qi, c, :, ch]
        m_new = jnp.maximum(m_prev, mx_slots[slot][c, :, ch])
        alpha = jnp.exp2(m_prev - m_new)
        p = jnp.exp2(s_slots[slot][c, 0:n_keys, ch] - m_new)
        l_ref[qi, c, :, ch] = alpha * l_ref[qi, c, :, ch] + jnp.sum(p, axis=0, keepdims=True)
        pv = jnp.dot(vt_ref[j, :, 0:n_keys], p.astype(BF16), preferred_element_type=F32)
        acc_ref[qi, c, :, ch] = alpha * acc_ref[qi, c, :, ch] + pv
        m_ref[qi, c, :, ch] = m_new

    def stage(cur, cur_kind, slot, nxt, nxt_kind):
        todo_next = chunk_plan[nxt_kind] if nxt is not None else []
        todo_cur = [(ch, n_keys) for ch, _, n_keys in chunk_plan[cur_kind]]
        for c in range(2):
            for i in range(max(len(todo_next), len(todo_cur))):
                if i < len(todo_next):
                    scores_chunk(nxt[0], nxt[1], 1 - slot, c, *todo_next[i])
                if i < len(todo_cur):
                    exp_pv_chunk(cur[0], cur[1], slot, c, *todo_cur[i])

    def finalize(qi):
        lam_v = lam_ref[...]
        lam = (jnp.exp(jnp.sum(lam_v[0:1] * lam_v[1:2], keepdims=True))
               - jnp.exp(jnp.sum(lam_v[2:3] * lam_v[3:4], keepdims=True)) + lambda_init)
        attn_t = acc_ref[qi, 0] / l_ref[qi, 0] - lam * (acc_ref[qi, 1] / l_ref[qi, 1])
        ms = jnp.mean(attn_t * attn_t, axis=0, keepdims=True)
        y = (attn_t * lax.rsqrt(ms + NORM_EPS)).T
        o_ref[rows(qi, TQ), :] = (y * sw_ref[...] * (1.0 - lambda_init)).astype(BF16)

    lane = lax.broadcasted_iota(jnp.int32, (TQ, DIFF_V_DIM), 1)
    for qi in range(n_tiles):
        q = q_ref[qi * TQ:(qi + 1) * TQ, :]
        zero = jnp.zeros_like(q)
        qc_ref[qi, 0] = jnp.where(lane < DIFF_HEAD_DIM, q, zero)
        qc_ref[qi, 1] = jnp.where(lane >= DIFF_HEAD_DIM, q, zero)
    m_ref[...] = jnp.full_like(m_ref, NEG_INF)
    l_ref[...] = jnp.zeros_like(l_ref)
    acc_ref[...] = jnp.zeros_like(acc_ref)

    def unmasked_pair(n):
        qi = 1
        for t in range(2, n_tiles):
            qi = qi + jnp.where(n >= t * (t - 1), 1, 0)
        return qi, n - qi * (qi - 1)

    for c in range(2):
        for plan in chunk_plan["full"]:
            scores_chunk(1, 0, 0, c, *plan)

    n_iters = n_unmasked // ATTN_UNROLL

    def convert_weights(it):
        for src, dst in ((wg_ref, wg_bf_ref), (wu_ref, wu_bf_ref), (wd_ref, wd_bf_ref)):
            per = src.shape[0] // n_iters
            rows_it = rows(it, per)
            dst[rows_it, :] = src[rows_it, :].astype(BF16)

    def unmasked_body(it, carry):
        convert_weights(it)
        for u in range(ATTN_UNROLL):
            n = it * ATTN_UNROLL + u
            stage(unmasked_pair(n), "full", u % 2, unmasked_pair(jnp.minimum(n + 1, n_unmasked - 1)), "full")
        return carry

    lax.fori_loop(0, n_iters, unmasked_body, 0)

    def masked_stages(qi, nxt_qi):
        stage((qi, 2 * qi), "lo", 0, (qi, 2 * qi + 1), "hi")
        stage((qi, 2 * qi + 1), "hi", 1, (nxt_qi, 2 * nxt_qi), "lo")

    for c in range(2):
        for plan in chunk_plan["lo"]:
            scores_chunk(0, 0, 0, c, *plan)
    for qi in range(n_tiles):
        if qi > 0:
            finalize(qi - 1)
        masked_stages(qi, min(qi + 1, n_tiles - 1))
    finalize(n_tiles - 1)


def _attention(q, k, vt, lam_vecs, subln_w, ffn_weights, *, batch, seq, n_heads, lambda_init):
    n, width = q.shape
    nk = seq // TK
    n_tiles = seq // TQ
    n_steps = batch * n_heads
    assert TQ == 2 * TK and seq % TQ == 0 and vt.shape == (n // TK, n_heads * DIFF_V_DIM, TK)
    kern = functools.partial(_attn_kernel, lambda_init=lambda_init)
    head = lambda b, h: (b, h)
    step_rows = lambda b, h: (b * n_heads + h, 0)
    w_specs = [pl.BlockSpec((w.shape[0] // n_steps, w.shape[1]), step_rows) for w in ffn_weights]
    outs = pl.pallas_call(
        kern,
        grid=(batch, n_heads),
        in_specs=[
            pl.BlockSpec(lam_vecs.shape, lambda b, h: (0, 0)),
            pl.BlockSpec((1, DIFF_V_DIM), lambda b, h: (0, 0)),
            pl.BlockSpec((seq, DIFF_V_DIM), head),
            pl.BlockSpec((seq, DIFF_V_DIM), head),
            pl.BlockSpec((nk, DIFF_V_DIM, TK), lambda b, h: (b, h, 0)),
        ] + w_specs,
        out_specs=[pl.BlockSpec((seq, DIFF_V_DIM), head)] + w_specs,
        out_shape=[jax.ShapeDtypeStruct((n, width), BF16)]
        + [jax.ShapeDtypeStruct(w.shape, BF16) for w in ffn_weights],
        scratch_shapes=[
            pltpu.VMEM((n_tiles, 2, TQ, DIFF_V_DIM), BF16),
            pltpu.VMEM((2, TK, TQ), F32),
            pltpu.VMEM((2, TK, TQ), F32),
            pltpu.VMEM((2, 1, TQ), F32),
            pltpu.VMEM((2, 1, TQ), F32),
            pltpu.VMEM((n_tiles, 2, 1, TQ), F32),
            pltpu.VMEM((n_tiles, 2, 1, TQ), F32),
            pltpu.VMEM((n_tiles, 2, DIFF_V_DIM, TQ), F32),
        ],
        compiler_params=pltpu.CompilerParams(
            dimension_semantics=("arbitrary", "arbitrary"), vmem_limit_bytes=VMEM_LIMIT_BYTES),
        name="diff_attention",
    )(lam_vecs, subln_w, q, k, vt, *ffn_weights)
    return outs[0], outs[1:]


def _out_proj_kernel(pool_ref, attn_ref, x_ref, w_ref, g_ref, h_ref, mix_ref, nscale_ref):
    pw = pool_ref.shape[1]
    for c in range(0, mix_ref.shape[1], N_CHUNK):
        mix_ref[:, c:c + N_CHUNK] = (
            jnp.dot(pool_ref[...], w_ref[:pw, c:c + N_CHUNK], preferred_element_type=F32)
            + jnp.dot(attn_ref[...], w_ref[pw:, c:c + N_CHUNK], preferred_element_type=F32))

    def store_h(rows, y):
        h_ref[rows, :] = x_ref[rows, :] + y

    _rms_norm_rows(mix_ref, g_ref, nscale_ref, store_h)


def _out_proj(pool_out, attn_out, x2, w_out_bf, g):
    n, d = x2.shape
    tm = TM_OUT
    row = lambda i: (i, 0)
    const = lambda i: (0, 0)
    return pl.pallas_call(
        _out_proj_kernel,
        grid=(n // tm,),
        in_specs=[
            pl.BlockSpec((tm, pool_out.shape[1]), row),
            pl.BlockSpec((tm, attn_out.shape[1]), row),
            pl.BlockSpec((tm, d), row),
            pl.BlockSpec(w_out_bf.shape, const),
            pl.BlockSpec((1, d), const),
        ],
        out_specs=pl.BlockSpec((tm, d), row),
        out_shape=jax.ShapeDtypeStruct((n, d), F32),
        scratch_shapes=[
            pltpu.VMEM((tm, d), F32),
            pltpu.VMEM((tm, 1), F32),
        ],
        compiler_params=pltpu.CompilerParams(
            dimension_semantics=("arbitrary",), vmem_limit_bytes=VMEM_LIMIT_BYTES),
        name="out_proj",
    )(pool_out, attn_out, x2, w_out_bf, g)


def _ffn_kernel(h_ref, gpre_ref, gpost_ref, wg_ref, wu_ref, wd_ref, o_ref, hn_ref, nscale_ref, act_ref):
    f = pl.program_id(1)

    @pl.when(f == 0)
    def _():
        def store_hn(rows, y):
            hn_ref[rows, :] = y.astype(BF16)
            o_ref[rows, :] = jnp.zeros_like(y)

        _rms_norm_rows(h_ref, gpre_ref, nscale_ref, store_hn)

    hn = hn_ref[...]
    for cc in range(0, wg_ref.shape[1], FFN_ACT_CHUNK):
        gate = jnp.dot(hn, wg_ref[:, cc:cc + FFN_ACT_CHUNK], preferred_element_type=F32)
        up = jnp.dot(hn, wu_ref[:, cc:cc + FFN_ACT_CHUNK], preferred_element_type=F32)
        act_ref[:, cc:cc + FFN_ACT_CHUNK] = (gate * jax.nn.sigmoid(gate) * up).astype(BF16)
    for c in range(0, o_ref.shape[1], N_CHUNK):
        o_ref[:, c:c + N_CHUNK] += jnp.dot(act_ref[...], wd_ref[:, c:c + N_CHUNK], preferred_element_type=F32)

    @pl.when(f == pl.num_programs(1) - 1)
    def _():
        def store_out(rows, y):
            o_ref[rows, :] = h_ref[rows, :] + y

        _rms_norm_rows(o_ref, gpost_ref, nscale_ref, store_out)


def _ffn(h, g_pre, g_post, w_gate, w_up, w_down):
    n, d = h.shape
    d_ff = w_gate.shape[1]
    tm, tf = TM_FFN, TF_FFN
    return pl.pallas_call(
        _ffn_kernel,
        grid=(n // tm, d_ff // tf),
        in_specs=[
            pl.BlockSpec((tm, d), lambda i, f: (i, 0)),
            pl.BlockSpec((1, d), lambda i, f: (0, 0)),
            pl.BlockSpec((1, d), lambda i, f: (0, 0)),
            pl.BlockSpec((d, tf), lambda i, f: (0, f)),
            pl.BlockSpec((d, tf), lambda i, f: (0, f)),
            pl.BlockSpec((tf, d), lambda i, f: (f, 0)),
        ],
        out_specs=pl.BlockSpec((tm, d), lambda i, f: (i, 0)),
        out_shape=jax.ShapeDtypeStruct((n, d), F32),
        scratch_shapes=[pltpu.VMEM((tm, d), BF16), pltpu.VMEM((tm, 1), F32), pltpu.VMEM((tm, tf), BF16)],
        compiler_params=pltpu.CompilerParams(
            dimension_semantics=("arbitrary", "arbitrary"), vmem_limit_bytes=VMEM_LIMIT_BYTES),
        name="ffn",
    )(h, g_pre, g_post, w_gate, w_up, w_down)


def kernel(x, positions, pre_mix_norm, post_mix_norm, w_in, pool_w, pool_scale,
           lam_q1, lam_k1, lam_q2, lam_k2, subln_w, w_out,
           pre_ffn_norm, post_ffn_norm, w_gate, w_up, w_down):
    batch, seq, d_model = x.shape
    depth = w_in.shape[0]
    pool_width = pool_scale.shape[1]
    attn_width = w_out.shape[1] - pool_width
    qk_width = (w_in.shape[2] - pool_width - attn_width) // 2
    n_heads = attn_width // DIFF_V_DIM
    assert qk_width == n_heads * 2 * DIFF_HEAD_DIM
    assert seq % TQ == 0 and seq % TM_IN == 0 and (batch * seq) % TM_FFN == 0
    assert TM_IN == TK

    n = batch * seq
    h = x.reshape(n, d_model)
    pos_rows = positions.reshape(n // TM_IN, 1, TM_IN)
    inv_freq = (ROPE_THETA ** (-jnp.arange(0, ROT_DIM, 2, dtype=F32) / ROT_DIM)).reshape(ROT_HALF, 1)

    for l in range(depth):
        lambda_init = _lambda_init(l)
        pool_out, q, k, vt, w_out_bf = _in_proj(
            h, pos_rows, inv_freq, pre_mix_norm[l].reshape(1, -1), w_in[l], pool_w[l],
            pool_scale[l].reshape(1, -1), w_out[l],
            seq=seq, pool_width=pool_width, qk_width=qk_width, attn_width=attn_width)
        lam_vecs = jnp.stack([lam_q1[l], lam_k1[l], lam_q2[l], lam_k2[l]]).astype(F32)
        attn_out, (wg_bf, wu_bf, wd_bf) = _attention(
            q, k, vt, lam_vecs, subln_w[l].reshape(1, -1), (w_gate[l], w_up[l], w_down[l]),
            batch=batch, seq=seq, n_heads=n_heads, lambda_init=lambda_init)
        h = _out_proj(pool_out, attn_out, h, w_out_bf, post_mix_norm[l].reshape(1, -1))
        h = _ffn(h, pre_ffn_norm[l].reshape(1, -1), post_ffn_norm[l].reshape(1, -1), wg_bf, wu_bf, wd_bf)
    return h.reshape(batch, seq, d_model)
```

```python
import functools
import math

import jax
import jax.numpy as jnp
from jax import lax
from jax.experimental import pallas as pl
from jax.experimental.pallas import tpu as pltpu

F32 = jnp.float32
BF16 = jnp.bfloat16

POOL_WINDOWS = (2, 4, 8, 16)
DIFF_HEAD_DIM = 64
DIFF_V_DIM = 2 * DIFF_HEAD_DIM
ROPE_THETA = 500000.0
ROT_DIM = DIFF_HEAD_DIM // 4
ROT_HALF = ROT_DIM // 2
NORM_EPS = 1e-6
NEG_INF = -1e30
LOG2_E = math.log2(math.e)

LANES = 128
BF16_ROWS = 16
POOL_HALO = 32
VMEM_LIMIT_BYTES = 56 * 1024 * 1024

TM_IN = 512
TQ = 1024
TK = 512
ATTN_CHUNK = 256
ATTN_UNROLL = 6
TM_OUT = 512
TM_FFN = 1024
TF_FFN = 512
FFN_ACT_CHUNK = 256
N_CHUNK = 512
NORM_ROWS = 128
IN_CAST_STEPS = 8


def _lambda_init(layer_idx):
    return 0.8 - 0.6 * math.exp(-0.3 * layer_idx)


def _rms_norm(xf, g):
    ms = jnp.mean(xf * xf, axis=-1, keepdims=True)
    return xf * lax.rsqrt(ms + NORM_EPS) * g


def _rms_norm_rows(src_ref, g_ref, scale_ref, emit):
    n = src_ref.shape[0]

    def scale_body(c, carry):
        rows = pl.ds(pl.multiple_of(c * NORM_ROWS, NORM_ROWS), NORM_ROWS)
        x = src_ref[rows, :]
        scale_ref[rows, :] = lax.rsqrt(jnp.mean(x * x, axis=-1, keepdims=True) + NORM_EPS)
        return carry

    lax.fori_loop(0, n // NORM_ROWS, scale_body, 0, unroll=True)

    def apply_body(c, carry):
        rows = pl.ds(pl.multiple_of(c * NORM_ROWS, NORM_ROWS), NORM_ROWS)
        emit(rows, src_ref[rows, :] * scale_ref[rows, :] * g_ref[...])
        return carry

    lax.fori_loop(0, n // NORM_ROWS, apply_body, 0, unroll=True)


def _in_proj_kernel(x_ref, pos_ref, invf_ref, g_ref, w_ref, pw_ref, ps_ref, wo_ref,
                    pool_ref, q_ref, k_ref, vt_ref, wo_bf_ref,
                    w_bf_ref, wvt_ref, trig_ref, hn_ref, u_ref, carry_ref, *, seq, pool_width, qk_width):
    tm = x_ref.shape[0]
    step = pl.program_id(0)
    rows_per_cast = w_ref.shape[0]
    n_cast = w_bf_ref.shape[0] // rows_per_cast
    n_direct = w_bf_ref.shape[1]
    tiles_per_cast = pos_ref.shape[0]
    lane = lax.broadcasted_iota(jnp.int32, (1, LANES), 1) % DIFF_HEAD_DIM

    @pl.when(step < n_cast)
    def _():
        r0 = pl.multiple_of(step * rows_per_cast, rows_per_cast)
        w_bf_ref[pl.ds(r0, rows_per_cast), :] = w_ref[:, :n_direct].astype(BF16)
        for t in range(tiles_per_cast):
            ang = invf_ref[...] * pos_ref[t].astype(F32)
            comp = jnp.concatenate(
                [jnp.cos(ang), jnp.sin(ang), jnp.zeros((DIFF_HEAD_DIM - ROT_DIM, tm), F32)], axis=0)
            trig_ref[step * tiles_per_cast + t] = jnp.concatenate([comp, comp], axis=0).T

    for t in range(n_cast):
        @pl.when(step == t)
        def _():
            wvt_ref[:, t * rows_per_cast:(t + 1) * rows_per_cast] = w_ref[:, n_direct:].T.astype(BF16)

    @pl.when(step >= n_cast)
    def _():
        tiles_per_seq = seq // tm
        ti = (step - n_cast) % tiles_per_seq

        wo_bf_ref[...] = wo_ref[...].astype(BF16)
        hn_ref[...] = _rms_norm(x_ref[...], g_ref[...]).astype(BF16)

        @pl.when(ti == 0)
        def _():
            carry_ref[...] = jnp.zeros_like(carry_ref)

        trig = trig_ref[step - n_cast]
        coef_self = jnp.where(lane < ROT_HALF, trig,
                              jnp.where(lane < ROT_DIM, pltpu.roll(trig, ROT_HALF, 1), 1.0))
        coef_lo = jnp.where((lane >= ROT_HALF) & (lane < ROT_DIM), trig, 0.0)
        coef_hi = jnp.where(lane < ROT_HALF, -pltpu.roll(trig, LANES - ROT_HALF, 1), 0.0)

        def rope_chunk(out_ref, chunk, c, scale):
            t = jnp.dot(hn_ref[...], w_bf_ref[:, chunk * N_CHUNK:(chunk + 1) * N_CHUNK],
                        preferred_element_type=F32)
            for h in range(N_CHUNK // LANES):
                th = t[:, h * LANES:(h + 1) * LANES]
                r = (th * coef_self + pltpu.roll(th, ROT_HALF, 1) * coef_lo
                     + pltpu.roll(th, LANES - ROT_HALF, 1) * coef_hi)
                if scale != 1.0:
                    r = r * scale
                out_ref[:, c + h * LANES:c + (h + 1) * LANES] = r.astype(BF16)

        gdim = pool_width // len(POOL_WINDOWS)
        t_in_seq = ti * tm + lax.broadcasted_iota(jnp.int32, (tm, 1), 0)

        def pool_group(g, w):
            c0 = g * gdim
            u = u_ref[:, c0:c0 + gdim]
            ext = jnp.concatenate([carry_ref[g], u], axis=0)
            carry_ref[g] = u[tm - POOL_HALO:, :]
            lvl, off, k = ext, 0, 1
            while k < w:
                new_off = min(off + 8, POOL_HALO)
                cur = lvl[new_off - off:, :]
                shifted = lvl[new_off - off - k: lvl.shape[0] - k, :]
                lvl, off, k = cur + shifted, new_off, 2 * k
            wsum = lvl[POOL_HALO - off:, :]
            cnt = jnp.minimum(t_in_seq + 1, w).astype(F32)
            pooled = (wsum / cnt - u).astype(BF16)
            mixed = jnp.dot(pooled, pw_ref[g].astype(BF16), preferred_element_type=F32)
            pool_ref[:, c0:c0 + gdim] = (mixed * ps_ref[:, c0:c0 + gdim]).astype(BF16)

        n_pool, n_qk = pool_width // N_CHUNK, qk_width // N_CHUNK
        for ci in range(n_pool):
            u_ref[:, ci * N_CHUNK:(ci + 1) * N_CHUNK] = jnp.dot(
                hn_ref[...], w_bf_ref[:, ci * N_CHUNK:(ci + 1) * N_CHUNK], preferred_element_type=F32)

        q_scale = DIFF_HEAD_DIM ** -0.5 * LOG2_E
        chunks = ([(q_ref, n_pool + ci, ci * N_CHUNK, q_scale) for ci in range(n_qk)]
                  + [(k_ref, n_pool + n_qk + ci, ci * N_CHUNK, 1.0) for ci in range(n_qk)])
        groups = list(enumerate(POOL_WINDOWS))
        for idx, args in enumerate(chunks):
            rope_chunk(*args)
            for g, w in groups[idx * len(groups) // len(chunks):(idx + 1) * len(groups) // len(chunks)]:
                pool_group(g, w)

        for c in range(0, vt_ref.shape[0], N_CHUNK):
            vt_ref[c:c + N_CHUNK, :] = lax.dot_general(
                wvt_ref[c:c + N_CHUNK, :], hn_ref[...], (((1,), (1,)), ((), ())),
                preferred_element_type=F32).astype(BF16)


def _in_proj(x2, pos2, inv_freq, g, w_in, pool_w, pool_scale, w_out, *, seq, pool_width, qk_width, attn_width):
    n, d = x2.shape
    tm = TM_IN
    n_groups = len(POOL_WINDOWS)
    gdim = pool_width // n_groups
    n_direct = pool_width + 2 * qk_width
    n_cast = IN_CAST_STEPS
    n_tiles = n // tm
    assert n_tiles % n_cast == 0
    tiles_per_cast = n_tiles // n_cast
    const = lambda s: (0, 0)
    row = lambda s: (jnp.maximum(s - n_cast, 0), 0)
    kern = functools.partial(_in_proj_kernel, seq=seq, pool_width=pool_width, qk_width=qk_width)
    return pl.pallas_call(
        kern,
        grid=(n_cast + n_tiles,),
        in_specs=[
            pl.BlockSpec((tm, d), row),
            pl.BlockSpec((tiles_per_cast, 1, tm), lambda s: (jnp.minimum(s, n_cast - 1), 0, 0)),
            pl.BlockSpec((ROT_HALF, 1), const),
            pl.BlockSpec((1, d), const),
            pl.BlockSpec((d // n_cast, w_in.shape[1]), lambda s: (jnp.minimum(s, n_cast - 1), 0)),
            pl.BlockSpec(pool_w.shape, lambda s: (0, 0, 0)),
            pl.BlockSpec((1, pool_width), const),
            pl.BlockSpec((w_out.shape[0] // n_tiles, w_out.shape[1]), row),
        ],
        out_specs=[
            pl.BlockSpec((tm, pool_width), row),
            pl.BlockSpec((tm, qk_width), row),
            pl.BlockSpec((tm, qk_width), row),
            pl.BlockSpec((None, attn_width, tm), lambda s: (jnp.maximum(s - n_cast, 0), 0, 0)),
            pl.BlockSpec((w_out.shape[0] // n_tiles, w_out.shape[1]), row),
        ],
        out_shape=[
            jax.ShapeDtypeStruct((n, pool_width), BF16),
            jax.ShapeDtypeStruct((n, qk_width), BF16),
            jax.ShapeDtypeStruct((n, qk_width), BF16),
            jax.ShapeDtypeStruct((n // tm, attn_width, tm), BF16),
            jax.ShapeDtypeStruct(w_out.shape, BF16),
        ],
        scratch_shapes=[
            pltpu.VMEM((d, n_direct), BF16),
            pltpu.VMEM((attn_width, d), BF16),
            pltpu.VMEM((n_tiles, tm, LANES), F32),
            pltpu.VMEM((tm, d), BF16),
            pltpu.VMEM((tm, pool_width), F32),
            pltpu.VMEM((n_groups, POOL_HALO, gdim), F32),
        ],
        compiler_params=pltpu.CompilerParams(
            dimension_semantics=("arbitrary",), vmem_limit_bytes=VMEM_LIMIT_BYTES),
        name="in_proj",
    )(x2, pos2, inv_freq, g, w_in, pool_w, pool_scale, w_out)


def _attn_kernel(lam_ref, sw_ref, q_ref, k_ref, vt_ref, wg_ref, wu_ref, wd_ref,
                 o_ref, wg_bf_ref, wu_bf_ref, wd_bf_ref,
                 qc_ref, s0_ref, s1_ref, mx0_ref, mx1_ref, m_ref, l_ref, acc_ref, *, lambda_init):
    seq = q_ref.shape[0]
    n_tiles = seq // TQ
    n_unmasked = n_tiles * (n_tiles - 1)
    assert n_tiles >= 2 and n_unmasked % ATTN_UNROLL == 0 and ATTN_UNROLL % 2 == 0
    assert all(w.shape[0] % (BF16_ROWS * (n_unmasked // ATTN_UNROLL)) == 0 for w in (wg_ref, wu_ref, wd_ref))
    s_slots, mx_slots = (s0_ref, s1_ref), (mx0_ref, mx1_ref)
    chunks = [slice(a, a + ATTN_CHUNK) for a in range(0, TQ, ATTN_CHUNK)]
    chunk_plan = {
        "full": [(ch, None, TK) for ch in chunks],
        "lo": [(ch, ch.start, min(TK, ch.stop)) if ch.start < TK else (ch, None, TK) for ch in chunks],
        "hi": [(ch, ch.start - TK, min(TK, ch.stop - TK)) for ch in chunks if ch.start >= TK],
    }

    def rows(i, size):
        return pl.ds(i * size if isinstance(i, int) else pl.multiple_of(i * size, size), size)

    def key_rows(j, n_keys):
        start = j * TK if isinstance(j, int) else pl.multiple_of(j * TK, TK)
        return pl.ds(start, n_keys)

    def scores_chunk(qi, j, slot, c, ch, offset, n_keys):
        s = lax.dot_general(k_ref[key_rows(j, n_keys), :], qc_ref[qi, c, ch, :], (((1,), (1,)), ((), ())),
                            preferred_element_type=F32)
        if offset is not None:
            kv = lax.broadcasted_iota(jnp.int32, s.shape, 0)
            r = lax.broadcasted_iota(jnp.int32, s.shape, 1)
            s = jnp.where(kv <= r + offset, s, NEG_INF)
        s_slots[slot][c, 0:n_keys, ch] = s
        mx_slots[slot][c, :, ch] = jnp.max(s, axis=0, keepdims=True)

    def exp_pv_chunk(qi, j, slot, c, ch, n_keys):
        m_prev = m_ref[qi, c, :, ch]
        m_new = jnp.maximum(m_prev, mx_slots[slot][c, :, ch])
        alpha = jnp.exp2(m_prev - m_new)
        p = jnp.exp2(s_slots[slot][c, 0:n_keys, ch] - m_new)
        l_ref[qi, c, :, ch] = alpha * l_ref[qi, c, :, ch] + jnp.sum(p, axis=0, keepdims=True)
        pv = jnp.dot(vt_ref[j, :, 0:n_keys], p.astype(BF16), preferred_element_type=F32)
        acc_ref[qi, c, :, ch] = alpha * acc_ref[qi, c, :, ch] + pv
        m_ref[qi, c, :, ch] = m_new

    def stage(cur, cur_kind, slot, nxt, nxt_kind):
        todo_next = chunk_plan[nxt_kind] if nxt is not None else []
        todo_cur = [(ch, n_keys) for ch, _, n_keys in chunk_plan[cur_kind]]
        for c in range(2):
            for i in range(max(len(todo_next), len(todo_cur))):
                if i < len(todo_next):
                    scores_chunk(nxt[0], nxt[1], 1 - slot, c, *todo_next[i])
                if i < len(todo_cur):
                    exp_pv_chunk(cur[0], cur[1], slot, c, *todo_cur[i])

    def finalize(qi):
        lam_v = lam_ref[...]
        lam = (jnp.exp(jnp.sum(lam_v[0:1] * lam_v[1:2], keepdims=True))
               - jnp.exp(jnp.sum(lam_v[2:3] * lam_v[3:4], keepdims=True)) + lambda_init)
        attn_t = acc_ref[qi, 0] / l_ref[qi, 0] - lam * (acc_ref[qi, 1] / l_ref[qi, 1])
        ms = jnp.mean(attn_t * attn_t, axis=0, keepdims=True)
        y = (attn_t * lax.rsqrt(ms + NORM_EPS)).T
        o_ref[rows(qi, TQ), :] = (y * sw_ref[...] * (1.0 - lambda_init)).astype(BF16)

    lane = lax.broadcasted_iota(jnp.int32, (TQ, DIFF_V_DIM), 1)
    for qi in range(n_tiles):
        q = q_ref[qi * TQ:(qi + 1) * TQ, :]
        zero = jnp.zeros_like(q)
        qc_ref[qi, 0] = jnp.where(lane < DIFF_HEAD_DIM, q, zero)
        qc_ref[qi, 1] = jnp.where(lane >= DIFF_HEAD_DIM, q, zero)
    m_ref[...] = jnp.full_like(m_ref, NEG_INF)
    l_ref[...] = jnp.zeros_like(l_ref)
    acc_ref[...] = jnp.zeros_like(acc_ref)

    def unmasked_pair(n):
        qi = 1
        for t in range(2, n_tiles):
            qi = qi + jnp.where(n >= t * (t - 1), 1, 0)
        return qi, n - qi * (qi - 1)

    for c in range(2):
        for plan in chunk_plan["full"]:
            scores_chunk(1, 0, 0, c, *plan)

    n_iters = n_unmasked // ATTN_UNROLL

    def convert_weights(it):
        for src, dst in ((wg_ref, wg_bf_ref), (wu_ref, wu_bf_ref), (wd_ref, wd_bf_ref)):
            per = src.shape[0] // n_iters
            rows_it = pl.ds(pl.multiple_of(it * per, per), per)
            dst[rows_it, :] = src[rows_it, :].astype(BF16)

    def unmasked_body(it, carry):
        convert_weights(it)
        for u in range(ATTN_UNROLL):
            n = it * ATTN_UNROLL + u
            stage(unmasked_pair(n), "full", u % 2, unmasked_pair(jnp.minimum(n + 1, n_unmasked - 1)), "full")
        return carry

    lax.fori_loop(0, n_iters, unmasked_body, 0)

    def masked_stages(qi, nxt_qi):
        stage((qi, 2 * qi), "lo", 0, (qi, 2 * qi + 1), "hi")
        stage((qi, 2 * qi + 1), "hi", 1, (nxt_qi, 2 * nxt_qi), "lo")

    for c in range(2):
        for plan in chunk_plan["lo"]:
            scores_chunk(0, 0, 0, c, *plan)
    for qi in range(n_tiles):
        if qi > 0:
            finalize(qi - 1)
        masked_stages(qi, min(qi + 1, n_tiles - 1))
    finalize(n_tiles - 1)


def _attention(q, k, vt, lam_vecs, subln_w, ffn_weights, *, batch, seq, n_heads, lambda_init):
    n, width = q.shape
    nk = seq // TK
    n_tiles = seq // TQ
    n_steps = batch * n_heads
    assert TQ == 2 * TK and seq % TQ == 0 and vt.shape == (n // TK, n_heads * DIFF_V_DIM, TK)
    kern = functools.partial(_attn_kernel, lambda_init=lambda_init)
    head = lambda b, h: (b, h)
    step_rows = lambda b, h: (b * n_heads + h, 0)
    w_specs = [pl.BlockSpec((w.shape[0] // n_steps, w.shape[1]), step_rows) for w in ffn_weights]
    outs = pl.pallas_call(
        kern,
        grid=(batch, n_heads),
        in_specs=[
            pl.BlockSpec(lam_vecs.shape, lambda b, h: (0, 0)),
            pl.BlockSpec((1, DIFF_V_DIM), lambda b, h: (0, 0)),
            pl.BlockSpec((seq, DIFF_V_DIM), head),
            pl.BlockSpec((seq, DIFF_V_DIM), head),
            pl.BlockSpec((nk, DIFF_V_DIM, TK), lambda b, h: (b, h, 0)),
        ] + w_specs,
        out_specs=[pl.BlockSpec((seq, DIFF_V_DIM), head)] + w_specs,
        out_shape=[jax.ShapeDtypeStruct((n, width), BF16)]
        + [jax.ShapeDtypeStruct(w.shape, BF16) for w in ffn_weights],
        scratch_shapes=[
            pltpu.VMEM((n_tiles, 2, TQ, DIFF_V_DIM), BF16),
            pltpu.VMEM((2, TK, TQ), F32),
            pltpu.VMEM((2, TK, TQ), F32),
            pltpu.VMEM((2, 1, TQ), F32),
            pltpu.VMEM((2, 1, TQ), F32),
            pltpu.VMEM((n_tiles, 2, 1, TQ), F32),
            pltpu.VMEM((n_tiles, 2, 1, TQ), F32),
            pltpu.VMEM((n_tiles, 2, DIFF_V_DIM, TQ), F32),
        ],
        compiler_params=pltpu.CompilerParams(
            dimension_semantics=("arbitrary", "arbitrary"), vmem_limit_bytes=VMEM_LIMIT_BYTES),
        name="diff_attention",
    )(lam_vecs, subln_w, q, k, vt, *ffn_weights)
    return outs[0], outs[1:]


def _out_proj_kernel(pool_ref, attn_ref, x_ref, w_ref, g_ref, h_ref, mix_ref, nscale_ref):
    pw = pool_ref.shape[1]
    for c in range(0, mix_ref.shape[1], N_CHUNK):
        mix_ref[:, c:c + N_CHUNK] = (
            jnp.dot(pool_ref[...], w_ref[:pw, c:c + N_CHUNK], preferred_element_type=F32)
            + jnp.dot(attn_ref[...], w_ref[pw:, c:c + N_CHUNK], preferred_element_type=F32))

    def store_h(rows, y):
        h_ref[rows, :] = x_ref[rows, :] + y

    _rms_norm_rows(mix_ref, g_ref, nscale_ref, store_h)


def _out_proj(pool_out, attn_out, x2, w_out_bf, g):
    n, d = x2.shape
    tm = TM_OUT
    row = lambda i: (i, 0)
    const = lambda i: (0, 0)
    return pl.pallas_call(
        _out_proj_kernel,
        grid=(n // tm,),
        in_specs=[
            pl.BlockSpec((tm, pool_out.shape[1]), row),
            pl.BlockSpec((tm, attn_out.shape[1]), row),
            pl.BlockSpec((tm, d), row),
            pl.BlockSpec(w_out_bf.shape, const),
            pl.BlockSpec((1, d), const),
        ],
        out_specs=pl.BlockSpec((tm, d), row),
        out_shape=jax.ShapeDtypeStruct((n, d), F32),
        scratch_shapes=[
            pltpu.VMEM((tm, d), F32),
            pltpu.VMEM((tm, 1), F32),
        ],
        compiler_params=pltpu.CompilerParams(
            dimension_semantics=("arbitrary",), vmem_limit_bytes=VMEM_LIMIT_BYTES),
        name="out_proj",
    )(pool_out, attn_out, x2, w_out_bf, g)


def _ffn_kernel(h_ref, gpre_ref, gpost_ref, wg_ref, wu_ref, wd_ref, o_ref, hn_ref, nscale_ref, act_ref):
    f = pl.program_id(1)

    @pl.when(f == 0)
    def _():
        def store_hn(rows, y):
            hn_ref[rows, :] = y.astype(BF16)
            o_ref[rows, :] = jnp.zeros_like(y)

        _rms_norm_rows(h_ref, gpre_ref, nscale_ref, store_hn)

    hn = hn_ref[...]
    for cc in range(0, wg_ref.shape[1], FFN_ACT_CHUNK):
        gate = jnp.dot(hn, wg_ref[:, cc:cc + FFN_ACT_CHUNK], preferred_element_type=F32)
        up = jnp.dot(hn, wu_ref[:, cc:cc + FFN_ACT_CHUNK], preferred_element_type=F32)
        act_ref[:, cc:cc + FFN_ACT_CHUNK] = (gate * jax.nn.sigmoid(gate) * up).astype(BF16)
    for c in range(0, o_ref.shape[1], N_CHUNK):
        o_ref[:, c:c + N_CHUNK] += jnp.dot(act_ref[...], wd_ref[:, c:c + N_CHUNK], preferred_element_type=F32)

    @pl.when(f == pl.num_programs(1) - 1)
    def _():
        def store_out(rows, y):
            o_ref[rows, :] = h_ref[rows, :] + y

        _rms_norm_rows(o_ref, gpost_ref, nscale_ref, store_out)


def _ffn(h, g_pre, g_post, w_gate, w_up, w_down):
    n, d = h.shape
    d_ff = w_gate.shape[1]
    tm, tf = TM_FFN, TF_FFN
    return pl.pallas_call(
        _ffn_kernel,
        grid=(n // tm, d_ff // tf),
        in_specs=[
            pl.BlockSpec((tm, d), lambda i, f: (i, 0)),
            pl.BlockSpec((1, d), lambda i, f: (0, 0)),
            pl.BlockSpec((1, d), lambda i, f: (0, 0)),
            pl.BlockSpec((d, tf), lambda i, f: (0, f)),
            pl.BlockSpec((d, tf), lambda i, f: (0, f)),
            pl.BlockSpec((tf, d), lambda i, f: (f, 0)),
        ],
        out_specs=pl.BlockSpec((tm, d), lambda i, f: (i, 0)),
        out_shape=jax.ShapeDtypeStruct((n, d), F32),
        scratch_shapes=[pltpu.VMEM((tm, d), BF16), pltpu.VMEM((tm, 1), F32), pltpu.VMEM((tm, tf), BF16)],
        compiler_params=pltpu.CompilerParams(
            dimension_semantics=("arbitrary", "arbitrary"), vmem_limit_bytes=VMEM_LIMIT_BYTES),
        name="ffn",
    )(h, g_pre, g_post, w_gate, w_up, w_down)


def kernel(x, positions, pre_mix_norm, post_mix_norm, w_in, pool_w, pool_scale,
           lam_q1, lam_k1, lam_q2, lam_k2, subln_w, w_out,
           pre_ffn_norm, post_ffn_norm, w_gate, w_up, w_down):
    batch, seq, d_model = x.shape
    depth = w_in.shape[0]
    pool_width = pool_scale.shape[1]
    attn_width = w_out.shape[1] - pool_width
    qk_width = (w_in.shape[2] - pool_width - attn_width) // 2
    n_heads = attn_width // DIFF_V_DIM
    assert qk_width == n_heads * 2 * DIFF_HEAD_DIM
    assert seq % TQ == 0 and seq % TM_IN == 0 and (batch * seq) % TM_FFN == 0
    assert TM_IN == TK

    n = batch * seq
    h = x.reshape(n, d_model)
    pos_rows = positions.reshape(n // TM_IN, 1, TM_IN)
    inv_freq = (ROPE_THETA ** (-jnp.arange(0, ROT_DIM, 2, dtype=F32) / ROT_DIM)).reshape(ROT_HALF, 1)

    for l in range(depth):
        lambda_init = _lambda_init(l)
        pool_out, q, k, vt, w_out_bf = _in_proj(
            h, pos_rows, inv_freq, pre_mix_norm[l].reshape(1, -1), w_in[l], pool_w[l],
            pool_scale[l].reshape(1, -1), w_out[l],
            seq=seq, pool_width=pool_width, qk_width=qk_width, attn_width=attn_width)
        lam_vecs = jnp.stack([lam_q1[l], lam_k1[l], lam_q2[l], lam_k2[l]]).astype(F32)
        attn_out, (wg_bf, wu_bf, wd_bf) = _attention(
            q, k, vt, lam_vecs, subln_w[l].reshape(1, -1), (w_gate[l], w_up[l], w_down[l]),
            batch=batch, seq=seq, n_heads=n_heads, lambda_init=lambda_init)
        h = _out_proj(pool_out, attn_out, h, w_out_bf, post_mix_norm[l].reshape(1, -1))
        h = _ffn(h, pre_ffn_norm[l].reshape(1, -1), post_ffn_norm[l].reshape(1, -1), wg_bf, wu_bf, wd_bf)
    return h.reshape(batch, seq, d_model)
```

```python
import functools
import math

import jax
import jax.numpy as jnp
from jax import lax
from jax.experimental import pallas as pl
from jax.experimental.pallas import tpu as pltpu

F32 = jnp.float32
BF16 = jnp.bfloat16

POOL_WINDOWS = (2, 4, 8, 16)
DIFF_HEAD_DIM = 64
DIFF_V_DIM = 2 * DIFF_HEAD_DIM
ROPE_THETA = 500000.0
ROT_DIM = DIFF_HEAD_DIM // 4
ROT_HALF = ROT_DIM // 2
NORM_EPS = 1e-6
NEG_INF = -1e30
LOG2_E = math.log2(math.e)

LANES = 128
BF16_ROWS = 16
POOL_HALO = 32
VMEM_LIMIT_BYTES = 56 * 1024 * 1024

TM_IN = 512
TQ = 1024
TK = 512
ATTN_CHUNK = 256
ATTN_UNROLL = 6
TM_OUT = 512
TM_FFN = 1024
TF_FFN = 512
FFN_ACT_CHUNK = 256
N_CHUNK = 512
NORM_ROWS = 128
IN_CAST_STEPS = 8


def _lambda_init(layer_idx):
    return 0.8 - 0.6 * math.exp(-0.3 * layer_idx)


def _rms_norm(xf, g):
    ms = jnp.mean(xf * xf, axis=-1, keepdims=True)
    return xf * lax.rsqrt(ms + NORM_EPS) * g


def _rms_norm_rows(src_ref, g_ref, scale_ref, emit):
    n = src_ref.shape[0]

    def scale_body(c, carry):
        rows = pl.ds(pl.multiple_of(c * NORM_ROWS, NORM_ROWS), NORM_ROWS)
        x = src_ref[rows, :]
        scale_ref[rows, :] = lax.rsqrt(jnp.mean(x * x, axis=-1, keepdims=True) + NORM_EPS)
        return carry

    lax.fori_loop(0, n // NORM_ROWS, scale_body, 0, unroll=True)

    def apply_body(c, carry):
        rows = pl.ds(pl.multiple_of(c * NORM_ROWS, NORM_ROWS), NORM_ROWS)
        emit(rows, src_ref[rows, :] * scale_ref[rows, :] * g_ref[...])
        return carry

    lax.fori_loop(0, n // NORM_ROWS, apply_body, 0, unroll=True)


def _in_proj_kernel(x_ref, pos_ref, invf_ref, g_ref, w_ref, pw_ref, ps_ref, wo_ref,
                    pool_ref, q_ref, k_ref, vt_ref, wo_bf_ref,
                    w_bf_ref, wvt_ref, trig_ref, hn_ref, u_ref, carry_ref, *, seq, pool_width, qk_width):
    tm = x_ref.shape[0]
    step = pl.program_id(0)
    rows_per_cast = w_ref.shape[0]
    n_cast = w_bf_ref.shape[0] // rows_per_cast
    n_direct = w_bf_ref.shape[1]
    tiles_per_cast = pos_ref.shape[0]
    lane = lax.broadcasted_iota(jnp.int32, (1, LANES), 1) % DIFF_HEAD_DIM

    @pl.when(step < n_cast)
    def _():
        r0 = pl.multiple_of(step * rows_per_cast, rows_per_cast)
        w_bf_ref[pl.ds(r0, rows_per_cast), :] = w_ref[:, :n_direct].astype(BF16)
        for t in range(tiles_per_cast):
            ang = invf_ref[...] * pos_ref[t].astype(F32)
            comp = jnp.concatenate(
                [jnp.cos(ang), jnp.sin(ang), jnp.zeros((DIFF_HEAD_DIM - ROT_DIM, tm), F32)], axis=0)
            trig_ref[step * tiles_per_cast + t] = jnp.concatenate([comp, comp], axis=0).T

    for t in range(n_cast):
        @pl.when(step == t)
        def _():
            wvt_ref[:, t * rows_per_cast:(t + 1) * rows_per_cast] = w_ref[:, n_direct:].T.astype(BF16)

    @pl.when(step >= n_cast)
    def _():
        tiles_per_seq = seq // tm
        ti = (step - n_cast) % tiles_per_seq

        wo_bf_ref[...] = wo_ref[...].astype(BF16)
        hn_ref[...] = _rms_norm(x_ref[...], g_ref[...]).astype(BF16)

        @pl.when(ti == 0)
        def _():
            carry_ref[...] = jnp.zeros_like(carry_ref)

        trig = trig_ref[step - n_cast]
        coef_self = jnp.where(lane < ROT_HALF, trig,
                              jnp.where(lane < ROT_DIM, pltpu.roll(trig, ROT_HALF, 1), 1.0))
        coef_lo = jnp.where((lane >= ROT_HALF) & (lane < ROT_DIM), trig, 0.0)
        coef_hi = jnp.where(lane < ROT_HALF, -pltpu.roll(trig, LANES - ROT_HALF, 1), 0.0)

        def rope_chunk(out_ref, chunk, c, scale):
            t = jnp.dot(hn_ref[...], w_bf_ref[:, chunk * N_CHUNK:(chunk + 1) * N_CHUNK],
                        preferred_element_type=F32)
            for h in range(N_CHUNK // LANES):
                th = t[:, h * LANES:(h + 1) * LANES]
                r = (th * coef_self + pltpu.roll(th, ROT_HALF, 1) * coef_lo
                     + pltpu.roll(th, LANES - ROT_HALF, 1) * coef_hi)
                if scale != 1.0:
                    r = r * scale
                out_ref[:, c + h * LANES:c + (h + 1) * LANES] = r.astype(BF16)

        gdim = pool_width // len(POOL_WINDOWS)
        t_in_seq = ti * tm + lax.broadcasted_iota(jnp.int32, (tm, 1), 0)

        def pool_group(g, w):
            c0 = g * gdim
            u = u_ref[:, c0:c0 + gdim]
            ext = jnp.concatenate([carry_ref[g], u], axis=0)
            carry_ref[g] = u[tm - POOL_HALO:, :]
            lvl, off, k = ext, 0, 1
            while k < w:
                new_off = min(off + 8, POOL_HALO)
                cur = lvl[new_off - off:, :]
                shifted = lvl[new_off - off - k: lvl.shape[0] - k, :]
                lvl, off, k = cur + shifted, new_off, 2 * k
            wsum = lvl[POOL_HALO - off:, :]
            cnt = jnp.minimum(t_in_seq + 1, w).astype(F32)
            pooled = (wsum / cnt - u).astype(BF16)
            mixed = jnp.dot(pooled, pw_ref[g].astype(BF16), preferred_element_type=F32)
            pool_ref[:, c0:c0 + gdim] = (mixed * ps_ref[:, c0:c0 + gdim]).astype(BF16)

        n_pool, n_qk = pool_width // N_CHUNK, qk_width // N_CHUNK
        for ci in range(n_pool):
            u_ref[:, ci * N_CHUNK:(ci + 1) * N_CHUNK] = jnp.dot(
                hn_ref[...], w_bf_ref[:, ci * N_CHUNK:(ci + 1) * N_CHUNK], preferred_element_type=F32)

        q_scale = DIFF_HEAD_DIM ** -0.5 * LOG2_E
        chunks = ([(q_ref, n_pool + ci, ci * N_CHUNK, q_scale) for ci in range(n_qk)]
                  + [(k_ref, n_pool + n_qk + ci, ci * N_CHUNK, 1.0) for ci in range(n_qk)])
        groups = list(enumerate(POOL_WINDOWS))
        for idx, args in enumerate(chunks):
            rope_chunk(*args)
            for g, w in groups[idx * len(groups) // len(chunks):(idx + 1) * len(groups) // len(chunks)]:
                pool_group(g, w)

        for c in range(0, vt_ref.shape[0], N_CHUNK):
            vt_ref[c:c + N_CHUNK, :] = lax.dot_general(
                wvt_ref[c:c + N_CHUNK, :], hn_ref[...], (((1,), (1,)), ((), ())),
                preferred_element_type=F32).astype(BF16)


def _in_proj(x2, pos2, inv_freq, g, w_in, pool_w, pool_scale, w_out, *, seq, pool_width, qk_width, attn_width):
    n, d = x2.shape
    tm = TM_IN
    n_groups = len(POOL_WINDOWS)
    gdim = pool_width // n_groups
    n_direct = pool_width + 2 * qk_width
    n_cast = IN_CAST_STEPS
    n_tiles = n // tm
    assert n_tiles % n_cast == 0
    tiles_per_cast = n_tiles // n_cast
    const = lambda s: (0, 0)
    row = lambda s: (jnp.maximum(s - n_cast, 0), 0)
    kern = functools.partial(_in_proj_kernel, seq=seq, pool_width=pool_width, qk_width=qk_width)
    return pl.pallas_call(
        kern,
        grid=(n_cast + n_tiles,),
        in_specs=[
            pl.BlockSpec((tm, d), row),
            pl.BlockSpec((tiles_per_cast, 1, tm), lambda s: (jnp.minimum(s, n_cast - 1), 0, 0)),
            pl.BlockSpec((ROT_HALF, 1), const),
            pl.BlockSpec((1, d), const),
            pl.BlockSpec((d // n_cast, w_in.shape[1]), lambda s: (jnp.minimum(s, n_cast - 1), 0)),
            pl.BlockSpec(pool_w.shape, lambda s: (0, 0, 0)),
            pl.BlockSpec((1, pool_width), const),
            pl.BlockSpec((w_out.shape[0] // n_tiles, w_out.shape[1]), row),
        ],
        out_specs=[
            pl.BlockSpec((tm, pool_width), row),
            pl.BlockSpec((tm, qk_width), row),
            pl.BlockSpec((tm, qk_width), row),
            pl.BlockSpec((None, attn_width, tm), lambda s: (jnp.maximum(s - n_cast, 0), 0, 0)),
            pl.BlockSpec((w_out.shape[0] // n_tiles, w_out.shape[1]), row),
        ],
        out_shape=[
            jax.ShapeDtypeStruct((n, pool_width), BF16),
            jax.ShapeDtypeStruct((n, qk_width), BF16),
            jax.ShapeDtypeStruct((n, qk_width), BF16),
            jax.ShapeDtypeStruct((n // tm, attn_width, tm), BF16),
            jax.ShapeDtypeStruct(w_out.shape, BF16),
        ],
        scratch_shapes=[
            pltpu.VMEM((d, n_direct), BF16),
            pltpu.VMEM((attn_width, d), BF16),
            pltpu.VMEM((n_tiles, tm, LANES), F32),
            pltpu.VMEM((tm, d), BF16),
            pltpu.VMEM((tm, pool_width), F32),
            pltpu.VMEM((n_groups, POOL_HALO, gdim), F32),
        ],
        compiler_params=pltpu.CompilerParams(
            dimension_semantics=("arbitrary",), vmem_limit_bytes=VMEM_LIMIT_BYTES),
        name="in_proj",
    )(x2, pos2, inv_freq, g, w_in, pool_w, pool_scale, w_out)


def _attn_kernel(lam_ref, sw_ref, q_ref, k_ref, vt_ref, wg_ref, wu_ref, wd_ref,
                 o_ref, wg_bf_ref, wu_bf_ref, wd_bf_ref,
                 qc_ref, s0_ref, s1_ref, mx0_ref, mx1_ref, m_ref, l_ref, acc_ref, *, lambda_init):
    seq = q_ref.shape[0]
    n_tiles = seq // TQ
    n_unmasked = n_tiles * (n_tiles - 1)
    assert n_tiles >= 2 and n_unmasked % ATTN_UNROLL == 0 and ATTN_UNROLL % 2 == 0
    assert all(w.shape[0] % (BF16_ROWS * (n_unmasked // ATTN_UNROLL)) == 0 for w in (wg_ref, wu_ref, wd_ref))
    s_slots, mx_slots = (s0_ref, s1_ref), (mx0_ref, mx1_ref)
    chunks = [slice(a, a + ATTN_CHUNK) for a in range(0, TQ, ATTN_CHUNK)]
    chunk_plan = {
        "full": [(ch, None, TK) for ch in chunks],
        "lo": [(ch, ch.start, min(TK, ch.stop)) if ch.start < TK else (ch, None, TK) for ch in chunks],
        "hi": [(ch, ch.start - TK, min(TK, ch.stop - TK)) for ch in chunks if ch.start >= TK],
    }

    def rows(i, size):
        return pl.ds(i * size if isinstance(i, int) else pl.multiple_of(i * size, size), size)

    def key_rows(j, n_keys):
        start = j * TK if isinstance(j, int) else pl.multiple_of(j * TK, TK)
        return pl.ds(start, n_keys)

    def scores_chunk(qi, j, slot, c, ch, offset, n_keys):
        s = lax.dot_general(k_ref[key_rows(j, n_keys), :], qc_ref[qi, c, ch, :], (((1,), (1,)), ((), ())),
                            preferred_element_type=F32)
        if offset is not None:
            kv = lax.broadcasted_iota(jnp.int32, s.shape, 0)
            r = lax.broadcasted_iota(jnp.int32, s.shape, 1)
            s = jnp.where(kv <= r + offset, s, NEG_INF)
        s_slots[slot][c, 0:n_keys, ch] = s
        mx_slots[slot][c, :, ch] = jnp.max(s, axis=0, keepdims=True)

    def exp_pv_chunk(qi, j, slot, c, ch, n_keys):
        m_prev = m_ref[qi, c, :, ch]
        m_new = jnp.maximum(m_prev, mx_slots[slot][c, :, ch])
        alpha = jnp.exp2(m_prev - m_new)
        p = jnp.exp2(s_slots[slot][c, 0:n_keys, ch] - m_new)
        l_ref[qi, c, :, ch] = alpha * l_ref[qi, c, :, ch] + jnp.sum(p, axis=0, keepdims=True)
        pv = jnp.dot(vt_ref[j, :, 0:n_keys], p.astype(BF16), preferred_element_type=F32)
        acc_ref[qi, c, :, ch] = alpha * acc_ref[qi, c, :, ch] + pv
        m_ref[qi, c, :, ch] = m_new

    def stage(cur, cur_kind, slot, nxt, nxt_kind):
        todo_next = chunk_plan[nxt_kind] if nxt is not None else []
        todo_cur = [(ch, n_keys) for ch, _, n_keys in chunk_plan[cur_kind]]
        for c in range(2):
            for i in range(max(len(todo_next), len(todo_cur))):
                if i < len(todo_next):
                    scores_chunk(nxt[0], nxt[1], 1 - slot, c, *todo_next[i])
                if i < len(todo_cur):
                    exp_pv_chunk(cur[0], cur[1], slot, c, *todo_cur[i])

    def finalize(qi):
        lam_v = lam_ref[...]
        lam = (jnp.exp(jnp.sum(lam_v[0:1] * lam_v[1:2], keepdims=True))
               - jnp.exp(jnp.sum(lam_v[2:3] * lam_v[3:4], keepdims=True)) + lambda_init)
        attn_t = acc_ref[qi, 0] / l_ref[qi, 0] - lam * (acc_ref[qi, 1] / l_ref[qi, 1])
        ms = jnp.mean(attn_t * attn_t, axis=0, keepdims=True)
        y = (attn_t * lax.rsqrt(ms + NORM_EPS)).T
        o_ref[rows(qi, TQ), :] = (y * sw_ref[...] * (1.0 - lambda_init)).astype(BF16)

    lane = lax.broadcasted_iota(jnp.int32, (TQ, DIFF_V_DIM), 1)
    for qi in range(n_tiles):
        q = q_ref[qi * TQ:(qi + 1) * TQ, :]
        zero = jnp.zeros_like(q)
        qc_ref[qi, 0] = jnp.where(lane < DIFF_HEAD_DIM, q, zero)
        qc_ref[qi, 1] = jnp.where(lane >= DIFF_HEAD_DIM, q, zero)
    m_ref[...] = jnp.full_like(m_ref, NEG_INF)
    l_ref[...] = jnp.zeros_like(l_ref)
    acc_ref[...] = jnp.zeros_like(acc_ref)

    def unmasked_pair(n):
        qi = 1
        for t in range(2, n_tiles):
            qi = qi + jnp.where(n >= t * (t - 1), 1, 0)
        return qi, n - qi * (qi - 1)

    for c in range(2):
        for plan in chunk_plan["full"]:
            scores_chunk(1, 0, 0, c, *plan)

    n_iters = n_unmasked // ATTN_UNROLL

    def convert_weights(it):
        for src, dst in ((wg_ref, wg_bf_ref), (wu_ref, wu_bf_ref), (wd_ref, wd_bf_ref)):
            per = src.shape[0] // n_iters
            rows_it = pl.ds(pl.multiple_of(it * per, per), per)
            dst[rows_it, :] = src[rows_it, :].astype(BF16)

    def unmasked_body(it, carry):
        convert_weights(it)
        for u in range(ATTN_UNROLL):
            n = it * ATTN_UNROLL + u
            stage(unmasked_pair(n), "full", u % 2, unmasked_pair(jnp.minimum(n + 1, n_unmasked - 1)), "full")
        return carry

    lax.fori_loop(0, n_iters, unmasked_body, 0)

    def masked_stages(qi, nxt_qi):
        stage((qi, 2 * qi), "lo", 0, (qi, 2 * qi + 1), "hi")
        stage((qi, 2 * qi + 1), "hi", 1, None if nxt_qi is None else (nxt_qi, 2 * nxt_qi), "lo")

    for c in range(2):
        for plan in chunk_plan["lo"]:
            scores_chunk(0, 0, 0, c, *plan)
    for qi in range(n_tiles):
        if qi > 0:
            finalize(qi - 1)
        masked_stages(qi, qi + 1 if qi + 1 < n_tiles else None)
    finalize(n_tiles - 1)


def _attention(q, k, vt, lam_vecs, subln_w, ffn_weights, *, batch, seq, n_heads, lambda_init):
    n, width = q.shape
    nk = seq // TK
    n_tiles = seq // TQ
    n_steps = batch * n_heads
    assert TQ == 2 * TK and seq % TQ == 0 and vt.shape == (n // TK, n_heads * DIFF_V_DIM, TK)
    kern = functools.partial(_attn_kernel, lambda_init=lambda_init)
    head = lambda b, h: (b, h)
    step_rows = lambda b, h: (b * n_heads + h, 0)
    w_specs = [pl.BlockSpec((w.shape[0] // n_steps, w.shape[1]), step_rows) for w in ffn_weights]
    outs = pl.pallas_call(
        kern,
        grid=(batch, n_heads),
        in_specs=[
            pl.BlockSpec(lam_vecs.shape, lambda b, h: (0, 0)),
            pl.BlockSpec((1, DIFF_V_DIM), lambda b, h: (0, 0)),
            pl.BlockSpec((seq, DIFF_V_DIM), head),
            pl.BlockSpec((seq, DIFF_V_DIM), head),
            pl.BlockSpec((nk, DIFF_V_DIM, TK), lambda b, h: (b, h, 0)),
        ] + w_specs,
        out_specs=[pl.BlockSpec((seq, DIFF_V_DIM), head)] + w_specs,
        out_shape=[jax.ShapeDtypeStruct((n, width), BF16)]
        + [jax.ShapeDtypeStruct(w.shape, BF16) for w in ffn_weights],
        scratch_shapes=[
            pltpu.VMEM((n_tiles, 2, TQ, DIFF_V_DIM), BF16),
            pltpu.VMEM((2, TK, TQ), F32),
            pltpu.VMEM((2, TK, TQ), F32),
            pltpu.VMEM((2, 1, TQ), F32),
            pltpu.VMEM((2, 1, TQ), F32),
            pltpu.VMEM((n_tiles, 2, 1, TQ), F32),
            pltpu.VMEM((n_tiles, 2, 1, TQ), F32),
            pltpu.VMEM((n_tiles, 2, DIFF_V_DIM, TQ), F32),
        ],
        compiler_params=pltpu.CompilerParams(
            dimension_semantics=("arbitrary", "arbitrary"), vmem_limit_bytes=VMEM_LIMIT_BYTES),
        name="diff_attention",
    )(lam_vecs, subln_w, q, k, vt, *ffn_weights)
    return outs[0], outs[1:]


def _out_proj_kernel(pool_ref, attn_ref, x_ref, w_ref, g_ref, h_ref, mix_ref, nscale_ref):
    pw = pool_ref.shape[1]
    for c in range(0, mix_ref.shape[1], N_CHUNK):
        mix_ref[:, c:c + N_CHUNK] = (
            jnp.dot(pool_ref[...], w_ref[:pw, c:c + N_CHUNK], preferred_element_type=F32)
            + jnp.dot(attn_ref[...], w_ref[pw:, c:c + N_CHUNK], preferred_element_type=F32))

    def store_h(rows, y):
        h_ref[rows, :] = x_ref[rows, :] + y

    _rms_norm_rows(mix_ref, g_ref, nscale_ref, store_h)


def _out_proj(pool_out, attn_out, x2, w_out_bf, g):
    n, d = x2.shape
    tm = TM_OUT
    row = lambda i: (i, 0)
    const = lambda i: (0, 0)
    return pl.pallas_call(
        _out_proj_kernel,
        grid=(n // tm,),
        in_specs=[
            pl.BlockSpec((tm, pool_out.shape[1]), row),
            pl.BlockSpec((tm, attn_out.shape[1]), row),
            pl.BlockSpec((tm, d), row),
            pl.BlockSpec(w_out_bf.shape, const),
            pl.BlockSpec((1, d), const),
        ],
        out_specs=pl.BlockSpec((tm, d), row),
        out_shape=jax.ShapeDtypeStruct((n, d), F32),
        scratch_shapes=[
            pltpu.VMEM((tm, d), F32),
            pltpu.VMEM((tm, 1), F32),
        ],
        compiler_params=pltpu.CompilerParams(
            dimension_semantics=("arbitrary",), vmem_limit_bytes=VMEM_LIMIT_BYTES),
        name="out_proj",
    )(pool_out, attn_out, x2, w_out_bf, g)


def _ffn_kernel(h_ref, gpre_ref, gpost_ref, wg_ref, wu_ref, wd_ref, o_ref, hn_ref, nscale_ref, act_ref):
    f = pl.program_id(1)

    @pl.when(f == 0)
    def _():
        def store_hn(rows, y):
            hn_ref[rows, :] = y.astype(BF16)
            o_ref[rows, :] = jnp.zeros_like(y)

        _rms_norm_rows(h_ref, gpre_ref, nscale_ref, store_hn)

    hn = hn_ref[...]
    for cc in range(0, wg_ref.shape[1], FFN_ACT_CHUNK):
        gate = jnp.dot(hn, wg_ref[:, cc:cc + FFN_ACT_CHUNK], preferred_element_type=F32)
        up = jnp.dot(hn, wu_ref[:, cc:cc + FFN_ACT_CHUNK], preferred_element_type=F32)
        act_ref[:, cc:cc + FFN_ACT_CHUNK] = (gate * jax.nn.sigmoid(gate) * up).astype(BF16)
    for c in range(0, o_ref.shape[1], N_CHUNK):
        o_ref[:, c:c + N_CHUNK] += jnp.dot(act_ref[...], wd_ref[:, c:c + N_CHUNK], preferred_element_type=F32)

    @pl.when(f == pl.num_programs(1) - 1)
    def _():
        def store_out(rows, y):
            o_ref[rows, :] = h_ref[rows, :] + y

        _rms_norm_rows(o_ref, gpost_ref, nscale_ref, store_out)


def _ffn(h, g_pre, g_post, w_gate, w_up, w_down):
    n, d = h.shape
    d_ff = w_gate.shape[1]
    tm, tf = TM_FFN, TF_FFN
    return pl.pallas_call(
        _ffn_kernel,
        grid=(n // tm, d_ff // tf),
        in_specs=[
            pl.BlockSpec((tm, d), lambda i, f: (i, 0)),
            pl.BlockSpec((1, d), lambda i, f: (0, 0)),
            pl.BlockSpec((1, d), lambda i, f: (0, 0)),
            pl.BlockSpec((d, tf), lambda i, f: (0, f)),
            pl.BlockSpec((d, tf), lambda i, f: (0, f)),
            pl.BlockSpec((tf, d), lambda i, f: (f, 0)),
        ],
        out_specs=pl.BlockSpec((tm, d), lambda i, f: (i, 0)),
        out_shape=jax.ShapeDtypeStruct((n, d), F32),
        scratch_shapes=[pltpu.VMEM((tm, d), BF16), pltpu.VMEM((tm, 1), F32), pltpu.VMEM((tm, tf), BF16)],
        compiler_params=pltpu.CompilerParams(
            dimension_semantics=("arbitrary", "arbitrary"), vmem_limit_bytes=VMEM_LIMIT_BYTES),
        name="ffn",
    )(h, g_pre, g_post, w_gate, w_up, w_down)


def kernel(x, positions, pre_mix_norm, post_mix_norm, w_in, pool_w, pool_scale,
           lam_q1, lam_k1, lam_q2, lam_k2, subln_w, w_out,
           pre_ffn_norm, post_ffn_norm, w_gate, w_up, w_down):
    batch, seq, d_model = x.shape
    depth = w_in.shape[0]
    pool_width = pool_scale.shape[1]
    attn_width = w_out.shape[1] - pool_width
    qk_width = (w_in.shape[2] - pool_width - attn_width) // 2
    n_heads = attn_width // DIFF_V_DIM
    assert qk_width == n_heads * 2 * DIFF_HEAD_DIM
    assert seq % TQ == 0 and seq % TM_IN == 0 and (batch * seq) % TM_FFN == 0
    assert TM_IN == TK

    n = batch * seq
    h = x.reshape(n, d_model)
    pos_rows = positions.reshape(n // TM_IN, 1, TM_IN)
    inv_freq = (ROPE_THETA ** (-jnp.arange(0, ROT_DIM, 2, dtype=F32) / ROT_DIM)).reshape(ROT_HALF, 1)

    for l in range(depth):
        lambda_init = _lambda_init(l)
        pool_out, q, k, vt, w_out_bf = _in_proj(
            h, pos_rows, inv_freq, pre_mix_norm[l].reshape(1, -1), w_in[l], pool_w[l],
            pool_scale[l].reshape(1, -1), w_out[l],
            seq=seq, pool_width=pool_width, qk_width=qk_width, attn_width=attn_width)
        lam_vecs = jnp.stack([lam_q1[l], lam_k1[l], lam_q2[l], lam_k2[l]]).astype(F32)
        attn_out, (wg_bf, wu_bf, wd_bf) = _attention(
            q, k, vt, lam_vecs, subln_w[l].reshape(1, -1), (w_gate[l], w_up[l], w_down[l]),
            batch=batch, seq=seq, n_heads=n_heads, lambda_init=lambda_init)
        h = _out_proj(pool_out, attn_out, h, w_out_bf, post_mix_norm[l].reshape(1, -1))
        h = _ffn(h, pre_ffn_norm[l].reshape(1, -1), post_ffn_norm[l].reshape(1, -1), wg_bf, wu_bf, wd_bf)
    return h.reshape(batch, seq, d_model)
```

```python
import functools
import math

import jax
import jax.numpy as jnp
from jax import lax
from jax.experimental import pallas as pl
from jax.experimental.pallas import tpu as pltpu

F32 = jnp.float32
BF16 = jnp.bfloat16

POOL_WINDOWS = (2, 4, 8, 16)
DIFF_HEAD_DIM = 64
DIFF_V_DIM = 2 * DIFF_HEAD_DIM
ROPE_THETA = 500000.0
ROT_DIM = DIFF_HEAD_DIM // 4
ROT_HALF = ROT_DIM // 2
NORM_EPS = 1e-6
NEG_INF = -1e30
LOG2_E = math.log2(math.e)

LANES = 128
BF16_ROWS = 16
POOL_HALO = 32
VMEM_LIMIT_BYTES = 56 * 1024 * 1024

TM_IN = 512
TQ = 1024
TK = 512
ATTN_CHUNK = 256
ATTN_UNROLL = 4
ATTN_STATIC_TAIL = 4
TM_OUT = 512
TM_FFN = 1024
TF_FFN = 512
FFN_ACT_CHUNK = 256
N_CHUNK = 512
NORM_ROWS = 128
IN_CAST_STEPS = 8


def _lambda_init(layer_idx):
    return 0.8 - 0.6 * math.exp(-0.3 * layer_idx)


def _rms_norm(xf, g):
    ms = jnp.mean(xf * xf, axis=-1, keepdims=True)
    return xf * lax.rsqrt(ms + NORM_EPS) * g


def _rms_norm_rows(src_ref, g_ref, scale_ref, emit):
    n = src_ref.shape[0]

    def scale_body(c, carry):
        rows = pl.ds(pl.multiple_of(c * NORM_ROWS, NORM_ROWS), NORM_ROWS)
        x = src_ref[rows, :]
        scale_ref[rows, :] = lax.rsqrt(jnp.mean(x * x, axis=-1, keepdims=True) + NORM_EPS)
        return carry

    lax.fori_loop(0, n // NORM_ROWS, scale_body, 0, unroll=True)

    def apply_body(c, carry):
        rows = pl.ds(pl.multiple_of(c * NORM_ROWS, NORM_ROWS), NORM_ROWS)
        emit(rows, src_ref[rows, :] * scale_ref[rows, :] * g_ref[...])
        return carry

    lax.fori_loop(0, n // NORM_ROWS, apply_body, 0, unroll=True)


def _in_proj_kernel(x_ref, pos_ref, invf_ref, g_ref, w_ref, pw_ref, ps_ref, wo_ref,
                    pool_ref, q_ref, k_ref, vt_ref, wo_bf_ref,
                    w_bf_ref, wvt_ref, trig_ref, hn_ref, u_ref, carry_ref, *, seq, pool_width, qk_width):
    tm = x_ref.shape[0]
    step = pl.program_id(0)
    rows_per_cast = w_ref.shape[0]
    n_cast = w_bf_ref.shape[0] // rows_per_cast
    n_direct = w_bf_ref.shape[1]
    tiles_per_cast = pos_ref.shape[0]
    lane = lax.broadcasted_iota(jnp.int32, (1, LANES), 1) % DIFF_HEAD_DIM

    @pl.when(step < n_cast)
    def _():
        r0 = pl.multiple_of(step * rows_per_cast, rows_per_cast)
        w_bf_ref[pl.ds(r0, rows_per_cast), :] = w_ref[:, :n_direct].astype(BF16)
        for t in range(tiles_per_cast):
            ang = invf_ref[...] * pos_ref[t].astype(F32)
            comp = jnp.concatenate(
                [jnp.cos(ang), jnp.sin(ang), jnp.zeros((DIFF_HEAD_DIM - ROT_DIM, tm), F32)], axis=0)
            trig_ref[step * tiles_per_cast + t] = jnp.concatenate([comp, comp], axis=0).T

    for t in range(n_cast):
        @pl.when(step == t)
        def _():
            wvt_ref[:, t * rows_per_cast:(t + 1) * rows_per_cast] = w_ref[:, n_direct:].T.astype(BF16)

    @pl.when(step >= n_cast)
    def _():
        tiles_per_seq = seq // tm
        ti = (step - n_cast) % tiles_per_seq

        wo_bf_ref[...] = wo_ref[...].astype(BF16)
        hn_ref[...] = _rms_norm(x_ref[...], g_ref[...]).astype(BF16)

        @pl.when(ti == 0)
        def _():
            carry_ref[...] = jnp.zeros_like(carry_ref)

        trig = trig_ref[step - n_cast]
        coef_self = jnp.where(lane < ROT_HALF, trig,
                              jnp.where(lane < ROT_DIM, pltpu.roll(trig, ROT_HALF, 1), 1.0))
        coef_lo = jnp.where((lane >= ROT_HALF) & (lane < ROT_DIM), trig, 0.0)
        coef_hi = jnp.where(lane < ROT_HALF, -pltpu.roll(trig, LANES - ROT_HALF, 1), 0.0)

        def rope_chunk(out_ref, chunk, c, scale):
            t = jnp.dot(hn_ref[...], w_bf_ref[:, chunk * N_CHUNK:(chunk + 1) * N_CHUNK],
                        preferred_element_type=F32)
            for h in range(N_CHUNK // LANES):
                th = t[:, h * LANES:(h + 1) * LANES]
                r = (th * coef_self + pltpu.roll(th, ROT_HALF, 1) * coef_lo
                     + pltpu.roll(th, LANES - ROT_HALF, 1) * coef_hi)
                if scale != 1.0:
                    r = r * scale
                out_ref[:, c + h * LANES:c + (h + 1) * LANES] = r.astype(BF16)

        gdim = pool_width // len(POOL_WINDOWS)
        t_in_seq = ti * tm + lax.broadcasted_iota(jnp.int32, (tm, 1), 0)

        def pool_group(g, w):
            c0 = g * gdim
            u = u_ref[:, c0:c0 + gdim]
            ext = jnp.concatenate([carry_ref[g], u], axis=0)
            carry_ref[g] = u[tm - POOL_HALO:, :]
            lvl, off, k = ext, 0, 1
            while k < w:
                new_off = min(off + 8, POOL_HALO)
                cur = lvl[new_off - off:, :]
                shifted = lvl[new_off - off - k: lvl.shape[0] - k, :]
                lvl, off, k = cur + shifted, new_off, 2 * k
            wsum = lvl[POOL_HALO - off:, :]
            cnt = jnp.minimum(t_in_seq + 1, w).astype(F32)
            pooled = (wsum / cnt - u).astype(BF16)
            mixed = jnp.dot(pooled, pw_ref[g].astype(BF16), preferred_element_type=F32)
            pool_ref[:, c0:c0 + gdim] = (mixed * ps_ref[:, c0:c0 + gdim]).astype(BF16)

        n_pool, n_qk = pool_width // N_CHUNK, qk_width // N_CHUNK
        for ci in range(n_pool):
            u_ref[:, ci * N_CHUNK:(ci + 1) * N_CHUNK] = jnp.dot(
                hn_ref[...], w_bf_ref[:, ci * N_CHUNK:(ci + 1) * N_CHUNK], preferred_element_type=F32)

        q_scale = DIFF_HEAD_DIM ** -0.5 * LOG2_E
        chunks = ([(q_ref, n_pool + ci, ci * N_CHUNK, q_scale) for ci in range(n_qk)]
                  + [(k_ref, n_pool + n_qk + ci, ci * N_CHUNK, 1.0) for ci in range(n_qk)])
        groups = list(enumerate(POOL_WINDOWS))
        for idx, args in enumerate(chunks):
            rope_chunk(*args)
            for g, w in groups[idx * len(groups) // len(chunks):(idx + 1) * len(groups) // len(chunks)]:
                pool_group(g, w)

        for c in range(0, vt_ref.shape[0], N_CHUNK):
            vt_ref[c:c + N_CHUNK, :] = lax.dot_general(
                wvt_ref[c:c + N_CHUNK, :], hn_ref[...], (((1,), (1,)), ((), ())),
                preferred_element_type=F32).astype(BF16)


def _in_proj(x2, pos2, inv_freq, g, w_in, pool_w, pool_scale, w_out, *, seq, pool_width, qk_width, attn_width):
    n, d = x2.shape
    tm = TM_IN
    n_groups = len(POOL_WINDOWS)
    gdim = pool_width // n_groups
    n_direct = pool_width + 2 * qk_width
    n_cast = IN_CAST_STEPS
    n_tiles = n // tm
    assert n_tiles % n_cast == 0
    tiles_per_cast = n_tiles // n_cast
    const = lambda s: (0, 0)
    row = lambda s: (jnp.maximum(s - n_cast, 0), 0)
    kern = functools.partial(_in_proj_kernel, seq=seq, pool_width=pool_width, qk_width=qk_width)
    return pl.pallas_call(
        kern,
        grid=(n_cast + n_tiles,),
        in_specs=[
            pl.BlockSpec((tm, d), row),
            pl.BlockSpec((tiles_per_cast, 1, tm), lambda s: (jnp.minimum(s, n_cast - 1), 0, 0)),
            pl.BlockSpec((ROT_HALF, 1), const),
            pl.BlockSpec((1, d), const),
            pl.BlockSpec((d // n_cast, w_in.shape[1]), lambda s: (jnp.minimum(s, n_cast - 1), 0)),
            pl.BlockSpec(pool_w.shape, lambda s: (0, 0, 0)),
            pl.BlockSpec((1, pool_width), const),
            pl.BlockSpec((w_out.shape[0] // n_tiles, w_out.shape[1]), row),
        ],
        out_specs=[
            pl.BlockSpec((tm, pool_width), row),
            pl.BlockSpec((tm, qk_width), row),
            pl.BlockSpec((tm, qk_width), row),
            pl.BlockSpec((None, attn_width, tm), lambda s: (jnp.maximum(s - n_cast, 0), 0, 0)),
            pl.BlockSpec((w_out.shape[0] // n_tiles, w_out.shape[1]), row),
        ],
        out_shape=[
            jax.ShapeDtypeStruct((n, pool_width), BF16),
            jax.ShapeDtypeStruct((n, qk_width), BF16),
            jax.ShapeDtypeStruct((n, qk_width), BF16),
            jax.ShapeDtypeStruct((n // tm, attn_width, tm), BF16),
            jax.ShapeDtypeStruct(w_out.shape, BF16),
        ],
        scratch_shapes=[
            pltpu.VMEM((d, n_direct), BF16),
            pltpu.VMEM((attn_width, d), BF16),
            pltpu.VMEM((n_tiles, tm, LANES), F32),
            pltpu.VMEM((tm, d), BF16),
            pltpu.VMEM((tm, pool_width), F32),
            pltpu.VMEM((n_groups, POOL_HALO, gdim), F32),
        ],
        compiler_params=pltpu.CompilerParams(
            dimension_semantics=("arbitrary",), vmem_limit_bytes=VMEM_LIMIT_BYTES),
        name="in_proj",
    )(x2, pos2, inv_freq, g, w_in, pool_w, pool_scale, w_out)


def _attn_kernel(lam_ref, sw_ref, q_ref, k_ref, vt_ref, wg_ref, wu_ref, wd_ref,
                 o_ref, wg_bf_ref, wu_bf_ref, wd_bf_ref,
                 qc_ref, s0_ref, s1_ref, mx0_ref, mx1_ref, m_ref, l_ref, acc_ref, *, lambda_init):
    seq = q_ref.shape[0]
    n_tiles = seq // TQ
    n_unmasked = n_tiles * (n_tiles - 1)
    assert n_tiles >= 2 and ATTN_UNROLL % 2 == 0 and ATTN_STATIC_TAIL % 2 == 0
    assert 0 < n_unmasked - ATTN_STATIC_TAIL and (n_unmasked - ATTN_STATIC_TAIL) % ATTN_UNROLL == 0
    assert all(w.shape[0] % (BF16_ROWS * ((n_unmasked - ATTN_STATIC_TAIL) // ATTN_UNROLL)) == 0
               for w in (wg_ref, wu_ref, wd_ref))
    s_slots, mx_slots = (s0_ref, s1_ref), (mx0_ref, mx1_ref)
    chunks = [slice(a, a + ATTN_CHUNK) for a in range(0, TQ, ATTN_CHUNK)]
    chunk_plan = {
        "full": [(ch, None, TK) for ch in chunks],
        "lo": [(ch, ch.start, min(TK, ch.stop)) if ch.start < TK else (ch, None, TK) for ch in chunks],
        "hi": [(ch, ch.start - TK, min(TK, ch.stop - TK)) for ch in chunks if ch.start >= TK],
    }

    def rows(i, size):
        return pl.ds(i * size if isinstance(i, int) else pl.multiple_of(i * size, size), size)

    def key_rows(j, n_keys):
        start = j * TK if isinstance(j, int) else pl.multiple_of(j * TK, TK)
        return pl.ds(start, n_keys)

    def scores_chunk(qi, j, slot, c, ch, offset, n_keys):
        s = lax.dot_general(k_ref[key_rows(j, n_keys), :], qc_ref[qi, c, ch, :], (((1,), (1,)), ((), ())),
                            preferred_element_type=F32)
        if offset is not None:
            kv = lax.broadcasted_iota(jnp.int32, s.shape, 0)
            r = lax.broadcasted_iota(jnp.int32, s.shape, 1)
            s = jnp.where(kv <= r + offset, s, NEG_INF)
        s_slots[slot][c, 0:n_keys, ch] = s
        mx_slots[slot][c, :, ch] = jnp.max(s, axis=0, keepdims=True)

    def exp_pv_chunk(qi, j, slot, c, ch, n_keys):
        m_prev = m_ref[qi, c, :, ch]
        m_new = jnp.maximum(m_prev, mx_slots[slot][c, :, ch])
        alpha = jnp.exp2(m_prev - m_new)
        p = jnp.exp2(s_slots[slot][c, 0:n_keys, ch] - m_new)
        l_ref[qi, c, :, ch] = alpha * l_ref[qi, c, :, ch] + jnp.sum(p, axis=0, keepdims=True)
        pv = jnp.dot(vt_ref[j, :, 0:n_keys], p.astype(BF16), preferred_element_type=F32)
        acc_ref[qi, c, :, ch] = alpha * acc_ref[qi, c, :, ch] + pv
        m_ref[qi, c, :, ch] = m_new

    def stage(cur, cur_kind, slot, nxt, nxt_kind):
        todo_next = chunk_plan[nxt_kind] if nxt is not None else []
        todo_cur = [(ch, n_keys) for ch, _, n_keys in chunk_plan[cur_kind]]
        for c in range(2):
            for i in range(max(len(todo_next), len(todo_cur))):
                if i < len(todo_next):
                    scores_chunk(nxt[0], nxt[1], 1 - slot, c, *todo_next[i])
                if i < len(todo_cur):
                    exp_pv_chunk(cur[0], cur[1], slot, c, *todo_cur[i])

    def finalize(qi):
        lam_v = lam_ref[...]
        lam = (jnp.exp(jnp.sum(lam_v[0:1] * lam_v[1:2], keepdims=True))
               - jnp.exp(jnp.sum(lam_v[2:3] * lam_v[3:4], keepdims=True)) + lambda_init)
        attn_t = acc_ref[qi, 0] / l_ref[qi, 0] - lam * (acc_ref[qi, 1] / l_ref[qi, 1])
        ms = jnp.mean(attn_t * attn_t, axis=0, keepdims=True)
        y = (attn_t * lax.rsqrt(ms + NORM_EPS)).T
        o_ref[rows(qi, TQ), :] = (y * sw_ref[...] * (1.0 - lambda_init)).astype(BF16)

    lane = lax.broadcasted_iota(jnp.int32, (TQ, DIFF_V_DIM), 1)
    for qi in range(n_tiles):
        q = q_ref[qi * TQ:(qi + 1) * TQ, :]
        zero = jnp.zeros_like(q)
        qc_ref[qi, 0] = jnp.where(lane < DIFF_HEAD_DIM, q, zero)
        qc_ref[qi, 1] = jnp.where(lane >= DIFF_HEAD_DIM, q, zero)
    m_ref[...] = jnp.full_like(m_ref, NEG_INF)
    l_ref[...] = jnp.zeros_like(l_ref)
    acc_ref[...] = jnp.zeros_like(acc_ref)

    def unmasked_pair(n):
        if isinstance(n, int):
            qi = max(t for t in range(1, n_tiles) if t * (t - 1) <= n)
            return qi, n - qi * (qi - 1)
        qi = 1
        for t in range(2, n_tiles):
            qi = qi + jnp.where(n >= t * (t - 1), 1, 0)
        return qi, n - qi * (qi - 1)

    for c in range(2):
        for plan in chunk_plan["full"]:
            scores_chunk(1, 0, 0, c, *plan)

    n_looped = n_unmasked - ATTN_STATIC_TAIL
    n_iters = n_looped // ATTN_UNROLL

    def convert_weights(it):
        for src, dst in ((wg_ref, wg_bf_ref), (wu_ref, wu_bf_ref), (wd_ref, wd_bf_ref)):
            per = src.shape[0] // n_iters
            rows_it = pl.ds(pl.multiple_of(it * per, per), per)
            dst[rows_it, :] = src[rows_it, :].astype(BF16)

    def unmasked_body(it, carry):
        convert_weights(it)
        for u in range(ATTN_UNROLL):
            n = it * ATTN_UNROLL + u
            stage(unmasked_pair(n), "full", u % 2, unmasked_pair(n + 1), "full")
        return carry

    lax.fori_loop(0, n_iters, unmasked_body, 0)
    for n in range(n_looped, n_unmasked):
        last = n == n_unmasked - 1
        stage(unmasked_pair(n), "full", n % 2, (0, 0) if last else unmasked_pair(n + 1), "lo" if last else "full")

    def masked_stages(qi, nxt_qi):
        stage((qi, 2 * qi), "lo", 0, (qi, 2 * qi + 1), "hi")
        stage((qi, 2 * qi + 1), "hi", 1, None if nxt_qi is None else (nxt_qi, 2 * nxt_qi), "lo")

    for qi in range(n_tiles):
        if qi > 0:
            finalize(qi - 1)
        masked_stages(qi, qi + 1 if qi + 1 < n_tiles else None)
    finalize(n_tiles - 1)


def _attention(q, k, vt, lam_vecs, subln_w, ffn_weights, *, batch, seq, n_heads, lambda_init):
    n, width = q.shape
    nk = seq // TK
    n_tiles = seq // TQ
    n_steps = batch * n_heads
    assert TQ == 2 * TK and seq % TQ == 0 and vt.shape == (n // TK, n_heads * DIFF_V_DIM, TK)
    kern = functools.partial(_attn_kernel, lambda_init=lambda_init)
    head = lambda b, h: (b, h)
    step_rows = lambda b, h: (b * n_heads + h, 0)
    w_specs = [pl.BlockSpec((w.shape[0] // n_steps, w.shape[1]), step_rows) for w in ffn_weights]
    outs = pl.pallas_call(
        kern,
        grid=(batch, n_heads),
        in_specs=[
            pl.BlockSpec(lam_vecs.shape, lambda b, h: (0, 0)),
            pl.BlockSpec((1, DIFF_V_DIM), lambda b, h: (0, 0)),
            pl.BlockSpec((seq, DIFF_V_DIM), head),
            pl.BlockSpec((seq, DIFF_V_DIM), head),
            pl.BlockSpec((nk, DIFF_V_DIM, TK), lambda b, h: (b, h, 0)),
        ] + w_specs,
        out_specs=[pl.BlockSpec((seq, DIFF_V_DIM), head)] + w_specs,
        out_shape=[jax.ShapeDtypeStruct((n, width), BF16)]
        + [jax.ShapeDtypeStruct(w.shape, BF16) for w in ffn_weights],
        scratch_shapes=[
            pltpu.VMEM((n_tiles, 2, TQ, DIFF_V_DIM), BF16),
            pltpu.VMEM((2, TK, TQ), F32),
            pltpu.VMEM((2, TK, TQ), F32),
            pltpu.VMEM((2, 1, TQ), F32),
            pltpu.VMEM((2, 1, TQ), F32),
            pltpu.VMEM((n_tiles, 2, 1, TQ), F32),
            pltpu.VMEM((n_tiles, 2, 1, TQ), F32),
            pltpu.VMEM((n_tiles, 2, DIFF_V_DIM, TQ), F32),
        ],
        compiler_params=pltpu.CompilerParams(
            dimension_semantics=("arbitrary", "arbitrary"), vmem_limit_bytes=VMEM_LIMIT_BYTES),
        name="diff_attention",
    )(lam_vecs, subln_w, q, k, vt, *ffn_weights)
    return outs[0], outs[1:]


def _out_proj_kernel(pool_ref, attn_ref, x_ref, w_ref, g_ref, h_ref, mix_ref, nscale_ref):
    pw = pool_ref.shape[1]
    for c in range(0, mix_ref.shape[1], N_CHUNK):
        mix_ref[:, c:c + N_CHUNK] = (
            jnp.dot(pool_ref[...], w_ref[:pw, c:c + N_CHUNK], preferred_element_type=F32)
            + jnp.dot(attn_ref[...], w_ref[pw:, c:c + N_CHUNK], preferred_element_type=F32))

    def store_h(rows, y):
        h_ref[rows, :] = x_ref[rows, :] + y

    _rms_norm_rows(mix_ref, g_ref, nscale_ref, store_h)


def _out_proj(pool_out, attn_out, x2, w_out_bf, g):
    n, d = x2.shape
    tm = TM_OUT
    row = lambda i: (i, 0)
    const = lambda i: (0, 0)
    return pl.pallas_call(
        _out_proj_kernel,
        grid=(n // tm,),
        in_specs=[
            pl.BlockSpec((tm, pool_out.shape[1]), row),
            pl.BlockSpec((tm, attn_out.shape[1]), row),
            pl.BlockSpec((tm, d), row),
            pl.BlockSpec(w_out_bf.shape, const),
            pl.BlockSpec((1, d), const),
        ],
        out_specs=pl.BlockSpec((tm, d), row),
        out_shape=jax.ShapeDtypeStruct((n, d), F32),
        scratch_shapes=[
            pltpu.VMEM((tm, d), F32),
            pltpu.VMEM((tm, 1), F32),
        ],
        compiler_params=pltpu.CompilerParams(
            dimension_semantics=("arbitrary",), vmem_limit_bytes=VMEM_LIMIT_BYTES),
        name="out_proj",
    )(pool_out, attn_out, x2, w_out_bf, g)


def _ffn_kernel(h_ref, gpre_ref, gpost_ref, wg_ref, wu_ref, wd_ref, o_ref, hn_ref, nscale_ref, act_ref):
    f = pl.program_id(1)

    @pl.when(f == 0)
    def _():
        def store_hn(rows, y):
            hn_ref[rows, :] = y.astype(BF16)
            o_ref[rows, :] = jnp.zeros_like(y)

        _rms_norm_rows(h_ref, gpre_ref, nscale_ref, store_hn)

    hn = hn_ref[...]
    for cc in range(0, wg_ref.shape[1], FFN_ACT_CHUNK):
        gate = jnp.dot(hn, wg_ref[:, cc:cc + FFN_ACT_CHUNK], preferred_element_type=F32)
        up = jnp.dot(hn, wu_ref[:, cc:cc + FFN_ACT_CHUNK], preferred_element_type=F32)
        act_ref[:, cc:cc + FFN_ACT_CHUNK] = (gate * jax.nn.sigmoid(gate) * up).astype(BF16)
    for c in range(0, o_ref.shape[1], N_CHUNK):
        o_ref[:, c:c + N_CHUNK] += jnp.dot(act_ref[...], wd_ref[:, c:c + N_CHUNK], preferred_element_type=F32)

    @pl.when(f == pl.num_programs(1) - 1)
    def _():
        def store_out(rows, y):
            o_ref[rows, :] = h_ref[rows, :] + y

        _rms_norm_rows(o_ref, gpost_ref, nscale_ref, store_out)


def _ffn(h, g_pre, g_post, w_gate, w_up, w_down):
    n, d = h.shape
    d_ff = w_gate.shape[1]
    tm, tf = TM_FFN, TF_FFN
    return pl.pallas_call(
        _ffn_kernel,
        grid=(n // tm, d_ff // tf),
        in_specs=[
            pl.BlockSpec((tm, d), lambda i, f: (i, 0)),
            pl.BlockSpec((1, d), lambda i, f: (0, 0)),
            pl.BlockSpec((1, d), lambda i, f: (0, 0)),
            pl.BlockSpec((d, tf), lambda i, f: (0, f)),
            pl.BlockSpec((d, tf), lambda i, f: (0, f)),
            pl.BlockSpec((tf, d), lambda i, f: (f, 0)),
        ],
        out_specs=pl.BlockSpec((tm, d), lambda i, f: (i, 0)),
        out_shape=jax.ShapeDtypeStruct((n, d), F32),
        scratch_shapes=[pltpu.VMEM((tm, d), BF16), pltpu.VMEM((tm, 1), F32), pltpu.VMEM((tm, tf), BF16)],
        compiler_params=pltpu.CompilerParams(
            dimension_semantics=("arbitrary", "arbitrary"), vmem_limit_bytes=VMEM_LIMIT_BYTES),
        name="ffn",
    )(h, g_pre, g_post, w_gate, w_up, w_down)


def kernel(x, positions, pre_mix_norm, post_mix_norm, w_in, pool_w, pool_scale,
           lam_q1, lam_k1, lam_q2, lam_k2, subln_w, w_out,
           pre_ffn_norm, post_ffn_norm, w_gate, w_up, w_down):
    batch, seq, d_model = x.shape
    depth = w_in.shape[0]
    pool_width = pool_scale.shape[1]
    attn_width = w_out.shape[1] - pool_width
    qk_width = (w_in.shape[2] - pool_width - attn_width) // 2
    n_heads = attn_width // DIFF_V_DIM
    assert qk_width == n_heads * 2 * DIFF_HEAD_DIM
    assert seq % TQ == 0 and seq % TM_IN == 0 and (batch * seq) % TM_FFN == 0
    assert TM_IN == TK

    n = batch * seq
    h = x.reshape(n, d_model)
    pos_rows = positions.reshape(n // TM_IN, 1, TM_IN)
    inv_freq = (ROPE_THETA ** (-jnp.arange(0, ROT_DIM, 2, dtype=F32) / ROT_DIM)).reshape(ROT_HALF, 1)

    for l in range(depth):
        lambda_init = _lambda_init(l)
        pool_out, q, k, vt, w_out_bf = _in_proj(
            h, pos_rows, inv_freq, pre_mix_norm[l].reshape(1, -1), w_in[l], pool_w[l],
            pool_scale[l].reshape(1, -1), w_out[l],
            seq=seq, pool_width=pool_width, qk_width=qk_width, attn_width=attn_width)
        lam_vecs = jnp.stack([lam_q1[l], lam_k1[l], lam_q2[l], lam_k2[l]]).astype(F32)
        attn_out, (wg_bf, wu_bf, wd_bf) = _attention(
            q, k, vt, lam_vecs, subln_w[l].reshape(1, -1), (w_gate[l], w_up[l], w_down[l]),
            batch=batch, seq=seq, n_heads=n_heads, lambda_init=lambda_init)
        h = _out_proj(pool_out, attn_out, h, w_out_bf, post_mix_norm[l].reshape(1, -1))
        h = _ffn(h, pre_ffn_norm[l].reshape(1, -1), post_ffn_norm[l].reshape(1, -1), wg_bf, wu_bf, wd_bf)
    return h.reshape(batch, seq, d_model)
```

```python
import functools
import math

import jax
import jax.numpy as jnp
from jax import lax
from jax.experimental import pallas as pl
from jax.experimental.pallas import tpu as pltpu

F32 = jnp.float32
BF16 = jnp.bfloat16

POOL_WINDOWS = (2, 4, 8, 16)
DIFF_HEAD_DIM = 64
DIFF_V_DIM = 2 * DIFF_HEAD_DIM
ROPE_THETA = 500000.0
ROT_DIM = DIFF_HEAD_DIM // 4
ROT_HALF = ROT_DIM // 2
NORM_EPS = 1e-6
NEG_INF = -1e30
LOG2_E = math.log2(math.e)

LANES = 128
BF16_ROWS = 16
POOL_HALO = 32
VMEM_LIMIT_BYTES = 56 * 1024 * 1024

TM_IN = 512
TQ = 1024
TK = 512
ATTN_CHUNK = 256
ATTN_UNROLL = 6
TM_OUT = 512
TM_FFN = 1024
TF_FFN = 512
FFN_ACT_CHUNK = 256
N_CHUNK = 512
NORM_ROWS = 128
IN_CAST_STEPS = 8


def _lambda_init(layer_idx):
    return 0.8 - 0.6 * math.exp(-0.3 * layer_idx)


def _rms_norm(xf, g):
    ms = jnp.mean(xf * xf, axis=-1, keepdims=True)
    return xf * lax.rsqrt(ms + NORM_EPS) * g


def _rms_norm_rows(src_ref, g_ref, scale_ref, emit):
    n = src_ref.shape[0]

    def scale_body(c, carry):
        rows = pl.ds(pl.multiple_of(c * NORM_ROWS, NORM_ROWS), NORM_ROWS)
        x = src_ref[rows, :]
        scale_ref[rows, :] = lax.rsqrt(jnp.mean(x * x, axis=-1, keepdims=True) + NORM_EPS)
        return carry

    lax.fori_loop(0, n // NORM_ROWS, scale_body, 0, unroll=True)

    def apply_body(c, carry):
        rows = pl.ds(pl.multiple_of(c * NORM_ROWS, NORM_ROWS), NORM_ROWS)
        emit(rows, src_ref[rows, :] * scale_ref[rows, :] * g_ref[...])
        return carry

    lax.fori_loop(0, n // NORM_ROWS, apply_body, 0, unroll=True)


def _in_proj_kernel(x_ref, pos_ref, invf_ref, g_ref, w_ref, pw_ref, ps_ref, wo_ref,
                    pool_ref, q_ref, k_ref, vt_ref, wo_bf_ref,
                    w_bf_ref, wvt_ref, trig_ref, hn_ref, u_ref, carry_ref, *, seq, pool_width, qk_width):
    tm = x_ref.shape[0]
    step = pl.program_id(0)
    rows_per_cast = w_ref.shape[0]
    n_cast = w_bf_ref.shape[0] // rows_per_cast
    n_direct = w_bf_ref.shape[1]
    tiles_per_cast = pos_ref.shape[0]
    lane = lax.broadcasted_iota(jnp.int32, (1, LANES), 1) % DIFF_HEAD_DIM

    @pl.when(step < n_cast)
    def _():
        r0 = pl.multiple_of(step * rows_per_cast, rows_per_cast)
        w_bf_ref[pl.ds(r0, rows_per_cast), :] = w_ref[:, :n_direct].astype(BF16)
        for t in range(tiles_per_cast):
            ang = invf_ref[...] * pos_ref[t].astype(F32)
            comp = jnp.concatenate(
                [jnp.cos(ang), jnp.sin(ang), jnp.zeros((DIFF_HEAD_DIM - ROT_DIM, tm), F32)], axis=0)
            trig_ref[step * tiles_per_cast + t] = jnp.concatenate([comp, comp], axis=0).T

    for t in range(n_cast):
        @pl.when(step == t)
        def _():
            wvt_ref[:, t * rows_per_cast:(t + 1) * rows_per_cast] = w_ref[:, n_direct:].T.astype(BF16)

    @pl.when(step >= n_cast)
    def _():
        tiles_per_seq = seq // tm
        ti = (step - n_cast) % tiles_per_seq

        wo_bf_ref[...] = wo_ref[...].astype(BF16)
        hn_ref[...] = _rms_norm(x_ref[...], g_ref[...]).astype(BF16)

        @pl.when(ti == 0)
        def _():
            carry_ref[...] = jnp.zeros_like(carry_ref)

        trig = trig_ref[step - n_cast]
        coef_self = jnp.where(lane < ROT_HALF, trig,
                              jnp.where(lane < ROT_DIM, pltpu.roll(trig, ROT_HALF, 1), 1.0))
        coef_lo = jnp.where((lane >= ROT_HALF) & (lane < ROT_DIM), trig, 0.0)
        coef_hi = jnp.where(lane < ROT_HALF, -pltpu.roll(trig, LANES - ROT_HALF, 1), 0.0)

        def rope_chunk(out_ref, chunk, c, scale):
            t = jnp.dot(hn_ref[...], w_bf_ref[:, chunk * N_CHUNK:(chunk + 1) * N_CHUNK],
                        preferred_element_type=F32)
            for h in range(N_CHUNK // LANES):
                th = t[:, h * LANES:(h + 1) * LANES]
                r = (th * coef_self + pltpu.roll(th, ROT_HALF, 1) * coef_lo
                     + pltpu.roll(th, LANES - ROT_HALF, 1) * coef_hi)
                if scale != 1.0:
                    r = r * scale
                out_ref[(c + h * LANES) // DIFF_V_DIM] = r.astype(BF16)

        gdim = pool_width // len(POOL_WINDOWS)
        t_in_seq = ti * tm + lax.broadcasted_iota(jnp.int32, (tm, 1), 0)

        def pool_group(g, w):
            c0 = g * gdim
            u = u_ref[:, c0:c0 + gdim]
            ext = jnp.concatenate([carry_ref[g], u], axis=0)
            carry_ref[g] = u[tm - POOL_HALO:, :]
            lvl, off, k = ext, 0, 1
            while k < w:
                new_off = min(off + 8, POOL_HALO)
                cur = lvl[new_off - off:, :]
                shifted = lvl[new_off - off - k: lvl.shape[0] - k, :]
                lvl, off, k = cur + shifted, new_off, 2 * k
            wsum = lvl[POOL_HALO - off:, :]
            cnt = jnp.minimum(t_in_seq + 1, w).astype(F32)
            pooled = (wsum / cnt - u).astype(BF16)
            mixed = jnp.dot(pooled, pw_ref[g].astype(BF16), preferred_element_type=F32)
            pool_ref[:, c0:c0 + gdim] = (mixed * ps_ref[:, c0:c0 + gdim]).astype(BF16)

        n_pool, n_qk = pool_width // N_CHUNK, qk_width // N_CHUNK
        for ci in range(n_pool):
            u_ref[:, ci * N_CHUNK:(ci + 1) * N_CHUNK] = jnp.dot(
                hn_ref[...], w_bf_ref[:, ci * N_CHUNK:(ci + 1) * N_CHUNK], preferred_element_type=F32)

        q_scale = DIFF_HEAD_DIM ** -0.5 * LOG2_E
        chunks = ([(q_ref, n_pool + ci, ci * N_CHUNK, q_scale) for ci in range(n_qk)]
                  + [(k_ref, n_pool + n_qk + ci, ci * N_CHUNK, 1.0) for ci in range(n_qk)])
        groups = list(enumerate(POOL_WINDOWS))
        for idx, args in enumerate(chunks):
            rope_chunk(*args)
            for g, w in groups[idx * len(groups) // len(chunks):(idx + 1) * len(groups) // len(chunks)]:
                pool_group(g, w)

        for c in range(0, vt_ref.shape[0], N_CHUNK):
            vt_ref[c:c + N_CHUNK, :] = lax.dot_general(
                wvt_ref[c:c + N_CHUNK, :], hn_ref[...], (((1,), (1,)), ((), ())),
                preferred_element_type=F32).astype(BF16)


def _in_proj(x2, pos2, inv_freq, g, w_in, pool_w, pool_scale, w_out, *, seq, pool_width, qk_width, attn_width):
    n, d = x2.shape
    tm = TM_IN
    n_groups = len(POOL_WINDOWS)
    gdim = pool_width // n_groups
    n_direct = pool_width + 2 * qk_width
    n_cast = IN_CAST_STEPS
    n_tiles = n // tm
    assert n_tiles % n_cast == 0
    tiles_per_cast = n_tiles // n_cast
    const = lambda s: (0, 0)
    row = lambda s: (jnp.maximum(s - n_cast, 0), 0)
    n_heads = qk_width // DIFF_V_DIM
    head_major = lambda s: (0, jnp.maximum(s - n_cast, 0), 0)
    kern = functools.partial(_in_proj_kernel, seq=seq, pool_width=pool_width, qk_width=qk_width)
    return pl.pallas_call(
        kern,
        grid=(n_cast + n_tiles,),
        in_specs=[
            pl.BlockSpec((tm, d), row),
            pl.BlockSpec((tiles_per_cast, 1, tm), lambda s: (jnp.minimum(s, n_cast - 1), 0, 0)),
            pl.BlockSpec((ROT_HALF, 1), const),
            pl.BlockSpec((1, d), const),
            pl.BlockSpec((d // n_cast, w_in.shape[1]), lambda s: (jnp.minimum(s, n_cast - 1), 0)),
            pl.BlockSpec(pool_w.shape, lambda s: (0, 0, 0)),
            pl.BlockSpec((1, pool_width), const),
            pl.BlockSpec((w_out.shape[0] // n_tiles, w_out.shape[1]), row),
        ],
        out_specs=[
            pl.BlockSpec((tm, pool_width), row),
            pl.BlockSpec((n_heads, tm, DIFF_V_DIM), head_major),
            pl.BlockSpec((n_heads, tm, DIFF_V_DIM), head_major),
            pl.BlockSpec((None, attn_width, tm), lambda s: (jnp.maximum(s - n_cast, 0), 0, 0)),
            pl.BlockSpec((w_out.shape[0] // n_tiles, w_out.shape[1]), row),
        ],
        out_shape=[
            jax.ShapeDtypeStruct((n, pool_width), BF16),
            jax.ShapeDtypeStruct((n_heads, n, DIFF_V_DIM), BF16),
            jax.ShapeDtypeStruct((n_heads, n, DIFF_V_DIM), BF16),
            jax.ShapeDtypeStruct((n // tm, attn_width, tm), BF16),
            jax.ShapeDtypeStruct(w_out.shape, BF16),
        ],
        scratch_shapes=[
            pltpu.VMEM((d, n_direct), BF16),
            pltpu.VMEM((attn_width, d), BF16),
            pltpu.VMEM((n_tiles, tm, LANES), F32),
            pltpu.VMEM((tm, d), BF16),
            pltpu.VMEM((tm, pool_width), F32),
            pltpu.VMEM((n_groups, POOL_HALO, gdim), F32),
        ],
        compiler_params=pltpu.CompilerParams(
            dimension_semantics=("arbitrary",), vmem_limit_bytes=VMEM_LIMIT_BYTES),
        name="in_proj",
    )(x2, pos2, inv_freq, g, w_in, pool_w, pool_scale, w_out)


def _attn_kernel(lam_ref, sw_ref, q_ref, k_ref, vt_ref, wg_ref, wu_ref, wd_ref,
                 o_ref, wg_bf_ref, wu_bf_ref, wd_bf_ref,
                 qc_ref, s0_ref, s1_ref, mx0_ref, mx1_ref, m_ref, l_ref, acc_ref, *, lambda_init):
    seq = q_ref.shape[0]
    n_tiles = seq // TQ
    n_unmasked = n_tiles * (n_tiles - 1)
    assert n_tiles >= 2 and n_unmasked % ATTN_UNROLL == 0 and ATTN_UNROLL % 2 == 0
    assert all(w.shape[0] % (BF16_ROWS * (n_unmasked // ATTN_UNROLL)) == 0 for w in (wg_ref, wu_ref, wd_ref))
    s_slots, mx_slots = (s0_ref, s1_ref), (mx0_ref, mx1_ref)
    chunks = [slice(a, a + ATTN_CHUNK) for a in range(0, TQ, ATTN_CHUNK)]
    chunk_plan = {
        "full": [(ch, None, TK) for ch in chunks],
        "lo": [(ch, ch.start, min(TK, ch.stop)) if ch.start < TK else (ch, None, TK) for ch in chunks],
        "hi": [(ch, ch.start - TK, min(TK, ch.stop - TK)) for ch in chunks if ch.start >= TK],
    }

    def rows(i, size):
        return pl.ds(i * size if isinstance(i, int) else pl.multiple_of(i * size, size), size)

    def key_rows(j, n_keys):
        start = j * TK if isinstance(j, int) else pl.multiple_of(j * TK, TK)
        return pl.ds(start, n_keys)

    def scores_chunk(qi, j, slot, c, ch, offset, n_keys):
        s = lax.dot_general(k_ref[key_rows(j, n_keys), :], qc_ref[qi, c, ch, :], (((1,), (1,)), ((), ())),
                            preferred_element_type=F32)
        if offset is not None:
            kv = lax.broadcasted_iota(jnp.int32, s.shape, 0)
            r = lax.broadcasted_iota(jnp.int32, s.shape, 1)
            s = jnp.where(kv <= r + offset, s, NEG_INF)
        s_slots[slot][c, 0:n_keys, ch] = s
        mx_slots[slot][c, :, ch] = jnp.max(s, axis=0, keepdims=True)

    def exp_pv_chunk(qi, j, slot, c, ch, n_keys):
        m_prev = m_ref[qi, c, :, ch]
        m_new = jnp.maximum(m_prev, mx_slots[slot][c, :, ch])
        alpha = jnp.exp2(m_prev - m_new)
        p = jnp.exp2(s_slots[slot][c, 0:n_keys, ch] - m_new)
        l_ref[qi, c, :, ch] = alpha * l_ref[qi, c, :, ch] + jnp.sum(p, axis=0, keepdims=True)
        pv = jnp.dot(vt_ref[j, :, 0:n_keys], p.astype(BF16), preferred_element_type=F32)
        acc_ref[qi, c, :, ch] = alpha * acc_ref[qi, c, :, ch] + pv
        m_ref[qi, c, :, ch] = m_new

    def stage(cur, cur_kind, slot, nxt, nxt_kind):
        todo_next = chunk_plan[nxt_kind] if nxt is not None else []
        todo_cur = [(ch, n_keys) for ch, _, n_keys in chunk_plan[cur_kind]]
        for c in range(2):
            for i in range(max(len(todo_next), len(todo_cur))):
                if i < len(todo_next):
                    scores_chunk(nxt[0], nxt[1], 1 - slot, c, *todo_next[i])
                if i < len(todo_cur):
                    exp_pv_chunk(cur[0], cur[1], slot, c, *todo_cur[i])

    def finalize(qi):
        lam_v = lam_ref[...]
        lam = (jnp.exp(jnp.sum(lam_v[0:1] * lam_v[1:2], keepdims=True))
               - jnp.exp(jnp.sum(lam_v[2:3] * lam_v[3:4], keepdims=True)) + lambda_init)
        attn_t = acc_ref[qi, 0] / l_ref[qi, 0] - lam * (acc_ref[qi, 1] / l_ref[qi, 1])
        ms = jnp.mean(attn_t * attn_t, axis=0, keepdims=True)
        y = (attn_t * lax.rsqrt(ms + NORM_EPS)).T
        o_ref[rows(qi, TQ), :] = (y * sw_ref[...] * (1.0 - lambda_init)).astype(BF16)

    lane = lax.broadcasted_iota(jnp.int32, (TQ, DIFF_V_DIM), 1)
    for qi in range(n_tiles):
        q = q_ref[qi * TQ:(qi + 1) * TQ, :]
        zero = jnp.zeros_like(q)
        qc_ref[qi, 0] = jnp.where(lane < DIFF_HEAD_DIM, q, zero)
        qc_ref[qi, 1] = jnp.where(lane >= DIFF_HEAD_DIM, q, zero)
    m_ref[...] = jnp.full_like(m_ref, NEG_INF)
    l_ref[...] = jnp.zeros_like(l_ref)
    acc_ref[...] = jnp.zeros_like(acc_ref)

    def unmasked_pair(n):
        qi = 1
        for t in range(2, n_tiles):
            qi = qi + jnp.where(n >= t * (t - 1), 1, 0)
        return qi, n - qi * (qi - 1)

    for c in range(2):
        for plan in chunk_plan["full"]:
            scores_chunk(1, 0, 0, c, *plan)

    n_iters = n_unmasked // ATTN_UNROLL

    def convert_weights(it):
        for src, dst in ((wg_ref, wg_bf_ref), (wu_ref, wu_bf_ref), (wd_ref, wd_bf_ref)):
            per = src.shape[0] // n_iters
            rows_it = pl.ds(pl.multiple_of(it * per, per), per)
            dst[rows_it, :] = src[rows_it, :].astype(BF16)

    def unmasked_body(it, carry):
        convert_weights(it)
        for u in range(ATTN_UNROLL):
            n = it * ATTN_UNROLL + u
            stage(unmasked_pair(n), "full", u % 2, unmasked_pair(jnp.minimum(n + 1, n_unmasked - 1)), "full")
        return carry

    lax.fori_loop(0, n_iters, unmasked_body, 0)

    def masked_stages(qi, nxt_qi):
        stage((qi, 2 * qi), "lo", 0, (qi, 2 * qi + 1), "hi")
        stage((qi, 2 * qi + 1), "hi", 1, None if nxt_qi is None else (nxt_qi, 2 * nxt_qi), "lo")

    for c in range(2):
        for plan in chunk_plan["lo"]:
            scores_chunk(0, 0, 0, c, *plan)
    for qi in range(n_tiles):
        if qi > 0:
            finalize(qi - 1)
        masked_stages(qi, qi + 1 if qi + 1 < n_tiles else None)
    finalize(n_tiles - 1)


def _attention(q, k, vt, lam_vecs, subln_w, ffn_weights, *, batch, seq, n_heads, lambda_init):
    _, n, _ = q.shape
    nk = seq // TK
    n_tiles = seq // TQ
    n_steps = batch * n_heads
    assert TQ == 2 * TK and seq % TQ == 0 and vt.shape == (n // TK, n_heads * DIFF_V_DIM, TK)
    kern = functools.partial(_attn_kernel, lambda_init=lambda_init)
    head = lambda b, h: (h, b, 0)
    step_rows = lambda b, h: (b * n_heads + h, 0)
    w_specs = [pl.BlockSpec((w.shape[0] // n_steps, w.shape[1]), step_rows) for w in ffn_weights]
    outs = pl.pallas_call(
        kern,
        grid=(batch, n_heads),
        in_specs=[
            pl.BlockSpec(lam_vecs.shape, lambda b, h: (0, 0)),
            pl.BlockSpec((1, DIFF_V_DIM), lambda b, h: (0, 0)),
            pl.BlockSpec((None, seq, DIFF_V_DIM), head),
            pl.BlockSpec((None, seq, DIFF_V_DIM), head),
            pl.BlockSpec((nk, DIFF_V_DIM, TK), lambda b, h: (b, h, 0)),
        ] + w_specs,
        out_specs=[pl.BlockSpec((None, seq, DIFF_V_DIM), head)] + w_specs,
        out_shape=[jax.ShapeDtypeStruct((n_heads, n, DIFF_V_DIM), BF16)]
        + [jax.ShapeDtypeStruct(w.shape, BF16) for w in ffn_weights],
        scratch_shapes=[
            pltpu.VMEM((n_tiles, 2, TQ, DIFF_V_DIM), BF16),
            pltpu.VMEM((2, TK, TQ), F32),
            pltpu.VMEM((2, TK, TQ), F32),
            pltpu.VMEM((2, 1, TQ), F32),
            pltpu.VMEM((2, 1, TQ), F32),
            pltpu.VMEM((n_tiles, 2, 1, TQ), F32),
            pltpu.VMEM((n_tiles, 2, 1, TQ), F32),
            pltpu.VMEM((n_tiles, 2, DIFF_V_DIM, TQ), F32),
        ],
        compiler_params=pltpu.CompilerParams(
            dimension_semantics=("arbitrary", "arbitrary"), vmem_limit_bytes=VMEM_LIMIT_BYTES),
        name="diff_attention",
    )(lam_vecs, subln_w, q, k, vt, *ffn_weights)
    return outs[0], outs[1:]


def _out_proj_kernel(pool_ref, attn_ref, x_ref, w_ref, g_ref, h_ref, mix_ref, nscale_ref, attn_cat_ref):
    pw = pool_ref.shape[1]
    for hd in range(attn_ref.shape[0]):
        attn_cat_ref[:, hd * DIFF_V_DIM:(hd + 1) * DIFF_V_DIM] = attn_ref[hd]
    for c in range(0, mix_ref.shape[1], N_CHUNK):
        mix_ref[:, c:c + N_CHUNK] = (
            jnp.dot(pool_ref[...], w_ref[:pw, c:c + N_CHUNK], preferred_element_type=F32)
            + jnp.dot(attn_cat_ref[...], w_ref[pw:, c:c + N_CHUNK], preferred_element_type=F32))

    def store_h(rows, y):
        h_ref[rows, :] = x_ref[rows, :] + y

    _rms_norm_rows(mix_ref, g_ref, nscale_ref, store_h)


def _out_proj(pool_out, attn_out, x2, w_out_bf, g):
    n, d = x2.shape
    n_heads = attn_out.shape[0]
    tm = TM_OUT
    row = lambda i: (i, 0)
    const = lambda i: (0, 0)
    return pl.pallas_call(
        _out_proj_kernel,
        grid=(n // tm,),
        in_specs=[
            pl.BlockSpec((tm, pool_out.shape[1]), row),
            pl.BlockSpec((n_heads, tm, DIFF_V_DIM), lambda i: (0, i, 0)),
            pl.BlockSpec((tm, d), row),
            pl.BlockSpec(w_out_bf.shape, const),
            pl.BlockSpec((1, d), const),
        ],
        out_specs=pl.BlockSpec((tm, d), row),
        out_shape=jax.ShapeDtypeStruct((n, d), F32),
        scratch_shapes=[
            pltpu.VMEM((tm, d), F32),
            pltpu.VMEM((tm, 1), F32),
            pltpu.VMEM((tm, n_heads * DIFF_V_DIM), BF16),
        ],
        compiler_params=pltpu.CompilerParams(
            dimension_semantics=("arbitrary",), vmem_limit_bytes=VMEM_LIMIT_BYTES),
        name="out_proj",
    )(pool_out, attn_out, x2, w_out_bf, g)


def _ffn_kernel(h_ref, gpre_ref, gpost_ref, wg_ref, wu_ref, wd_ref, o_ref, hn_ref, nscale_ref, act_ref):
    f = pl.program_id(1)

    @pl.when(f == 0)
    def _():
        def store_hn(rows, y):
            hn_ref[rows, :] = y.astype(BF16)
            o_ref[rows, :] = jnp.zeros_like(y)

        _rms_norm_rows(h_ref, gpre_ref, nscale_ref, store_hn)

    hn = hn_ref[...]
    for cc in range(0, wg_ref.shape[1], FFN_ACT_CHUNK):
        gate = jnp.dot(hn, wg_ref[:, cc:cc + FFN_ACT_CHUNK], preferred_element_type=F32)
        up = jnp.dot(hn, wu_ref[:, cc:cc + FFN_ACT_CHUNK], preferred_element_type=F32)
        act_ref[:, cc:cc + FFN_ACT_CHUNK] = (gate * jax.nn.sigmoid(gate) * up).astype(BF16)
    for c in range(0, o_ref.shape[1], N_CHUNK):
        o_ref[:, c:c + N_CHUNK] += jnp.dot(act_ref[...], wd_ref[:, c:c + N_CHUNK], preferred_element_type=F32)

    @pl.when(f == pl.num_programs(1) - 1)
    def _():
        def store_out(rows, y):
            o_ref[rows, :] = h_ref[rows, :] + y

        _rms_norm_rows(o_ref, gpost_ref, nscale_ref, store_out)


def _ffn(h, g_pre, g_post, w_gate, w_up, w_down):
    n, d = h.shape
    d_ff = w_gate.shape[1]
    tm, tf = TM_FFN, TF_FFN
    return pl.pallas_call(
        _ffn_kernel,
        grid=(n // tm, d_ff // tf),
        in_specs=[
            pl.BlockSpec((tm, d), lambda i, f: (i, 0)),
            pl.BlockSpec((1, d), lambda i, f: (0, 0)),
            pl.BlockSpec((1, d), lambda i, f: (0, 0)),
            pl.BlockSpec((d, tf), lambda i, f: (0, f)),
            pl.BlockSpec((d, tf), lambda i, f: (0, f)),
            pl.BlockSpec((tf, d), lambda i, f: (f, 0)),
        ],
        out_specs=pl.BlockSpec((tm, d), lambda i, f: (i, 0)),
        out_shape=jax.ShapeDtypeStruct((n, d), F32),
        scratch_shapes=[pltpu.VMEM((tm, d), BF16), pltpu.VMEM((tm, 1), F32), pltpu.VMEM((tm, tf), BF16)],
        compiler_params=pltpu.CompilerParams(
            dimension_semantics=("arbitrary", "arbitrary"), vmem_limit_bytes=VMEM_LIMIT_BYTES),
        name="ffn",
    )(h, g_pre, g_post, w_gate, w_up, w_down)


def kernel(x, positions, pre_mix_norm, post_mix_norm, w_in, pool_w, pool_scale,
           lam_q1, lam_k1, lam_q2, lam_k2, subln_w, w_out,
           pre_ffn_norm, post_ffn_norm, w_gate, w_up, w_down):
    batch, seq, d_model = x.shape
    depth = w_in.shape[0]
    pool_width = pool_scale.shape[1]
    attn_width = w_out.shape[1] - pool_width
    qk_width = (w_in.shape[2] - pool_width - attn_width) // 2
    n_heads = attn_width // DIFF_V_DIM
    assert qk_width == n_heads * 2 * DIFF_HEAD_DIM
    assert seq % TQ == 0 and seq % TM_IN == 0 and (batch * seq) % TM_FFN == 0
    assert TM_IN == TK

    n = batch * seq
    h = x.reshape(n, d_model)
    pos_rows = positions.reshape(n // TM_IN, 1, TM_IN)
    inv_freq = (ROPE_THETA ** (-jnp.arange(0, ROT_DIM, 2, dtype=F32) / ROT_DIM)).reshape(ROT_HALF, 1)

    for l in range(depth):
        lambda_init = _lambda_init(l)
        pool_out, q, k, vt, w_out_bf = _in_proj(
            h, pos_rows, inv_freq, pre_mix_norm[l].reshape(1, -1), w_in[l], pool_w[l],
            pool_scale[l].reshape(1, -1), w_out[l],
            seq=seq, pool_width=pool_width, qk_width=qk_width, attn_width=attn_width)
        lam_vecs = jnp.stack([lam_q1[l], lam_k1[l], lam_q2[l], lam_k2[l]]).astype(F32)
        attn_out, (wg_bf, wu_bf, wd_bf) = _attention(
            q, k, vt, lam_vecs, subln_w[l].reshape(1, -1), (w_gate[l], w_up[l], w_down[l]),
            batch=batch, seq=seq, n_heads=n_heads, lambda_init=lambda_init)
        h = _out_proj(pool_out, attn_out, h, w_out_bf, post_mix_norm[l].reshape(1, -1))
        h = _ffn(h, pre_ffn_norm[l].reshape(1, -1), post_ffn_norm[l].reshape(1, -1), wg_bf, wu_bf, wd_bf)
    return h.reshape(batch, seq, d_model)
```

```python
import functools
import math

import jax
import jax.numpy as jnp
from jax import lax
from jax.experimental import pallas as pl
from jax.experimental.pallas import tpu as pltpu

F32 = jnp.float32
BF16 = jnp.bfloat16

POOL_WINDOWS = (2, 4, 8, 16)
DIFF_HEAD_DIM = 64
DIFF_V_DIM = 2 * DIFF_HEAD_DIM
ROPE_THETA = 500000.0
ROT_DIM = DIFF_HEAD_DIM // 4
ROT_HALF = ROT_DIM // 2
NORM_EPS = 1e-6
NEG_INF = -1e30
LOG2_E = math.log2(math.e)

LANES = 128
BF16_ROWS = 16
POOL_HALO = 32
VMEM_LIMIT_BYTES = 56 * 1024 * 1024

TM_IN = 512
TQ = 1024
TK = 512
ATTN_CHUNK = 256
ATTN_UNROLL = 6
TM_OUT = 512
TM_FFN = 1024
TF_FFN = 512
FFN_ACT_CHUNK = 256
N_CHUNK = 512
NORM_ROWS = 128
IN_CAST_STEPS = 8


def _lambda_init(layer_idx):
    return 0.8 - 0.6 * math.exp(-0.3 * layer_idx)


def _rms_norm(xf, g):
    ms = jnp.mean(xf * xf, axis=-1, keepdims=True)
    return xf * lax.rsqrt(ms + NORM_EPS) * g


def _rms_norm_rows(src_ref, g_ref, scale_ref, emit):
    n = src_ref.shape[0]

    def scale_body(c, carry):
        rows = pl.ds(pl.multiple_of(c * NORM_ROWS, NORM_ROWS), NORM_ROWS)
        x = src_ref[rows, :]
        scale_ref[rows, :] = lax.rsqrt(jnp.mean(x * x, axis=-1, keepdims=True) + NORM_EPS)
        return carry

    lax.fori_loop(0, n // NORM_ROWS, scale_body, 0, unroll=True)

    def apply_body(c, carry):
        rows = pl.ds(pl.multiple_of(c * NORM_ROWS, NORM_ROWS), NORM_ROWS)
        emit(rows, src_ref[rows, :] * scale_ref[rows, :] * g_ref[...])
        return carry

    lax.fori_loop(0, n // NORM_ROWS, apply_body, 0, unroll=True)


def _in_proj_kernel(x_ref, pos_ref, invf_ref, g_ref, w_ref, pw_ref, ps_ref, wo_ref,
                    pool_ref, q_ref, k_ref, vt_ref, wo_bf_ref,
                    w_bf_ref, wvt_ref, trig_ref, hn_ref, u_ref, carry_ref, *, seq, pool_width, qk_width):
    tm = x_ref.shape[0]
    step = pl.program_id(0)
    rows_per_cast = w_ref.shape[0]
    n_cast = w_bf_ref.shape[0] // rows_per_cast
    n_direct = w_bf_ref.shape[1]
    tiles_per_cast = pos_ref.shape[0]
    lane = lax.broadcasted_iota(jnp.int32, (1, LANES), 1) % DIFF_HEAD_DIM

    @pl.when(step < n_cast)
    def _():
        r0 = pl.multiple_of(step * rows_per_cast, rows_per_cast)
        w_bf_ref[pl.ds(r0, rows_per_cast), :] = w_ref[:, :n_direct].astype(BF16)
        for t in range(tiles_per_cast):
            ang = invf_ref[...] * pos_ref[t].astype(F32)
            comp = jnp.concatenate(
                [jnp.cos(ang), jnp.sin(ang), jnp.zeros((DIFF_HEAD_DIM - ROT_DIM, tm), F32)], axis=0)
            trig_ref[step * tiles_per_cast + t] = jnp.concatenate([comp, comp], axis=0).T

    for t in range(n_cast):
        @pl.when(step == t)
        def _():
            wvt_ref[:, t * rows_per_cast:(t + 1) * rows_per_cast] = w_ref[:, n_direct:].T.astype(BF16)

    @pl.when(step >= n_cast)
    def _():
        tiles_per_seq = seq // tm
        ti = (step - n_cast) % tiles_per_seq

        wo_bf_ref[...] = wo_ref[...].astype(BF16)
        hn_ref[...] = _rms_norm(x_ref[...], g_ref[...]).astype(BF16)

        @pl.when(ti == 0)
        def _():
            carry_ref[...] = jnp.zeros_like(carry_ref)

        trig = trig_ref[step - n_cast]
        coef_self = jnp.where(lane < ROT_HALF, trig,
                              jnp.where(lane < ROT_DIM, pltpu.roll(trig, ROT_HALF, 1), 1.0))
        coef_lo = jnp.where((lane >= ROT_HALF) & (lane < ROT_DIM), trig, 0.0)
        coef_hi = jnp.where(lane < ROT_HALF, -pltpu.roll(trig, LANES - ROT_HALF, 1), 0.0)

        def rope_chunk(out_ref, chunk, c, scale):
            t = jnp.dot(hn_ref[...], w_bf_ref[:, chunk * N_CHUNK:(chunk + 1) * N_CHUNK],
                        preferred_element_type=F32)
            for h in range(N_CHUNK // LANES):
                th = t[:, h * LANES:(h + 1) * LANES]
                r = (th * coef_self + pltpu.roll(th, ROT_HALF, 1) * coef_lo
                     + pltpu.roll(th, LANES - ROT_HALF, 1) * coef_hi)
                if scale != 1.0:
                    r = r * scale
                out_ref[:, c + h * LANES:c + (h + 1) * LANES] = r.astype(BF16)

        gdim = pool_width // len(POOL_WINDOWS)
        t_in_seq = ti * tm + lax.broadcasted_iota(jnp.int32, (tm, 1), 0)

        def pool_group(g, w):
            c0 = g * gdim
            u = u_ref[:, c0:c0 + gdim]
            ext = jnp.concatenate([carry_ref[g], u], axis=0)
            carry_ref[g] = u[tm - POOL_HALO:, :]
            lvl, off, k = ext, 0, 1
            while k < w:
                new_off = min(off + 8, POOL_HALO)
                cur = lvl[new_off - off:, :]
                shifted = lvl[new_off - off - k: lvl.shape[0] - k, :]
                lvl, off, k = cur + shifted, new_off, 2 * k
            wsum = lvl[POOL_HALO - off:, :]
            cnt = jnp.minimum(t_in_seq + 1, w).astype(F32)
            pooled = (wsum / cnt - u).astype(BF16)
            mixed = jnp.dot(pooled, pw_ref[g].astype(BF16), preferred_element_type=F32)
            pool_ref[:, c0:c0 + gdim] = (mixed * ps_ref[:, c0:c0 + gdim]).astype(BF16)

        n_pool, n_qk = pool_width // N_CHUNK, qk_width // N_CHUNK
        for ci in range(n_pool):
            u_ref[:, ci * N_CHUNK:(ci + 1) * N_CHUNK] = jnp.dot(
                hn_ref[...], w_bf_ref[:, ci * N_CHUNK:(ci + 1) * N_CHUNK], preferred_element_type=F32)

        q_scale = DIFF_HEAD_DIM ** -0.5 * LOG2_E
        chunks = ([(q_ref, n_pool + ci, ci * N_CHUNK, q_scale) for ci in range(n_qk)]
                  + [(k_ref, n_pool + n_qk + ci, ci * N_CHUNK, 1.0) for ci in range(n_qk)])
        groups = list(enumerate(POOL_WINDOWS))
        for idx, args in enumerate(chunks):
            rope_chunk(*args)
            for g, w in groups[idx * len(groups) // len(chunks):(idx + 1) * len(groups) // len(chunks)]:
                pool_group(g, w)

        for c in range(0, vt_ref.shape[0], N_CHUNK):
            vt_ref[c:c + N_CHUNK, :] = lax.dot_general(
                wvt_ref[c:c + N_CHUNK, :], hn_ref[...], (((1,), (1,)), ((), ())),
                preferred_element_type=F32).astype(BF16)


def _in_proj(x2, pos2, inv_freq, g, w_in, pool_w, pool_scale, w_out, *, seq, pool_width, qk_width, attn_width):
    n, d = x2.shape
    tm = TM_IN
    n_groups = len(POOL_WINDOWS)
    gdim = pool_width // n_groups
    n_direct = pool_width + 2 * qk_width
    n_cast = IN_CAST_STEPS
    n_tiles = n // tm
    assert n_tiles % n_cast == 0
    tiles_per_cast = n_tiles // n_cast
    const = lambda s: (0, 0)
    row = lambda s: (jnp.maximum(s - n_cast, 0), 0)
    kern = functools.partial(_in_proj_kernel, seq=seq, pool_width=pool_width, qk_width=qk_width)
    return pl.pallas_call(
        kern,
        grid=(n_cast + n_tiles,),
        in_specs=[
            pl.BlockSpec((tm, d), row),
            pl.BlockSpec((tiles_per_cast, 1, tm), lambda s: (jnp.minimum(s, n_cast - 1), 0, 0)),
            pl.BlockSpec((ROT_HALF, 1), const),
            pl.BlockSpec((1, d), const),
            pl.BlockSpec((d // n_cast, w_in.shape[1]), lambda s: (jnp.minimum(s, n_cast - 1), 0)),
            pl.BlockSpec(pool_w.shape, lambda s: (0, 0, 0)),
            pl.BlockSpec((1, pool_width), const),
            pl.BlockSpec((w_out.shape[0] // n_tiles, w_out.shape[1]), row),
        ],
        out_specs=[
            pl.BlockSpec((tm, pool_width), row),
            pl.BlockSpec((tm, qk_width), row),
            pl.BlockSpec((tm, qk_width), row),
            pl.BlockSpec((None, attn_width, tm), lambda s: (jnp.maximum(s - n_cast, 0), 0, 0)),
            pl.BlockSpec((w_out.shape[0] // n_tiles, w_out.shape[1]), row),
        ],
        out_shape=[
            jax.ShapeDtypeStruct((n, pool_width), BF16),
            jax.ShapeDtypeStruct((n, qk_width), BF16),
            jax.ShapeDtypeStruct((n, qk_width), BF16),
            jax.ShapeDtypeStruct((n // tm, attn_width, tm), BF16),
            jax.ShapeDtypeStruct(w_out.shape, BF16),
        ],
        scratch_shapes=[
            pltpu.VMEM((d, n_direct), BF16),
            pltpu.VMEM((attn_width, d), BF16),
            pltpu.VMEM((n_tiles, tm, LANES), F32),
            pltpu.VMEM((tm, d), BF16),
            pltpu.VMEM((tm, pool_width), F32),
            pltpu.VMEM((n_groups, POOL_HALO, gdim), F32),
        ],
        compiler_params=pltpu.CompilerParams(
            dimension_semantics=("arbitrary",), vmem_limit_bytes=VMEM_LIMIT_BYTES),
        name="in_proj",
    )(x2, pos2, inv_freq, g, w_in, pool_w, pool_scale, w_out)


def _attn_kernel(lam_ref, sw_ref, q_ref, k_ref, vt_ref, wg_ref, wu_ref, wd_ref,
                 o_ref, wg_bf_ref, wu_bf_ref, wd_bf_ref,
                 qc_ref, s0_ref, s1_ref, mx0_ref, mx1_ref, m_ref, l_ref, acc_ref, *, lambda_init):
    seq = q_ref.shape[0]
    n_tiles = seq // TQ
    n_unmasked = n_tiles * (n_tiles - 1)
    assert n_tiles >= 2 and n_unmasked % ATTN_UNROLL == 0 and ATTN_UNROLL % 2 == 0
    assert all(w.shape[0] % (BF16_ROWS * (n_unmasked // ATTN_UNROLL)) == 0 for w in (wg_ref, wu_ref, wd_ref))
    s_slots, mx_slots = (s0_ref, s1_ref), (mx0_ref, mx1_ref)
    chunks = [slice(a, a + ATTN_CHUNK) for a in range(0, TQ, ATTN_CHUNK)]
    chunk_plan = {
        "full": [(ch, None, TK) for ch in chunks],
        "lo": [(ch, ch.start, min(TK, ch.stop)) if ch.start < TK else (ch, None, TK) for ch in chunks],
        "hi": [(ch, ch.start - TK, min(TK, ch.stop - TK)) for ch in chunks if ch.start >= TK],
    }

    def rows(i, size):
        return pl.ds(i * size if isinstance(i, int) else pl.multiple_of(i * size, size), size)

    def key_rows(j, n_keys):
        start = j * TK if isinstance(j, int) else pl.multiple_of(j * TK, TK)
        return pl.ds(start, n_keys)

    def scores_chunk(qi, j, slot, c, ch, offset, n_keys):
        s = lax.dot_general(k_ref[key_rows(j, n_keys), :], qc_ref[qi, c, ch, :], (((1,), (1,)), ((), ())),
                            preferred_element_type=F32)
        if offset is not None:
            kv = lax.broadcasted_iota(jnp.int32, s.shape, 0)
            r = lax.broadcasted_iota(jnp.int32, s.shape, 1)
            s = jnp.where(kv <= r + offset, s, NEG_INF)
        s_slots[slot][c, 0:n_keys, ch] = s
        mx_slots[slot][c, :, ch] = jnp.max(s, axis=0, keepdims=True)

    def exp_pv_chunk(qi, j, slot, c, ch, n_keys):
        m_prev = m_ref[qi, c, :, ch]
        m_new = jnp.maximum(m_prev, mx_slots[slot][c, :, ch])
        alpha = jnp.exp2(m_prev - m_new)
        p = jnp.exp2(s_slots[slot][c, 0:n_keys, ch] - m_new)
        l_ref[qi, c, :, ch] = alpha * l_ref[qi, c, :, ch] + jnp.sum(p, axis=0, keepdims=True)
        pv = jnp.dot(vt_ref[j, :, 0:n_keys], p.astype(BF16), preferred_element_type=F32)
        acc_ref[qi, c, :, ch] = alpha * acc_ref[qi, c, :, ch] + pv
        m_ref[qi, c, :, ch] = m_new

    def stage(cur, cur_kind, slot, nxt, nxt_kind):
        todo_next = chunk_plan[nxt_kind] if nxt is not None else []
        todo_cur = [(ch, n_keys) for ch, _, n_keys in chunk_plan[cur_kind]]
        for c in range(2):
            for i in range(max(len(todo_next), len(todo_cur))):
                if i < len(todo_next):
                    scores_chunk(nxt[0], nxt[1], 1 - slot, c, *todo_next[i])
                if i < len(todo_cur):
                    exp_pv_chunk(cur[0], cur[1], slot, c, *todo_cur[i])

    def finalize(qi):
        lam_v = lam_ref[...]
        lam = (jnp.exp(jnp.sum(lam_v[0:1] * lam_v[1:2], keepdims=True))
               - jnp.exp(jnp.sum(lam_v[2:3] * lam_v[3:4], keepdims=True)) + lambda_init)
        attn_t = acc_ref[qi, 0] / l_ref[qi, 0] - lam * (acc_ref[qi, 1] / l_ref[qi, 1])
        ms = jnp.mean(attn_t * attn_t, axis=0, keepdims=True)
        y = (attn_t * lax.rsqrt(ms + NORM_EPS)).T
        o_ref[rows(qi, TQ), :] = (y * sw_ref[...] * (1.0 - lambda_init)).astype(BF16)

    lane = lax.broadcasted_iota(jnp.int32, (TQ, DIFF_V_DIM), 1)
    for qi in range(n_tiles):
        q = q_ref[qi * TQ:(qi + 1) * TQ, :]
        zero = jnp.zeros_like(q)
        qc_ref[qi, 0] = jnp.where(lane < DIFF_HEAD_DIM, q, zero)
        qc_ref[qi, 1] = jnp.where(lane >= DIFF_HEAD_DIM, q, zero)
    m_ref[...] = jnp.full_like(m_ref, NEG_INF)
    l_ref[...] = jnp.zeros_like(l_ref)
    acc_ref[...] = jnp.zeros_like(acc_ref)

    def unmasked_pair(n):
        qi = 1
        for t in range(2, n_tiles):
            qi = qi + jnp.where(n >= t * (t - 1), 1, 0)
        return qi, n - qi * (qi - 1)

    for c in range(2):
        for plan in chunk_plan["full"]:
            scores_chunk(1, 0, 0, c, *plan)

    n_iters = n_unmasked // ATTN_UNROLL

    def convert_weights(it):
        for src, dst in ((wg_ref, wg_bf_ref), (wu_ref, wu_bf_ref), (wd_ref, wd_bf_ref)):
            per = src.shape[0] // n_iters
            rows_it = pl.ds(pl.multiple_of(it * per, per), per)
            dst[rows_it, :] = src[rows_it, :].astype(BF16)

    def unmasked_body(it, carry):
        convert_weights(it)
        for u in range(ATTN_UNROLL):
            n = it * ATTN_UNROLL + u
            stage(unmasked_pair(n), "full", u % 2, unmasked_pair(jnp.minimum(n + 1, n_unmasked - 1)), "full")
        return carry

    lax.fori_loop(0, n_iters, unmasked_body, 0)

    def masked_stages(qi, nxt_qi):
        stage((qi, 2 * qi), "lo", 0, (qi, 2 * qi + 1), "hi")
        stage((qi, 2 * qi + 1), "hi", 1, None if nxt_qi is None else (nxt_qi, 2 * nxt_qi), "lo")

    for c in range(2):
        for plan in chunk_plan["lo"]:
            scores_chunk(0, 0, 0, c, *plan)
    for qi in range(n_tiles):
        if qi > 0:
            finalize(qi - 1)
        masked_stages(qi, qi + 1 if qi + 1 < n_tiles else None)
    finalize(n_tiles - 1)


def _attention(q, k, vt, lam_vecs, subln_w, ffn_weights, *, batch, seq, n_heads, lambda_init):
    n, width = q.shape
    nk = seq // TK
    n_tiles = seq // TQ
    n_steps = batch * n_heads
    assert TQ == 2 * TK and seq % TQ == 0 and vt.shape == (n // TK, n_heads * DIFF_V_DIM, TK)
    kern = functools.partial(_attn_kernel, lambda_init=lambda_init)
    head = lambda b, h: (b, h)
    step_rows = lambda b, h: (b * n_heads + h, 0)
    w_specs = [pl.BlockSpec((w.shape[0] // n_steps, w.shape[1]), step_rows) for w in ffn_weights]
    outs = pl.pallas_call(
        kern,
        grid=(batch, n_heads),
        in_specs=[
            pl.BlockSpec(lam_vecs.shape, lambda b, h: (0, 0)),
            pl.BlockSpec((1, DIFF_V_DIM), lambda b, h: (0, 0)),
            pl.BlockSpec((seq, DIFF_V_DIM), head),
            pl.BlockSpec((seq, DIFF_V_DIM), head),
            pl.BlockSpec((nk, DIFF_V_DIM, TK), lambda b, h: (b, h, 0)),
        ] + w_specs,
        out_specs=[pl.BlockSpec((seq, DIFF_V_DIM), head)] + w_specs,
        out_shape=[jax.ShapeDtypeStruct((n, width), BF16)]
        + [jax.ShapeDtypeStruct(w.shape, BF16) for w in ffn_weights],
        scratch_shapes=[
            pltpu.VMEM((n_tiles, 2, TQ, DIFF_V_DIM), BF16),
            pltpu.VMEM((2, TK, TQ), F32),
            pltpu.VMEM((2, TK, TQ), F32),
            pltpu.VMEM((2, 1, TQ), F32),
            pltpu.VMEM((2, 1, TQ), F32),
            pltpu.VMEM((n_tiles, 2, 1, TQ), F32),
            pltpu.VMEM((n_tiles, 2, 1, TQ), F32),
            pltpu.VMEM((n_tiles, 2, DIFF_V_DIM, TQ), F32),
        ],
        compiler_params=pltpu.CompilerParams(
            dimension_semantics=("arbitrary", "arbitrary"), vmem_limit_bytes=VMEM_LIMIT_BYTES),
        name="diff_attention",
    )(lam_vecs, subln_w, q, k, vt, *ffn_weights)
    return outs[0], outs[1:]


def _out_proj_kernel(pool_ref, attn_ref, x_ref, w_ref, g_ref, h_ref, mix0_ref, mix1_ref, nscale_ref, *, n_tiles):
    i = pl.program_id(0)
    pw = pool_ref.shape[1]
    mix = (mix0_ref, mix1_ref)

    def project(mix_ref):
        for c in range(0, mix_ref.shape[1], N_CHUNK):
            mix_ref[:, c:c + N_CHUNK] = (
                jnp.dot(pool_ref[...], w_ref[:pw, c:c + N_CHUNK], preferred_element_type=F32)
                + jnp.dot(attn_ref[...], w_ref[pw:, c:c + N_CHUNK], preferred_element_type=F32))

    def store_h(rows, y):
        h_ref[rows, :] = x_ref[rows, :] + y

    def finish(mix_ref):
        _rms_norm_rows(mix_ref, g_ref, nscale_ref, store_h)

    @pl.when(i == 0)
    def _():
        project(mix[0])

    for parity in range(2):
        @pl.when((i > 0) & (i < n_tiles) & (i % 2 == parity))
        def _():
            project(mix[parity])
            finish(mix[1 - parity])

    @pl.when(i == n_tiles)
    def _():
        finish(mix[(n_tiles - 1) % 2])


def _out_proj(pool_out, attn_out, x2, w_out_bf, g):
    n, d = x2.shape
    tm = TM_OUT
    n_tiles = n // tm
    row = lambda i: (jnp.minimum(i, n_tiles - 1), 0)
    prev = lambda i: (jnp.maximum(i - 1, 0), 0)
    const = lambda i: (0, 0)
    return pl.pallas_call(
        functools.partial(_out_proj_kernel, n_tiles=n_tiles),
        grid=(n_tiles + 1,),
        in_specs=[
            pl.BlockSpec((tm, pool_out.shape[1]), row),
            pl.BlockSpec((tm, attn_out.shape[1]), row),
            pl.BlockSpec((tm, d), prev),
            pl.BlockSpec(w_out_bf.shape, const),
            pl.BlockSpec((1, d), const),
        ],
        out_specs=pl.BlockSpec((tm, d), prev),
        out_shape=jax.ShapeDtypeStruct((n, d), F32),
        scratch_shapes=[
            pltpu.VMEM((tm, d), F32),
            pltpu.VMEM((tm, d), F32),
            pltpu.VMEM((tm, 1), F32),
        ],
        compiler_params=pltpu.CompilerParams(
            dimension_semantics=("arbitrary",), vmem_limit_bytes=VMEM_LIMIT_BYTES),
        name="out_proj",
    )(pool_out, attn_out, x2, w_out_bf, g)


def _ffn_kernel(h_ref, gpre_ref, gpost_ref, wg_ref, wu_ref, wd_ref, o_ref, hn_ref, nscale_ref, act_ref):
    f = pl.program_id(1)

    @pl.when(f == 0)
    def _():
        def store_hn(rows, y):
            hn_ref[rows, :] = y.astype(BF16)
            o_ref[rows, :] = jnp.zeros_like(y)

        _rms_norm_rows(h_ref, gpre_ref, nscale_ref, store_hn)

    hn = hn_ref[...]
    for cc in range(0, wg_ref.shape[1], FFN_ACT_CHUNK):
        gate = jnp.dot(hn, wg_ref[:, cc:cc + FFN_ACT_CHUNK], preferred_element_type=F32)
        up = jnp.dot(hn, wu_ref[:, cc:cc + FFN_ACT_CHUNK], preferred_element_type=F32)
        act_ref[:, cc:cc + FFN_ACT_CHUNK] = (gate * jax.nn.sigmoid(gate) * up).astype(BF16)
    for c in range(0, o_ref.shape[1], N_CHUNK):
        o_ref[:, c:c + N_CHUNK] += jnp.dot(act_ref[...], wd_ref[:, c:c + N_CHUNK], preferred_element_type=F32)

    @pl.when(f == pl.num_programs(1) - 1)
    def _():
        def store_out(rows, y):
            o_ref[rows, :] = h_ref[rows, :] + y

        _rms_norm_rows(o_ref, gpost_ref, nscale_ref, store_out)


def _ffn(h, g_pre, g_post, w_gate, w_up, w_down):
    n, d = h.shape
    d_ff = w_gate.shape[1]
    tm, tf = TM_FFN, TF_FFN
    return pl.pallas_call(
        _ffn_kernel,
        grid=(n // tm, d_ff // tf),
        in_specs=[
            pl.BlockSpec((tm, d), lambda i, f: (i, 0)),
            pl.BlockSpec((1, d), lambda i, f: (0, 0)),
            pl.BlockSpec((1, d), lambda i, f: (0, 0)),
            pl.BlockSpec((d, tf), lambda i, f: (0, f)),
            pl.BlockSpec((d, tf), lambda i, f: (0, f)),
            pl.BlockSpec((tf, d), lambda i, f: (f, 0)),
        ],
        out_specs=pl.BlockSpec((tm, d), lambda i, f: (i, 0)),
        out_shape=jax.ShapeDtypeStruct((n, d), F32),
        scratch_shapes=[pltpu.VMEM((tm, d), BF16), pltpu.VMEM((tm, 1), F32), pltpu.VMEM((tm, tf), BF16)],
        compiler_params=pltpu.CompilerParams(
            dimension_semantics=("arbitrary", "arbitrary"), vmem_limit_bytes=VMEM_LIMIT_BYTES),
        name="ffn",
    )(h, g_pre, g_post, w_gate, w_up, w_down)


def kernel(x, positions, pre_mix_norm, post_mix_norm, w_in, pool_w, pool_scale,
           lam_q1, lam_k1, lam_q2, lam_k2, subln_w, w_out,
           pre_ffn_norm, post_ffn_norm, w_gate, w_up, w_down):
    batch, seq, d_model = x.shape
    depth = w_in.shape[0]
    pool_width = pool_scale.shape[1]
    attn_width = w_out.shape[1] - pool_width
    qk_width = (w_in.shape[2] - pool_width - attn_width) // 2
    n_heads = attn_width // DIFF_V_DIM
    assert qk_width == n_heads * 2 * DIFF_HEAD_DIM
    assert seq % TQ == 0 and seq % TM_IN == 0 and (batch * seq) % TM_FFN == 0
    assert TM_IN == TK

    n = batch * seq
    h = x.reshape(n, d_model)
    pos_rows = positions.reshape(n // TM_IN, 1, TM_IN)
    inv_freq = (ROPE_THETA ** (-jnp.arange(0, ROT_DIM, 2, dtype=F32) / ROT_DIM)).reshape(ROT_HALF, 1)

    for l in range(depth):
        lambda_init = _lambda_init(l)
        pool_out, q, k, vt, w_out_bf = _in_proj(
            h, pos_rows, inv_freq, pre_mix_norm[l].reshape(1, -1), w_in[l], pool_w[l],
            pool_scale[l].reshape(1, -1), w_out[l],
            seq=seq, pool_width=pool_width, qk_width=qk_width, attn_width=attn_width)
        lam_vecs = jnp.stack([lam_q1[l], lam_k1[l], lam_q2[l], lam_k2[l]]).astype(F32)
        attn_out, (wg_bf, wu_bf, wd_bf) = _attention(
            q, k, vt, lam_vecs, subln_w[l].reshape(1, -1), (w_gate[l], w_up[l], w_down[l]),
            batch=batch, seq=seq, n_heads=n_heads, lambda_init=lambda_init)
        h = _out_proj(pool_out, attn_out, h, w_out_bf, post_mix_norm[l].reshape(1, -1))
        h = _ffn(h, pre_ffn_norm[l].reshape(1, -1), post_ffn_norm[l].reshape(1, -1), wg_bf, wu_bf, wd_bf)
    return h.reshape(batch, seq, d_model)
```

```python
import functools
import math

import jax
import jax.numpy as jnp
from jax import lax
from jax.experimental import pallas as pl
from jax.experimental.pallas import tpu as pltpu

F32 = jnp.float32
BF16 = jnp.bfloat16

POOL_WINDOWS = (2, 4, 8, 16)
DIFF_HEAD_DIM = 64
DIFF_V_DIM = 2 * DIFF_HEAD_DIM
ROPE_THETA = 500000.0
ROT_DIM = DIFF_HEAD_DIM // 4
ROT_HALF = ROT_DIM // 2
NORM_EPS = 1e-6
NEG_INF = -1e30
LOG2_E = math.log2(math.e)

LANES = 128
BF16_ROWS = 16
POOL_HALO = 32
VMEM_LIMIT_BYTES = 56 * 1024 * 1024

TM_IN = 512
TQ = 1024
TK = 512
ATTN_CHUNK = 256
ATTN_UNROLL = 6
TM_OUT = 512
TM_FFN = 1024
TF_FFN = 512
FFN_ACT_CHUNK = 256
N_CHUNK = 512
NORM_ROWS = 128
IN_CAST_STEPS = 8


def _lambda_init(layer_idx):
    return 0.8 - 0.6 * math.exp(-0.3 * layer_idx)


def _rms_norm(xf, g):
    ms = jnp.mean(xf * xf, axis=-1, keepdims=True)
    return xf * lax.rsqrt(ms + NORM_EPS) * g


def _rms_norm_rows(src_ref, g_ref, scale_ref, emit):
    n = src_ref.shape[0]

    def scale_body(c, carry):
        rows = pl.ds(pl.multiple_of(c * NORM_ROWS, NORM_ROWS), NORM_ROWS)
        x = src_ref[rows, :]
        scale_ref[rows, :] = lax.rsqrt(jnp.mean(x * x, axis=-1, keepdims=True) + NORM_EPS)
        return carry

    lax.fori_loop(0, n // NORM_ROWS, scale_body, 0, unroll=True)

    def apply_body(c, carry):
        rows = pl.ds(pl.multiple_of(c * NORM_ROWS, NORM_ROWS), NORM_ROWS)
        emit(rows, src_ref[rows, :] * scale_ref[rows, :] * g_ref[...])
        return carry

    lax.fori_loop(0, n // NORM_ROWS, apply_body, 0, unroll=True)


def _in_proj_kernel(x_ref, pos_ref, invf_ref, g_ref, w_ref, pw_ref, ps_ref, wo_ref,
                    pool_ref, q_ref, k_ref, vt_ref, wo_bf_ref,
                    w_bf_ref, wvt_ref, trig_ref, hn_ref, u_ref, carry_ref, *, seq, pool_width, qk_width):
    tm = x_ref.shape[0]
    step = pl.program_id(0)
    rows_per_cast = w_ref.shape[0]
    n_cast = w_bf_ref.shape[0] // rows_per_cast
    n_direct = w_bf_ref.shape[1]
    tiles_per_cast = pos_ref.shape[0]
    lane = lax.broadcasted_iota(jnp.int32, (1, LANES), 1) % DIFF_HEAD_DIM

    @pl.when(step < n_cast)
    def _():
        r0 = pl.multiple_of(step * rows_per_cast, rows_per_cast)
        w_bf_ref[pl.ds(r0, rows_per_cast), :] = w_ref[:, :n_direct].astype(BF16)
        for t in range(tiles_per_cast):
            ang = invf_ref[...] * pos_ref[t].astype(F32)
            comp = jnp.concatenate(
                [jnp.cos(ang), jnp.sin(ang), jnp.zeros((DIFF_HEAD_DIM - ROT_DIM, tm), F32)], axis=0)
            trig_ref[step * tiles_per_cast + t] = jnp.concatenate([comp, comp], axis=0).T

    for t in range(n_cast):
        @pl.when(step == t)
        def _():
            wvt_ref[:, t * rows_per_cast:(t + 1) * rows_per_cast] = w_ref[:, n_direct:].T.astype(BF16)

    @pl.when(step >= n_cast)
    def _():
        tiles_per_seq = seq // tm
        ti = (step - n_cast) % tiles_per_seq

        wo_bf_ref[...] = wo_ref[...].astype(BF16)
        hn_ref[...] = _rms_norm(x_ref[...], g_ref[...]).astype(BF16)

        @pl.when(ti == 0)
        def _():
            carry_ref[...] = jnp.zeros_like(carry_ref)

        trig = trig_ref[step - n_cast]
        coef_self = jnp.where(lane < ROT_HALF, trig,
                              jnp.where(lane < ROT_DIM, pltpu.roll(trig, ROT_HALF, 1), 1.0))
        coef_lo = jnp.where((lane >= ROT_HALF) & (lane < ROT_DIM), trig, 0.0)
        coef_hi = jnp.where(lane < ROT_HALF, -pltpu.roll(trig, LANES - ROT_HALF, 1), 0.0)

        def rope_chunk(out_ref, chunk, c, scale):
            t = jnp.dot(hn_ref[...], w_bf_ref[:, chunk * N_CHUNK:(chunk + 1) * N_CHUNK],
                        preferred_element_type=F32)
            for h in range(N_CHUNK // LANES):
                th = t[:, h * LANES:(h + 1) * LANES]
                r = (th * coef_self + pltpu.roll(th, ROT_HALF, 1) * coef_lo
                     + pltpu.roll(th, LANES - ROT_HALF, 1) * coef_hi)
                if scale != 1.0:
                    r = r * scale
                out_ref[:, c + h * LANES:c + (h + 1) * LANES] = r.astype(BF16)

        gdim = pool_width // len(POOL_WINDOWS)
        t_in_seq = ti * tm + lax.broadcasted_iota(jnp.int32, (tm, 1), 0)

        def pool_group(g, w):
            c0 = g * gdim
            u = u_ref[:, c0:c0 + gdim]
            ext = jnp.concatenate([carry_ref[g], u], axis=0)
            carry_ref[g] = u[tm - POOL_HALO:, :]
            lvl, off, k = ext, 0, 1
            while k < w:
                new_off = min(off + 8, POOL_HALO)
                cur = lvl[new_off - off:, :]
                shifted = lvl[new_off - off - k: lvl.shape[0] - k, :]
                lvl, off, k = cur + shifted, new_off, 2 * k
            wsum = lvl[POOL_HALO - off:, :]
            cnt = jnp.minimum(t_in_seq + 1, w).astype(F32)
            pooled = (wsum / cnt - u).astype(BF16)
            mixed = jnp.dot(pooled, pw_ref[g].astype(BF16), preferred_element_type=F32)
            pool_ref[:, c0:c0 + gdim] = (mixed * ps_ref[:, c0:c0 + gdim]).astype(BF16)

        n_pool, n_qk = pool_width // N_CHUNK, qk_width // N_CHUNK
        for ci in range(n_pool):
            u_ref[:, ci * N_CHUNK:(ci + 1) * N_CHUNK] = jnp.dot(
                hn_ref[...], w_bf_ref[:, ci * N_CHUNK:(ci + 1) * N_CHUNK], preferred_element_type=F32)

        q_scale = DIFF_HEAD_DIM ** -0.5 * LOG2_E
        chunks = ([(q_ref, n_pool + ci, ci * N_CHUNK, q_scale) for ci in range(n_qk)]
                  + [(k_ref, n_pool + n_qk + ci, ci * N_CHUNK, 1.0) for ci in range(n_qk)])
        groups = list(enumerate(POOL_WINDOWS))
        for idx, args in enumerate(chunks):
            rope_chunk(*args)
            for g, w in groups[idx * len(groups) // len(chunks):(idx + 1) * len(groups) // len(chunks)]:
                pool_group(g, w)

        for c in range(0, vt_ref.shape[0], N_CHUNK):
            vt_ref[c:c + N_CHUNK, :] = lax.dot_general(
                wvt_ref[c:c + N_CHUNK, :], hn_ref[...], (((1,), (1,)), ((), ())),
                preferred_element_type=F32).astype(BF16)


def _in_proj(x2, pos2, inv_freq, g, w_in, pool_w, pool_scale, w_out, *, seq, pool_width, qk_width, attn_width):
    n, d = x2.shape
    tm = TM_IN
    n_groups = len(POOL_WINDOWS)
    gdim = pool_width // n_groups
    n_direct = pool_width + 2 * qk_width
    n_cast = IN_CAST_STEPS
    n_tiles = n // tm
    assert n_tiles % n_cast == 0
    tiles_per_cast = n_tiles // n_cast
    const = lambda s: (0, 0)
    row = lambda s: (jnp.maximum(s - n_cast, 0), 0)
    kern = functools.partial(_in_proj_kernel, seq=seq, pool_width=pool_width, qk_width=qk_width)
    return pl.pallas_call(
        kern,
        grid=(n_cast + n_tiles,),
        in_specs=[
            pl.BlockSpec((tm, d), row),
            pl.BlockSpec((tiles_per_cast, 1, tm), lambda s: (jnp.minimum(s, n_cast - 1), 0, 0)),
            pl.BlockSpec((ROT_HALF, 1), const),
            pl.BlockSpec((1, d), const),
            pl.BlockSpec((d // n_cast, w_in.shape[1]), lambda s: (jnp.minimum(s, n_cast - 1), 0)),
            pl.BlockSpec(pool_w.shape, lambda s: (0, 0, 0)),
            pl.BlockSpec((1, pool_width), const),
            pl.BlockSpec((w_out.shape[0] // n_tiles, w_out.shape[1]), row),
        ],
        out_specs=[
            pl.BlockSpec((tm, pool_width), row),
            pl.BlockSpec((tm, qk_width), row),
            pl.BlockSpec((tm, qk_width), row),
            pl.BlockSpec((None, attn_width, tm), lambda s: (jnp.maximum(s - n_cast, 0), 0, 0)),
            pl.BlockSpec((w_out.shape[0] // n_tiles, w_out.shape[1]), row),
        ],
        out_shape=[
            jax.ShapeDtypeStruct((n, pool_width), BF16),
            jax.ShapeDtypeStruct((n, qk_width), BF16),
            jax.ShapeDtypeStruct((n, qk_width), BF16),
            jax.ShapeDtypeStruct((n // tm, attn_width, tm), BF16),
            jax.ShapeDtypeStruct(w_out.shape, BF16),
        ],
        scratch_shapes=[
            pltpu.VMEM((d, n_direct), BF16),
            pltpu.VMEM((attn_width, d), BF16),
            pltpu.VMEM((n_tiles, tm, LANES), F32),
            pltpu.VMEM((tm, d), BF16),
            pltpu.VMEM((tm, pool_width), F32),
            pltpu.VMEM((n_groups, POOL_HALO, gdim), F32),
        ],
        compiler_params=pltpu.CompilerParams(
            dimension_semantics=("arbitrary",), vmem_limit_bytes=VMEM_LIMIT_BYTES),
        name="in_proj",
    )(x2, pos2, inv_freq, g, w_in, pool_w, pool_scale, w_out)


def _attn_kernel(lam_ref, sw_ref, q_ref, k_ref, vt_ref, wg_ref, wu_ref, wd_ref,
                 o_ref, wg_bf_ref, wu_bf_ref, wd_bf_ref,
                 qc_ref, s0_ref, s1_ref, mx0_ref, mx1_ref, m_ref, l_ref, acc_ref, *, lambda_init):
    seq = q_ref.shape[0]
    n_tiles = seq // TQ
    n_unmasked = n_tiles * (n_tiles - 1)
    assert n_tiles >= 2 and n_unmasked % ATTN_UNROLL == 0 and ATTN_UNROLL % 2 == 0
    assert all(w.shape[0] % (BF16_ROWS * (n_unmasked // ATTN_UNROLL)) == 0 for w in (wg_ref, wu_ref, wd_ref))
    s_slots, mx_slots = (s0_ref, s1_ref), (mx0_ref, mx1_ref)
    chunks = [slice(a, a + ATTN_CHUNK) for a in range(0, TQ, ATTN_CHUNK)]
    chunk_plan = {
        "full": [(ch, None, TK) for ch in chunks],
        "lo": [(ch, ch.start, min(TK, ch.stop)) if ch.start < TK else (ch, None, TK) for ch in chunks],
        "hi": [(ch, ch.start - TK, min(TK, ch.stop - TK)) for ch in chunks if ch.start >= TK],
    }

    def rows(i, size):
        return pl.ds(i * size if isinstance(i, int) else pl.multiple_of(i * size, size), size)

    def key_rows(j, n_keys):
        start = j * TK if isinstance(j, int) else pl.multiple_of(j * TK, TK)
        return pl.ds(start, n_keys)

    def scores_chunk(qi, j, slot, c, ch, offset, n_keys):
        s = lax.dot_general(k_ref[key_rows(j, n_keys), :], qc_ref[qi, c, ch, :], (((1,), (1,)), ((), ())),
                            preferred_element_type=F32)
        if offset is not None:
            kv = lax.broadcasted_iota(jnp.int32, s.shape, 0)
            r = lax.broadcasted_iota(jnp.int32, s.shape, 1)
            s = jnp.where(kv <= r + offset, s, NEG_INF)
        s_slots[slot][c, 0:n_keys, ch] = s
        mx_slots[slot][c, :, ch] = jnp.max(s, axis=0, keepdims=True)

    def exp_pv_chunk(qi, j, slot, c, ch, n_keys):
        m_prev = m_ref[qi, c, :, ch]
        m_new = jnp.maximum(m_prev, mx_slots[slot][c, :, ch])
        alpha = jnp.exp2(m_prev - m_new)
        p = jnp.exp2(s_slots[slot][c, 0:n_keys, ch] - m_new)
        l_ref[qi, c, :, ch] = alpha * l_ref[qi, c, :, ch] + jnp.sum(p, axis=0, keepdims=True)
        pv = jnp.dot(vt_ref[j, :, 0:n_keys], p.astype(BF16), preferred_element_type=F32)
        acc_ref[qi, c, :, ch] = alpha * acc_ref[qi, c, :, ch] + pv
        m_ref[qi, c, :, ch] = m_new

    def stage(cur, cur_kind, slot, nxt, nxt_kind):
        todo_next = chunk_plan[nxt_kind] if nxt is not None else []
        todo_cur = [(ch, n_keys) for ch, _, n_keys in chunk_plan[cur_kind]]
        for c in range(2):
            for i in range(max(len(todo_next), len(todo_cur))):
                if i < len(todo_next):
                    scores_chunk(nxt[0], nxt[1], 1 - slot, c, *todo_next[i])
                if i < len(todo_cur):
                    exp_pv_chunk(cur[0], cur[1], slot, c, *todo_cur[i])

    def finalize(qi):
        lam_v = lam_ref[...]
        lam = (jnp.exp(jnp.sum(lam_v[0:1] * lam_v[1:2], keepdims=True))
               - jnp.exp(jnp.sum(lam_v[2:3] * lam_v[3:4], keepdims=True)) + lambda_init)
        attn_t = acc_ref[qi, 0] / l_ref[qi, 0] - lam * (acc_ref[qi, 1] / l_ref[qi, 1])
        ms = jnp.mean(attn_t * attn_t, axis=0, keepdims=True)
        y = (attn_t * lax.rsqrt(ms + NORM_EPS)).T
        o_ref[rows(qi, TQ), :] = (y * sw_ref[...] * (1.0 - lambda_init)).astype(BF16)

    lane = lax.broadcasted_iota(jnp.int32, (TQ, DIFF_V_DIM), 1)
    for qi in range(n_tiles):
        q = q_ref[qi * TQ:(qi + 1) * TQ, :]
        zero = jnp.zeros_like(q)
        qc_ref[qi, 0] = jnp.where(lane < DIFF_HEAD_DIM, q, zero)
        qc_ref[qi, 1] = jnp.where(lane >= DIFF_HEAD_DIM, q, zero)
    m_ref[...] = jnp.full_like(m_ref, NEG_INF)
    l_ref[...] = jnp.zeros_like(l_ref)
    acc_ref[...] = jnp.zeros_like(acc_ref)

    def unmasked_pair(n):
        qi = 1
        for t in range(2, n_tiles):
            qi = qi + jnp.where(n >= t * (t - 1), 1, 0)
        return qi, n - qi * (qi - 1)

    for c in range(2):
        for plan in chunk_plan["full"]:
            scores_chunk(1, 0, 0, c, *plan)

    n_iters = n_unmasked // ATTN_UNROLL

    def convert_weights(it):
        for src, dst in ((wg_ref, wg_bf_ref), (wu_ref, wu_bf_ref), (wd_ref, wd_bf_ref)):
            per = src.shape[0] // n_iters
            rows_it = pl.ds(pl.multiple_of(it * per, per), per)
            dst[rows_it, :] = src[rows_it, :].astype(BF16)

    def unmasked_body(it, carry):
        convert_weights(it)
        for u in range(ATTN_UNROLL):
            n = it * ATTN_UNROLL + u
            stage(unmasked_pair(n), "full", u % 2, unmasked_pair(jnp.minimum(n + 1, n_unmasked - 1)), "full")
        return carry

    lax.fori_loop(0, n_iters, unmasked_body, 0)

    def masked_stages(qi, nxt_qi):
        stage((qi, 2 * qi), "lo", 0, (qi, 2 * qi + 1), "hi")
        stage((qi, 2 * qi + 1), "hi", 1, None if nxt_qi is None else (nxt_qi, 2 * nxt_qi), "lo")

    for c in range(2):
        for plan in chunk_plan["lo"]:
            scores_chunk(0, 0, 0, c, *plan)
    for qi in range(n_tiles):
        if qi > 0:
            finalize(qi - 1)
        masked_stages(qi, qi + 1 if qi + 1 < n_tiles else None)
    finalize(n_tiles - 1)


def _attention(q, k, vt, lam_vecs, subln_w, ffn_weights, *, batch, seq, n_heads, lambda_init):
    n, width = q.shape
    nk = seq // TK
    n_tiles = seq // TQ
    n_steps = batch * n_heads
    assert TQ == 2 * TK and seq % TQ == 0 and vt.shape == (n // TK, n_heads * DIFF_V_DIM, TK)
    kern = functools.partial(_attn_kernel, lambda_init=lambda_init)
    head = lambda b, h: (b, h)
    step_rows = lambda b, h: (b * n_heads + h, 0)
    w_specs = [pl.BlockSpec((w.shape[0] // n_steps, w.shape[1]), step_rows) for w in ffn_weights]
    outs = pl.pallas_call(
        kern,
        grid=(batch, n_heads),
        in_specs=[
            pl.BlockSpec(lam_vecs.shape, lambda b, h: (0, 0)),
            pl.BlockSpec((1, DIFF_V_DIM), lambda b, h: (0, 0)),
            pl.BlockSpec((seq, DIFF_V_DIM), head),
            pl.BlockSpec((seq, DIFF_V_DIM), head),
            pl.BlockSpec((nk, DIFF_V_DIM, TK), lambda b, h: (b, h, 0)),
        ] + w_specs,
        out_specs=[pl.BlockSpec((seq, DIFF_V_DIM), head)] + w_specs,
        out_shape=[jax.ShapeDtypeStruct((n, width), BF16)]
        + [jax.ShapeDtypeStruct(w.shape, BF16) for w in ffn_weights],
        scratch_shapes=[
            pltpu.VMEM((n_tiles, 2, TQ, DIFF_V_DIM), BF16),
            pltpu.VMEM((2, TK, TQ), F32),
            pltpu.VMEM((2, TK, TQ), F32),
            pltpu.VMEM((2, 1, TQ), F32),
            pltpu.VMEM((2, 1, TQ), F32),
            pltpu.VMEM((n_tiles, 2, 1, TQ), F32),
            pltpu.VMEM((n_tiles, 2, 1, TQ), F32),
            pltpu.VMEM((n_tiles, 2, DIFF_V_DIM, TQ), F32),
        ],
        compiler_params=pltpu.CompilerParams(
            dimension_semantics=("arbitrary", "arbitrary"), vmem_limit_bytes=VMEM_LIMIT_BYTES),
        name="diff_attention",
    )(lam_vecs, subln_w, q, k, vt, *ffn_weights)
    return outs[0], outs[1:]


def _out_proj_kernel(pool_ref, attn_ref, x_ref, w_ref, g_ref, h_ref, mix0_ref, mix1_ref, nscale_ref, *, n_tiles):
    i = pl.program_id(0)
    pw = pool_ref.shape[1]
    tm, d = mix0_ref.shape
    mix = (mix0_ref, mix1_ref)
    col_chunks = range(0, d, N_CHUNK)
    rows_per_chunk = tm // len(col_chunks)

    def project(mix_ref, c):
        mix_ref[:, c:c + N_CHUNK] = (
            jnp.dot(pool_ref[...], w_ref[:pw, c:c + N_CHUNK], preferred_element_type=F32)
            + jnp.dot(attn_ref[...], w_ref[pw:, c:c + N_CHUNK], preferred_element_type=F32))

    def row_scales(mix_ref):
        for r in range(0, tm, NORM_ROWS):
            y = mix_ref[r:r + NORM_ROWS, :]
            nscale_ref[r:r + NORM_ROWS, :] = lax.rsqrt(jnp.mean(y * y, axis=-1, keepdims=True) + NORM_EPS)

    def residual_rows(mix_ref, start, stop):
        for r in range(start, stop, NORM_ROWS):
            rows = slice(r, r + NORM_ROWS)
            h_ref[rows, :] = x_ref[rows, :] + mix_ref[rows, :] * nscale_ref[rows, :] * g_ref[...]

    @pl.when(i == 0)
    def _():
        for c in col_chunks:
            project(mix[0], c)

    for parity in range(2):
        @pl.when((i > 0) & (i < n_tiles) & (i % 2 == parity))
        def _():
            row_scales(mix[1 - parity])
            for ci, c in enumerate(col_chunks):
                project(mix[parity], c)
                residual_rows(mix[1 - parity], ci * rows_per_chunk, (ci + 1) * rows_per_chunk)

    @pl.when(i == n_tiles)
    def _():
        last = mix[(n_tiles - 1) % 2]
        row_scales(last)
        residual_rows(last, 0, tm)


def _out_proj(pool_out, attn_out, x2, w_out_bf, g):
    n, d = x2.shape
    tm = TM_OUT
    n_tiles = n // tm
    row = lambda i: (jnp.minimum(i, n_tiles - 1), 0)
    prev = lambda i: (jnp.maximum(i - 1, 0), 0)
    const = lambda i: (0, 0)
    return pl.pallas_call(
        functools.partial(_out_proj_kernel, n_tiles=n_tiles),
        grid=(n_tiles + 1,),
        in_specs=[
            pl.BlockSpec((tm, pool_out.shape[1]), row),
            pl.BlockSpec((tm, attn_out.shape[1]), row),
            pl.BlockSpec((tm, d), prev),
            pl.BlockSpec(w_out_bf.shape, const),
            pl.BlockSpec((1, d), const),
        ],
        out_specs=pl.BlockSpec((tm, d), prev),
        out_shape=jax.ShapeDtypeStruct((n, d), F32),
        scratch_shapes=[
            pltpu.VMEM((tm, d), F32),
            pltpu.VMEM((tm, d), F32),
            pltpu.VMEM((tm, 1), F32),
        ],
        compiler_params=pltpu.CompilerParams(
            dimension_semantics=("arbitrary",), vmem_limit_bytes=VMEM_LIMIT_BYTES),
        name="out_proj",
    )(pool_out, attn_out, x2, w_out_bf, g)


def _ffn_kernel(h_ref, gpre_ref, gpost_ref, wg_ref, wu_ref, wd_ref, o_ref, hn_ref, nscale_ref, act_ref):
    f = pl.program_id(1)

    @pl.when(f == 0)
    def _():
        def store_hn(rows, y):
            hn_ref[rows, :] = y.astype(BF16)
            o_ref[rows, :] = jnp.zeros_like(y)

        _rms_norm_rows(h_ref, gpre_ref, nscale_ref, store_hn)

    hn = hn_ref[...]
    for cc in range(0, wg_ref.shape[1], FFN_ACT_CHUNK):
        gate = jnp.dot(hn, wg_ref[:, cc:cc + FFN_ACT_CHUNK], preferred_element_type=F32)
        up = jnp.dot(hn, wu_ref[:, cc:cc + FFN_ACT_CHUNK], preferred_element_type=F32)
        act_ref[:, cc:cc + FFN_ACT_CHUNK] = (gate * jax.nn.sigmoid(gate) * up).astype(BF16)
    for c in range(0, o_ref.shape[1], N_CHUNK):
        o_ref[:, c:c + N_CHUNK] += jnp.dot(act_ref[...], wd_ref[:, c:c + N_CHUNK], preferred_element_type=F32)

    @pl.when(f == pl.num_programs(1) - 1)
    def _():
        def store_out(rows, y):
            o_ref[rows, :] = h_ref[rows, :] + y

        _rms_norm_rows(o_ref, gpost_ref, nscale_ref, store_out)


def _ffn(h, g_pre, g_post, w_gate, w_up, w_down):
    n, d = h.shape
    d_ff = w_gate.shape[1]
    tm, tf = TM_FFN, TF_FFN
    return pl.pallas_call(
        _ffn_kernel,
        grid=(n // tm, d_ff // tf),
        in_specs=[
            pl.BlockSpec((tm, d), lambda i, f: (i, 0)),
            pl.BlockSpec((1, d), lambda i, f: (0, 0)),
            pl.BlockSpec((1, d), lambda i, f: (0, 0)),
            pl.BlockSpec((d, tf), lambda i, f: (0, f)),
            pl.BlockSpec((d, tf), lambda i, f: (0, f)),
            pl.BlockSpec((tf, d), lambda i, f: (f, 0)),
        ],
        out_specs=pl.BlockSpec((tm, d), lambda i, f: (i, 0)),
        out_shape=jax.ShapeDtypeStruct((n, d), F32),
        scratch_shapes=[pltpu.VMEM((tm, d), BF16), pltpu.VMEM((tm, 1), F32), pltpu.VMEM((tm, tf), BF16)],
        compiler_params=pltpu.CompilerParams(
            dimension_semantics=("arbitrary", "arbitrary"), vmem_limit_bytes=VMEM_LIMIT_BYTES),
        name="ffn",
    )(h, g_pre, g_post, w_gate, w_up, w_down)


def kernel(x, positions, pre_mix_norm, post_mix_norm, w_in, pool_w, pool_scale,
           lam_q1, lam_k1, lam_q2, lam_k2, subln_w, w_out,
           pre_ffn_norm, post_ffn_norm, w_gate, w_up, w_down):
    batch, seq, d_model = x.shape
    depth = w_in.shape[0]
    pool_width = pool_scale.shape[1]
    attn_width = w_out.shape[1] - pool_width
    qk_width = (w_in.shape[2] - pool_width - attn_width) // 2
    n_heads = attn_width // DIFF_V_DIM
    assert qk_width == n_heads * 2 * DIFF_HEAD_DIM
    assert seq % TQ == 0 and seq % TM_IN == 0 and (batch * seq) % TM_FFN == 0
    assert TM_IN == TK

    n = batch * seq
    h = x.reshape(n, d_model)
    pos_rows = positions.reshape(n // TM_IN, 1, TM_IN)
    inv_freq = (ROPE_THETA ** (-jnp.arange(0, ROT_DIM, 2, dtype=F32) / ROT_DIM)).reshape(ROT_HALF, 1)

    for l in range(depth):
        lambda_init = _lambda_init(l)
        pool_out, q, k, vt, w_out_bf = _in_proj(
            h, pos_rows, inv_freq, pre_mix_norm[l].reshape(1, -1), w_in[l], pool_w[l],
            pool_scale[l].reshape(1, -1), w_out[l],
            seq=seq, pool_width=pool_width, qk_width=qk_width, attn_width=attn_width)
        lam_vecs = jnp.stack([lam_q1[l], lam_k1[l], lam_q2[l], lam_k2[l]]).astype(F32)
        attn_out, (wg_bf, wu_bf, wd_bf) = _attention(
            q, k, vt, lam_vecs, subln_w[l].reshape(1, -1), (w_gate[l], w_up[l], w_down[l]),
            batch=batch, seq=seq, n_heads=n_heads, lambda_init=lambda_init)
        h = _out_proj(pool_out, attn_out, h, w_out_bf, post_mix_norm[l].reshape(1, -1))
        h = _ffn(h, pre_ffn_norm[l].reshape(1, -1), post_ffn_norm[l].reshape(1, -1), wg_bf, wu_bf, wd_bf)
    return h.reshape(batch, seq, d_model)
```
